```python
import jax, jax.numpy as jnp
from jax import lax
import numpy as np

D_MODEL = 1024
BATCH = 16
SEQ = 256
DEPTH = 4
DEC_BATCH = 8
DEC_SEQ = 1024
PAST_LEN = 512

GRID_W = 64
D_A = D_MODEL
A_GROUPS = 4
A_CHUNK = 128
D_B = D_MODEL
POOL_WINDOWS = (2, 4, 8, 16)
B_GROUPS = len(POOL_WINDOWS)
B_GW = D_B // B_GROUPS
RET_HEADS = 4
RET_DV = D_MODEL // RET_HEADS
RET_DK = RET_DV // 2
RET_CHUNK = 128
PEER_HEADS = 8
PEER_NKEYS = 128
PEER_EXPERTS = PEER_NKEYS * PEER_NKEYS
PEER_DKEY = 256
PEER_TOPK = 16
PEER_BLOCK = 128
N_BRANCH = 3
EPS = 1e-6

IN_SPLITS = (D_A, D_A, D_B, RET_HEADS * RET_DK, RET_HEADS * RET_DK, RET_HEADS * RET_DV,
             RET_HEADS * RET_DV, RET_HEADS * RET_DV, N_BRANCH * D_MODEL)
IN_COLS = sum(IN_SPLITS)
IN_OFFSETS = tuple(int(o) for o in np.cumsum(IN_SPLITS)[:-1])

kernel_name = 'hybrid_gmlp_pool_retention_peer_flow_step'


def _rmsnorm(x, g):
    xf = x.astype(jnp.float32)
    y = xf * lax.rsqrt(jnp.mean(xf * xf, axis=-1, keepdims=True) + EPS)
    return (y * g.astype(jnp.float32)).astype(x.dtype)


def _layernorm(x, g):
    xf = x.astype(jnp.float32)
    xc = xf - jnp.mean(xf, axis=-1, keepdims=True)
    y = xc * lax.rsqrt(jnp.mean(xc * xc, axis=-1, keepdims=True) + EPS)
    return (y * g.astype(jnp.float32)).astype(x.dtype)


def _head_norm(o):
    return o * lax.rsqrt(jnp.mean(o * o, axis=-1, keepdims=True) + EPS)


def _gmlp_mix(u, v, ws, bs, ln_g):
    b_, n, _ = v.shape
    nc = n // A_CHUNK
    vn = _layernorm(v, ln_g).reshape(b_, nc, A_CHUNK, A_GROUPS, D_A // A_GROUPS)
    sv = jnp.einsum('gij,bcjgd->bcigd', ws, vn) + bs.T[:, :, None]
    return u * sv.reshape(b_, n, D_A).astype(u.dtype)


def _box_sum(x, w, axis):
    a = w // 2
    b = w - 1 - a
    n = x.shape[axis]
    pad = [(0, 0)] * x.ndim
    pad[axis] = (a + 1, b)
    cs = jnp.cumsum(jnp.pad(x, pad), axis=axis)
    return lax.slice_in_dim(cs, w, w + n, axis=axis) - lax.slice_in_dim(cs, 0, n, axis=axis)


def _pool_mix(z, pool_w, pool_scale, grid):
    b_, n, _ = z.shape
    zf = z.astype(jnp.float32)
    outs = []
    for gi, w in enumerate(POOL_WINDOWS):
        zg = zf[..., gi * B_GW:(gi + 1) * B_GW]
        if grid:
            rows = n // GRID_W
            s = _box_sum(_box_sum(zg.reshape(b_, rows, GRID_W, B_GW), w, 1), w, 2)
            cnt_r = _box_sum(jnp.ones((rows,), jnp.float32), w, 0)
            cnt_c = _box_sum(jnp.ones((GRID_W,), jnp.float32), w, 0)
            cnt = cnt_r[:, None] * cnt_c[None, :]
            mean = (s / cnt[None, :, :, None]).reshape(b_, n, B_GW)
        else:
            s = _box_sum(zg, w, 1)
            cnt = _box_sum(jnp.ones((n,), jnp.float32), w, 0)
            mean = s / cnt[None, :, None]
        outs.append(jnp.einsum('bnc,cd->bnd', mean - zg, pool_w[gi].astype(jnp.float32)))
    return (jnp.concatenate(outs, axis=-1) * pool_scale.astype(jnp.float32)).astype(z.dtype)


def _retention_dir(q, k, v, log_gamma, s0):
    b_, n = q.shape[:2]
    nc = n // RET_CHUNK
    cl = RET_CHUNK
    q = q.reshape(b_, nc, cl, RET_HEADS, RET_DK)
    k = k.reshape(b_, nc, cl, RET_HEADS, RET_DK)
    v = v.reshape(b_, nc, cl, RET_HEADS, RET_DV)
    pos = jnp.arange(cl, dtype=jnp.float32)
    rel = pos[:, None] - pos[None, :]
    dmask = jnp.where(rel >= 0, jnp.exp(log_gamma[:, None, None] * jnp.maximum(rel, 0.0)), 0.0)
    scores = jnp.einsum('bcihk,bcjhk->bchij', q, k) * dmask
    o_in = jnp.einsum('bchij,bcjhv->bcihv', scores, v)
    k_dec = jnp.exp((cl - 1.0 - pos)[:, None] * log_gamma[None, :])
    kv = jnp.einsum('bcjhk,jh,bcjhv->cbhkv', k, k_dec, v)
    chunk_decay = jnp.exp(cl * log_gamma)[None, :, None, None]

    def step(s, kv_c):
        return chunk_decay * s + kv_c, s

    s_fin, s_prev = lax.scan(step, s0, kv)
    q_dec = jnp.exp((pos + 1.0)[:, None] * log_gamma[None, :])
    o_x = jnp.einsum('bcihk,ih,cbhkv->bcihv', q, q_dec, s_prev)
    return (o_in + o_x).reshape(b_, n, RET_HEADS, RET_DV), s_fin


def _retention(q, k, v, g_f, g_b, log_gamma, s0_f, s0_b):
    o_f, s_f = _retention_dir(q, k, v, log_gamma[0], s0_f)
    flip = lambda t: jnp.flip(t, axis=1)
    o_b, s_b = _retention_dir(flip(q), flip(k), flip(v), log_gamma[1], s0_b)
    o_b = flip(o_b)
    y = jax.nn.silu(g_f) * _head_norm(o_f) + jax.nn.silu(g_b) * _head_norm(o_b)
    b_, n = q.shape[:2]
    return y.reshape(b_, n, RET_HEADS * RET_DV), s_f, s_b


def _peer(h, wq, subkeys, u_tab, v_tab):
    b_, n, d = h.shape
    xb = h.reshape(-1, PEER_BLOCK, d)

    def block(xt):
        t = xt.shape[0]
        qp = (xt @ wq).reshape(t, PEER_HEADS, 2, PEER_DKEY // 2)
        sc = jnp.einsum('thsk,hsnk->thsn', qp, subkeys).astype(jnp.float32)
        top_s, top_i = lax.top_k(sc, PEER_TOPK)
        cand_s = (top_s[:, :, 0, :, None] + top_s[:, :, 1, None, :]).reshape(t, PEER_HEADS, PEER_TOPK * PEER_TOPK)
        cand_i = (top_i[:, :, 0, :, None] * PEER_NKEYS + top_i[:, :, 1, None, :]).reshape(t, PEER_HEADS, PEER_TOPK * PEER_TOPK)
        best_s, best_p = lax.top_k(cand_s, PEER_TOPK)
        expert = jnp.take_along_axis(cand_i, best_p, axis=-1)
        g = jax.nn.softmax(best_s, axis=-1)
        act = jax.nn.gelu(jnp.einsum('thkd,td->thk', u_tab[expert], xt).astype(jnp.float32))
        wgt = (g * act).astype(xt.dtype)
        return jnp.einsum('thk,thkd->td', wgt, v_tab[expert]).astype(xt.dtype)

    return lax.map(block, xb).reshape(b_, n, d)


def _layer(x, cond, lp, grid, s0_f, s0_b):
    b_, n, _ = x.shape
    mod = (jax.nn.silu(cond) @ lp['w_mod'] + lp['b_mod'])[:, None, :]
    sh1, sc1, gt1, sh2, sc2, gt2 = jnp.split(mod.astype(x.dtype), 6, axis=-1)
    h = _rmsnorm(x, lp['norm_g'][0]) * (1 + sc1) + sh1
    proj = h @ lp['w_in']
    u, v, zb, q, k, rv, rg_f, rg_b, gm = jnp.split(proj, IN_OFFSETS, axis=-1)
    y_a = _gmlp_mix(jax.nn.gelu(u), jax.nn.gelu(v), lp['gmlp_ws'], lp['gmlp_b'], lp['gmlp_ln_g'])
    y_b = _pool_mix(zb, lp['pool_w'], lp['pool_scale'], grid)
    heads = lambda t, dh: t.reshape(b_, n, RET_HEADS, dh).astype(jnp.float32)
    log_gamma = jax.nn.log_sigmoid(lp['ret_decay'].astype(jnp.float32))
    y_c, s_f, s_b = _retention(heads(q, RET_DK), heads(k, RET_DK) * (RET_DK ** -0.5), heads(rv, RET_DV),
                               heads(rg_f, RET_DV), heads(rg_b, RET_DV), log_gamma, s0_f, s0_b)
    ga, gb, gc = jnp.split(jax.nn.sigmoid(gm), N_BRANCH, axis=-1)
    merged = ga * y_a + gb * y_b + gc * y_c.astype(x.dtype)
    x = x + gt1 * (merged @ lp['w_out'])
    h2 = _rmsnorm(x, lp['norm_g'][1]) * (1 + sc2) + sh2
    x = x + gt2 * _peer(h2, lp['peer_wq'], lp['peer_subkeys'], lp['peer_u'], lp['peer_v'])
    return x, s_f, s_b


def setup_inputs(seed: int = 0) -> dict:
    key = jax.random.key(seed)
    ks = jax.random.split(key, 21)
    f32 = jnp.float32
    nrm = lambda k, shape, s: jax.random.normal(k, shape, f32) * s
    base_logit = jnp.log(2.0 ** (5.0 + jnp.arange(RET_HEADS, dtype=f32)) - 1.0)
    return {
        'x_prompt': nrm(ks[0], (BATCH, SEQ, D_MODEL), 1.0),
        'x_sample': nrm(ks[1], (DEC_BATCH, DEC_SEQ, D_MODEL), 1.0),
        'state_ret': nrm(ks[2], (DEC_BATCH, DEPTH, 2, RET_HEADS, RET_DK, RET_DV), 1.0),
        'c': nrm(ks[3], (DEC_BATCH, D_MODEL), 1.0),
        'c_ctx': nrm(ks[4], (D_MODEL,), 1.0),
        'w_mod': nrm(ks[5], (DEPTH, D_MODEL, 6 * D_MODEL), 0.5 * D_MODEL ** -0.5),
        'b_mod': nrm(ks[6], (DEPTH, 6 * D_MODEL), 0.01),
        'norm_g': 1.0 + nrm(ks[7], (DEPTH, 2, D_MODEL), 0.02),
        'w_in': nrm(ks[8], (DEPTH, D_MODEL, IN_COLS), D_MODEL ** -0.5),
        'w_out': nrm(ks[9], (DEPTH, D_MODEL, D_MODEL), D_MODEL ** -0.5),
        'gmlp_ws': nrm(ks[10], (DEPTH, A_GROUPS, A_CHUNK, A_CHUNK), 0.5 * A_CHUNK ** -0.5),
        'gmlp_b': 1.0 + nrm(ks[11], (DEPTH, A_GROUPS, A_CHUNK), 0.02),
        'gmlp_ln_g': 1.0 + nrm(ks[12], (DEPTH, D_A), 0.02),
        'pool_w': nrm(ks[13], (DEPTH, B_GROUPS, B_GW, B_GW), B_GW ** -0.5),
        'pool_scale': 1.0 + nrm(ks[14], (DEPTH, D_B), 0.02),
        'ret_decay': base_logit[None, None, :] + nrm(ks[15], (DEPTH, 2, RET_HEADS), 0.1),
        'peer_wq': nrm(ks[16], (DEPTH, D_MODEL, PEER_HEADS * PEER_DKEY), D_MODEL ** -0.5),
        'peer_subkeys': nrm(ks[17], (DEPTH, PEER_HEADS, 2, PEER_NKEYS, PEER_DKEY // 2), (PEER_DKEY // 2) ** -0.5),
        'peer_u': nrm(ks[18], (DEPTH, PEER_EXPERTS, D_MODEL), D_MODEL ** -0.5),
        'peer_v': nrm(ks[19], (DEPTH, PEER_EXPERTS, D_MODEL), 0.5),
        'final_norm_g': 1.0 + nrm(ks[20], (D_MODEL,), 0.02),
    }


def reference(x_prompt, x_sample, state_ret, c, c_ctx, w_mod, b_mod, norm_g, w_in, w_out,
              gmlp_ws, gmlp_b, gmlp_ln_g, pool_w, pool_scale, ret_decay, peer_wq, peer_subkeys,
              peer_u, peer_v, final_norm_g):
    xc = x_prompt
    xs = x_sample
    zero_state = jnp.zeros((x_prompt.shape[0], RET_HEADS, RET_DK, RET_DV), jnp.float32)
    cond_ctx = c_ctx[None, :]
    new_states = []
    for l in range(DEPTH):
        lp = {
            'w_mod': w_mod[l], 'b_mod': b_mod[l], 'norm_g': norm_g[l], 'w_in': w_in[l],
            'w_out': w_out[l], 'gmlp_ws': gmlp_ws[l], 'gmlp_b': gmlp_b[l], 'gmlp_ln_g': gmlp_ln_g[l],
            'pool_w': pool_w[l], 'pool_scale': pool_scale[l], 'ret_decay': ret_decay[l],
            'peer_wq': peer_wq[l], 'peer_subkeys': peer_subkeys[l], 'peer_u': peer_u[l],
            'peer_v': peer_v[l],
        }
        xc, s_f, s_b = _layer(xc, cond_ctx, lp, False, zero_state, zero_state)
        new_states.append(jnp.stack([s_f, s_b], axis=1))
        s0 = state_ret[:, l].astype(jnp.float32)
        xs, _, _ = _layer(xs, c, lp, True, s0[:, 0], s0[:, 1])
    new_state_ret = jnp.stack(new_states, axis=1).astype(x_prompt.dtype)
    y_prompt = _rmsnorm(xc, final_norm_g)
    y_sample = _rmsnorm(xs, final_norm_g)
    return (y_prompt, y_sample, new_state_ret)
```

```python
import functools

import numpy as np
import jax
import jax.numpy as jnp
from jax import lax
from jax.experimental import pallas as pl
from jax.experimental.pallas import tpu as pltpu

F32 = jnp.float32
BF16 = jnp.bfloat16
I32 = jnp.int32

D_MODEL = 1024
A_GROUPS = 4
A_CHUNK = 128
POOL_WINDOWS = (2, 4, 8, 16)
B_GW = D_MODEL // len(POOL_WINDOWS)
GRID_W = 64
RET_HEADS = 4
RET_DV = D_MODEL // RET_HEADS
RET_DK = RET_DV // 2
RET_CHUNK = 128
PEER_HEADS = 8
PEER_NKEYS = 128
PEER_DKEY = 256
PEER_TOPK = 16
EPS = 1e-6
IN_COLS = 10 * D_MODEL

SUBLANES = 8
LANES = 128
M_PITCH = PEER_NKEYS + SUBLANES
VMEM_LIMIT = 56 * 1024 * 1024

MOD_ROWS = 16
PROJ_TB = 512
PROJ_TN = 2048
GMLP_TB = 256
MERGE_TB = 512
ROUTE_TB = 256
PEER_TB = 256
PEER_CH = 1024
PEER_SUB = 256


def _gelu(x):
    return 0.5 * x * (1.0 + jnp.tanh(0.7978845608028654 * (x + 0.044715 * (x * x * x))))


def _sigmoid(x):
    return 1.0 / (1.0 + jnp.exp(-x))


def _rms(x, g):
    return x * lax.rsqrt(jnp.mean(x * x, axis=-1, keepdims=True) + EPS) * g


def _params(*sem):
    return pltpu.CompilerParams(dimension_semantics=sem, vmem_limit_bytes=VMEM_LIMIT)


def _nt_dot(a, b):
    return lax.dot_general(a, b, (((1,), (1,)), ((), ())), preferred_element_type=F32)


def _mod_kernel(cond_ref, w_ref, b_ref, o_ref):
    c = cond_ref[...]
    s = (c * _sigmoid(c)).astype(BF16)
    o_ref[...] = jnp.dot(s, w_ref[...].astype(BF16), preferred_element_type=F32) + b_ref[...]


def _modulation(cond, w_mod, b_mod):
    depth = w_mod.shape[0]
    nj = w_mod.shape[2] // D_MODEL
    return pl.pallas_call(
        _mod_kernel,
        grid=(depth, nj),
        in_specs=[
            pl.BlockSpec((MOD_ROWS, D_MODEL), lambda l, j: (0, 0)),
            pl.BlockSpec((None, D_MODEL, D_MODEL), lambda l, j: (l, 0, j)),
            pl.BlockSpec((None, 1, D_MODEL), lambda l, j: (l, 0, j)),
        ],
        out_specs=pl.BlockSpec((None, MOD_ROWS, D_MODEL), lambda l, j: (l, 0, j)),
        out_shape=jax.ShapeDtypeStruct((depth, MOD_ROWS, nj * D_MODEL), F32),
        compiler_params=_params("parallel", "parallel"),
        name="mod",
    )(cond, w_mod, b_mod.reshape(depth, 1, nj * D_MODEL))


def _proj_kernel(x_ref, mod_ref, g_ref, w_ref, o_ref, h_ref):
    @pl.when(pl.program_id(1) == 0)
    def _():
        y = _rms(x_ref[...], g_ref[...])
        sh = mod_ref[:, 0:D_MODEL]
        sc = mod_ref[:, D_MODEL:2 * D_MODEL]
        h_ref[...] = (y * (1.0 + sc) + sh).astype(BF16)

    o_ref[...] = jnp.dot(h_ref[...], w_ref[...], preferred_element_type=F32)


def _proj(x, mod4, row_fn, layer, norm_g, w_in):
    n = x.shape[0]
    tb, tn = PROJ_TB, PROJ_TN
    return pl.pallas_call(
        _proj_kernel,
        grid=(n // tb, IN_COLS // tn),
        in_specs=[
            pl.BlockSpec((tb, D_MODEL), lambda i, j: (i, 0)),
            pl.BlockSpec((None, None, 1, 6 * D_MODEL), lambda i, j: (layer, row_fn(i * tb), 0, 0)),
            pl.BlockSpec((1, D_MODEL), lambda i, j: (0, 0)),
            pl.BlockSpec((D_MODEL, tn), lambda i, j: (0, j)),
        ],
        out_specs=pl.BlockSpec((tb, tn), lambda i, j: (i, j)),
        out_shape=jax.ShapeDtypeStruct((n, IN_COLS), F32),
        scratch_shapes=[pltpu.VMEM((tb, D_MODEL), BF16)],
        compiler_params=_params("parallel", "arbitrary"),
        name="proj",
    )(x, mod4, norm_g, w_in)


def _gmlp_kernel(u_ref, v_ref, ws_ref, bs_ref, lng_ref, o_ref):
    gw = D_MODEL // A_GROUPS
    for c in range(GMLP_TB // A_CHUNK):
        rows = slice(c * A_CHUNK, (c + 1) * A_CHUNK)
        v = _gelu(v_ref[rows, :])
        vc = v - jnp.mean(v, axis=-1, keepdims=True)
        vn = vc * lax.rsqrt(jnp.mean(vc * vc, axis=-1, keepdims=True) + EPS) * lng_ref[...]
        vnb = vn.astype(BF16)
        for g in range(A_GROUPS):
            cols = slice(g * gw, (g + 1) * gw)
            sv = jnp.dot(ws_ref[g], vnb[:, cols], preferred_element_type=F32) + bs_ref[:, g:g + 1]
            o_ref[rows, cols] = _gelu(u_ref[rows, cols]) * sv


def _gmlp(proj, ws, bs_t, ln_g):
    n = proj.shape[0]
    tb = GMLP_TB
    return pl.pallas_call(
        _gmlp_kernel,
        grid=(n // tb,),
        in_specs=[
            pl.BlockSpec((tb, D_MODEL), lambda i: (i, 0)),
            pl.BlockSpec((tb, D_MODEL), lambda i: (i, 1)),
            pl.BlockSpec((A_GROUPS, A_CHUNK, A_CHUNK), lambda i: (0, 0, 0)),
            pl.BlockSpec((A_CHUNK, A_GROUPS), lambda i: (0, 0)),
            pl.BlockSpec((1, D_MODEL), lambda i: (0, 0)),
        ],
        out_specs=pl.BlockSpec((tb, D_MODEL), lambda i: (i, 0)),
        out_shape=jax.ShapeDtypeStruct((n, D_MODEL), F32),
        compiler_params=_params("parallel"),
        name="gmlp",
    )(proj, proj, ws, bs_t, ln_g)


POOL_PAD = 16


def _window_count(pos, size, a, b):
    return jnp.minimum(pos + b, size - 1) - jnp.maximum(pos - a, 0) + 1


def _pool_kernel(z_ref, w_ref, scale_ref, o_ref, zp_ref, cp_ref, *, n, grid):
    rows = n // GRID_W
    rpad = cp_ref.shape[0] - n
    tok = lax.broadcasted_iota(I32, (n, 1), 0)
    for gi, w in enumerate(POOL_WINDOWS):
        a = w // 2
        b = w - 1 - a
        cols = slice(gi * B_GW, (gi + 1) * B_GW)
        z = z_ref[:, cols]
        zp_ref[0:POOL_PAD, :] = jnp.zeros((POOL_PAD, B_GW), F32)
        zp_ref[POOL_PAD + n:, :] = jnp.zeros((POOL_PAD, B_GW), F32)
        zp_ref[POOL_PAD:POOL_PAD + n, :] = z
        if grid:
            col = tok & (GRID_W - 1)
            row = tok >> (GRID_W.bit_length() - 1)
            s = jnp.zeros((n, B_GW), F32)
            for d in range(-a, b + 1):
                sh = zp_ref[POOL_PAD + d:POOL_PAD + d + n, :]
                ok = jnp.logical_and(col + d >= 0, col + d < GRID_W)
                s = s + jnp.where(ok, sh, 0.0)
            half = rpad // 2
            cp_ref[0:half, :] = jnp.zeros((half, B_GW), F32)
            cp_ref[half + n:, :] = jnp.zeros((half, B_GW), F32)
            cp_ref[half:half + n, :] = s
            s = jnp.zeros((n, B_GW), F32)
            for d in range(-a, b + 1):
                s = s + cp_ref[half + d * GRID_W:half + d * GRID_W + n, :]
            cnt = (_window_count(row, rows, a, b) * _window_count(col, GRID_W, a, b)).astype(F32)
        else:
            s = jnp.zeros((n, B_GW), F32)
            for d in range(-a, b + 1):
                s = s + zp_ref[POOL_PAD + d:POOL_PAD + d + n, :]
            cnt = _window_count(tok, n, a, b).astype(F32)
        diff = (s / cnt - z).astype(BF16)
        o_ref[:, cols] = jnp.dot(diff, w_ref[gi], preferred_element_type=F32) * scale_ref[:, cols]


def _pool(proj, pool_w, pool_scale, nseq, n, grid):
    rpad = 2 * (max(POOL_WINDOWS) // 2) * GRID_W if grid else 2 * SUBLANES
    return pl.pallas_call(
        functools.partial(_pool_kernel, n=n, grid=grid),
        grid=(nseq,),
        in_specs=[
            pl.BlockSpec((n, D_MODEL), lambda i: (i, 2)),
            pl.BlockSpec((len(POOL_WINDOWS), B_GW, B_GW), lambda i: (0, 0, 0)),
            pl.BlockSpec((1, D_MODEL), lambda i: (0, 0)),
        ],
        out_specs=pl.BlockSpec((n, D_MODEL), lambda i: (i, 0)),
        out_shape=jax.ShapeDtypeStruct((nseq * n, D_MODEL), F32),
        scratch_shapes=[pltpu.VMEM((n + 2 * POOL_PAD, B_GW), F32), pltpu.VMEM((n + rpad, B_GW), F32)],
        compiler_params=_params("parallel"),
        name="pool",
    )(proj, pool_w, pool_scale)


def _log_sigmoid(x):
    return jnp.minimum(x, 0.0) - jnp.log(1.0 + jnp.exp(-jnp.abs(x)))


def _ret_kernel(*refs, n, has_s0):
    if has_s0:
        q_ref, k_ref, v_ref, gf_ref, gb_ref, rd_ref, s0_ref, y_ref, s_ref, dm_ref = refs
        sfin_ref = None
    else:
        q_ref, k_ref, v_ref, gf_ref, gb_ref, rd_ref, y_ref, sfin_ref, s_ref, dm_ref = refs
        s0_ref = None
    cl = RET_CHUNK
    nc = n // cl
    pi = lax.broadcasted_iota(I32, (cl, cl), 0).astype(F32)
    pj = lax.broadcasted_iota(I32, (cl, cl), 1).astype(F32)
    pcol = lax.broadcasted_iota(I32, (cl, 1), 0).astype(F32)
    kscale = RET_DK ** -0.5

    for d in range(2):
        g_ref = gf_ref if d == 0 else gb_ref
        lgs = []
        for h in range(RET_HEADS):
            lg = _log_sigmoid(rd_ref[d:d + 1, h:h + 1])
            lgs.append(lg)
            rel = (pi - pj) if d == 0 else (pj - pi)
            dm_ref[h] = jnp.where(rel >= 0.0, jnp.exp(lg * jnp.maximum(rel, 0.0)), 0.0)
            if has_s0:
                s_ref[h] = s0_ref[d, h]
            else:
                s_ref[h] = jnp.zeros((RET_DK, RET_DV), F32)

        def chunk(ci, carry, d=d, g_ref=g_ref, lgs=lgs):
            c = ci if d == 0 else nc - 1 - ci
            rows = pl.ds(pl.multiple_of(c * cl, cl), cl)
            for h in range(RET_HEADS):
                lg = lgs[h]
                if d == 0:
                    qdec = jnp.exp(lg * (pcol + 1.0))
                    kdec = jnp.exp(lg * (cl - 1.0 - pcol))
                else:
                    qdec = jnp.exp(lg * (cl - pcol))
                    kdec = jnp.exp(lg * pcol)
                cdec = jnp.exp(lg * float(cl))
                q = q_ref[rows, h * RET_DK:(h + 1) * RET_DK]
                k = k_ref[rows, h * RET_DK:(h + 1) * RET_DK] * kscale
                vb = v_ref[rows, h * RET_DV:(h + 1) * RET_DV].astype(BF16)
                sc = _nt_dot(q.astype(BF16), k.astype(BF16)) * dm_ref[h]
                o = jnp.dot(sc.astype(BF16), vb, preferred_element_type=F32)
                s_prev = s_ref[h]
                o = o + jnp.dot((q * qdec).astype(BF16), s_prev.astype(BF16), preferred_element_type=F32)
                kd_t = jnp.transpose(k * kdec).astype(BF16)
                s_ref[h] = cdec * s_prev + jnp.dot(kd_t, vb, preferred_element_type=F32)
                on = o * lax.rsqrt(jnp.mean(o * o, axis=-1, keepdims=True) + EPS)
                g = g_ref[rows, h * RET_DV:(h + 1) * RET_DV]
                yv = g * _sigmoid(g) * on
                if d == 0:
                    y_ref[rows, h * RET_DV:(h + 1) * RET_DV] = yv
                else:
                    y_ref[rows, h * RET_DV:(h + 1) * RET_DV] += yv
            return carry

        lax.fori_loop(0, nc, chunk, 0)
        if not has_s0:
            for h in range(RET_HEADS):
                sfin_ref[d, h] = s_ref[h]


def _retention(proj, ret_decay, nseq, n, s0, layer):
    has_s0 = s0 is not None
    dkb = RET_HEADS * RET_DK
    in_specs = [
        pl.BlockSpec((n, dkb), lambda i: (i, 3 * D_MODEL // dkb)),
        pl.BlockSpec((n, dkb), lambda i: (i, 3 * D_MODEL // dkb + 1)),
        pl.BlockSpec((n, D_MODEL), lambda i: (i, 4)),
        pl.BlockSpec((n, D_MODEL), lambda i: (i, 5)),
        pl.BlockSpec((n, D_MODEL), lambda i: (i, 6)),
        pl.BlockSpec((2, RET_HEADS), lambda i: (0, 0)),
    ]
    args = [proj, proj, proj, proj, proj, ret_decay]
    y_shape = jax.ShapeDtypeStruct((nseq * n, D_MODEL), F32)
    y_spec = pl.BlockSpec((n, D_MODEL), lambda i: (i, 0))
    if has_s0:
        in_specs.append(pl.BlockSpec((None, None, 2, RET_HEADS, RET_DK, RET_DV), lambda i: (i, layer, 0, 0, 0, 0)))
        args.append(s0)
        out_shape, out_specs = y_shape, y_spec
    else:
        out_shape = (y_shape, jax.ShapeDtypeStruct((nseq, 2, RET_HEADS, RET_DK, RET_DV), F32))
        out_specs = (y_spec, pl.BlockSpec((None, 2, RET_HEADS, RET_DK, RET_DV), lambda i: (i, 0, 0, 0, 0)))
    return pl.pallas_call(
        functools.partial(_ret_kernel, n=n, has_s0=has_s0),
        grid=(nseq,),
        in_specs=in_specs,
        out_specs=out_specs,
        out_shape=out_shape,
        scratch_shapes=[pltpu.VMEM((RET_HEADS, RET_DK, RET_DV), F32),
                        pltpu.VMEM((RET_HEADS, RET_CHUNK, RET_CHUNK), F32)],
        compiler_params=_params("parallel"),
        name="ret",
    )(*args)


def _merge_kernel(x_ref, ga_ref, gb_ref, gc_ref, ya_ref, yb_ref, yc_ref, mod_ref, g2_ref, w_ref, xo_ref, h2_ref):
    merged = (_sigmoid(ga_ref[...]) * ya_ref[...] + _sigmoid(gb_ref[...]) * yb_ref[...]
              + _sigmoid(gc_ref[...]) * yc_ref[...])
    o = jnp.dot(merged.astype(BF16), w_ref[...], preferred_element_type=F32)
    gt1 = mod_ref[:, 2 * D_MODEL:3 * D_MODEL]
    sh2 = mod_ref[:, 3 * D_MODEL:4 * D_MODEL]
    sc2 = mod_ref[:, 4 * D_MODEL:5 * D_MODEL]
    xn = x_ref[...] + gt1 * o
    xo_ref[...] = xn
    h2_ref[...] = (_rms(xn, g2_ref[...]) * (1.0 + sc2) + sh2).astype(BF16)


def _merge(x, proj, ya, yb, yc, mod4, row_fn, layer, norm_g2, w_out):
    n = x.shape[0]
    tb = MERGE_TB
    tok = lambda c: pl.BlockSpec((tb, D_MODEL), lambda i: (i, c))
    return pl.pallas_call(
        _merge_kernel,
        grid=(n // tb,),
        in_specs=[
            tok(0), tok(7), tok(8), tok(9), tok(0), tok(0), tok(0),
            pl.BlockSpec((None, None, 1, 6 * D_MODEL), lambda i: (layer, row_fn(i * tb), 0, 0)),
            pl.BlockSpec((1, D_MODEL), lambda i: (0, 0)),
            pl.BlockSpec((D_MODEL, D_MODEL), lambda i: (0, 0)),
        ],
        out_specs=(tok(0), tok(0)),
        out_shape=(jax.ShapeDtypeStruct((n, D_MODEL), F32), jax.ShapeDtypeStruct((n, D_MODEL), BF16)),
        compiler_params=_params("parallel"),
        name="merge",
    )(x, proj, proj, proj, ya, yb, yc, mod4, norm_g2, w_out)


_CAND_PIECES = ((0, 1, 16), (1, 2, 8), (2, 3, 8), (3, 4, 8), (4, 5, 8), (5, 6, 8), (6, 7, 8), (7, 8, 8))
_CAND_TAIL = (8, 16)


def _cand_flat_index():
    flat = []
    for j1, _, nj2 in _CAND_PIECES:
        flat += [j1 * PEER_TOPK + j2 for j2 in range(nj2)]
    flat += [j1 * PEER_TOPK for j1 in range(*_CAND_TAIL)]
    return np.asarray(flat, np.int32)[:, None]


def _route_kernel(h_ref, wq_ref, sk_ref, flat_ref, e_ref, g_ref, q_scr, ts_scr, ti_scr, eo_scr, go_scr):
    tb = h_ref.shape[0]
    nk = PEER_NKEYS
    kd = PEER_DKEY // 2
    q = jnp.dot(h_ref[...], wq_ref[...], preferred_element_type=F32).astype(BF16)
    for hs in range(2 * PEER_HEADS):
        q_scr[hs] = q[:, hs * kd:(hs + 1) * kd]
    key_iota = lax.broadcasted_iota(I32, (nk, tb), 0)
    neg = -jnp.inf

    def side(hs, carry):
        x = _nt_dot(sk_ref[hs], q_scr[hs])
        vals, idxs = [], []
        for _ in range(PEER_TOPK):
            m = jnp.max(x, axis=0, keepdims=True)
            idx = jnp.min(jnp.where(x == m, key_iota, nk), axis=0, keepdims=True)
            vals.append(m)
            idxs.append(idx)
            x = jnp.where(key_iota == idx, neg, x)
        ts_scr[hs] = jnp.concatenate(vals, axis=0)
        ti_scr[hs] = jnp.concatenate(idxs, axis=0)
        return carry

    lax.fori_loop(0, 2 * PEER_HEADS, side, 0)

    flat = flat_ref[...]
    big = jnp.int32(2 ** 30)

    def head(h, carry):
        s1 = ts_scr[2 * h]
        s2 = ts_scr[2 * h + 1]
        e1 = ti_scr[2 * h] * nk
        e2 = ti_scr[2 * h + 1]
        cs, ce = [], []
        for j1, _, nj2 in _CAND_PIECES:
            cs.append(s1[j1:j1 + 1, :] + s2[0:nj2, :])
            ce.append(e1[j1:j1 + 1, :] + e2[0:nj2, :])
        lo, hi = _CAND_TAIL
        cs.append(s1[lo:hi, :] + s2[0:1, :])
        ce.append(e1[lo:hi, :] + e2[0:1, :])
        cand = jnp.concatenate(cs, axis=0)
        key = flat * (nk * nk) + jnp.concatenate(ce, axis=0)
        vals, exps = [], []
        for _ in range(PEER_TOPK):
            m = jnp.max(cand, axis=0, keepdims=True)
            ksel = jnp.min(jnp.where(cand == m, key, big), axis=0, keepdims=True)
            vals.append(m)
            exps.append(ksel & (nk * nk - 1))
            cand = jnp.where(key == ksel, neg, cand)
        best = jnp.concatenate(vals, axis=0)
        p = jnp.exp(best - best[0:1, :])
        go_scr[h] = p / jnp.sum(p, axis=0, keepdims=True)
        eo_scr[h] = jnp.concatenate(exps, axis=0)
        return carry

    lax.fori_loop(0, PEER_HEADS, head, 0)
    nsel = PEER_HEADS * PEER_TOPK
    e_ref[...] = jnp.transpose(eo_scr[...].reshape(nsel, tb))
    g_ref[...] = jnp.transpose(go_scr[...].reshape(nsel, tb))


def _route(h2, wq, subkeys):
    n = h2.shape[0]
    tb = ROUTE_TB
    nsel = PEER_HEADS * PEER_TOPK
    kd = PEER_DKEY // 2
    flat = _cand_flat_index()
    ncand = flat.shape[0]
    return pl.pallas_call(
        _route_kernel,
        grid=(n // tb,),
        in_specs=[
            pl.BlockSpec((tb, D_MODEL), lambda i: (i, 0)),
            pl.BlockSpec((D_MODEL, PEER_HEADS * PEER_DKEY), lambda i: (0, 0)),
            pl.BlockSpec((2 * PEER_HEADS, PEER_NKEYS, kd), lambda i: (0, 0, 0)),
            pl.BlockSpec((ncand, 1), lambda i: (0, 0)),
        ],
        out_specs=(pl.BlockSpec((tb, nsel), lambda i: (i, 0)), pl.BlockSpec((tb, nsel), lambda i: (i, 0))),
        out_shape=(jax.ShapeDtypeStruct((n, nsel), I32), jax.ShapeDtypeStruct((n, nsel), F32)),
        scratch_shapes=[
            pltpu.VMEM((2 * PEER_HEADS, tb, kd), BF16),
            pltpu.VMEM((2 * PEER_HEADS, PEER_TOPK, tb), F32),
            pltpu.VMEM((2 * PEER_HEADS, PEER_TOPK, tb), I32),
            pltpu.VMEM((PEER_HEADS, PEER_TOPK, tb), I32),
            pltpu.VMEM((PEER_HEADS, PEER_TOPK, tb), F32),
        ],
        compiler_params=_params("parallel"),
        name="route",
    )(h2, wq, subkeys, jnp.asarray(flat))


def _peer_kernel(x_ref, h_ref, e_ref, g_ref, mod_ref, fg_ref, u_ref, v_ref, o_ref, m_ref, acc_ref, *, final):
    tb = x_ref.shape[0]
    nk = PEER_NKEYS
    j = pl.program_id(1)

    @pl.when(j == 0)
    def _():
        sub_iota = lax.broadcasted_iota(I32, (nk, nk), 0)
        zero = jnp.zeros((nk, nk), BF16)

        def group(gi, carry):
            t0 = pl.multiple_of(gi * SUBLANES, SUBLANES)
            er = e_ref[pl.ds(t0, SUBLANES), :]
            gr = g_ref[pl.ds(t0, SUBLANES), :]
            for pair in range(SUBLANES // 2):
                at, bt = [], []
                for tt in (2 * pair, 2 * pair + 1):
                    e1 = er[tt:tt + 1, :]
                    at.append(jnp.where(sub_iota == (e1 >> 7), gr[tt:tt + 1, :], 0.0).astype(BF16))
                    bt.append(jnp.where(sub_iota == (e1 & (nk - 1)), 1.0, 0.0).astype(BF16))
                lhs = jnp.concatenate(at, axis=1)
                rhs_t = jnp.concatenate([jnp.concatenate([bt[0], zero], axis=1),
                                         jnp.concatenate([zero, bt[1]], axis=1)], axis=0)
                out = _nt_dot(lhs, rhs_t)
                r0 = pl.multiple_of((t0 + 2 * pair) * M_PITCH, SUBLANES)
                m_ref[pl.ds(r0, nk), :] = out[:, :nk]
                m_ref[pl.ds(r0 + M_PITCH, nk), :] = out[:, nk:]
            return carry

        lax.fori_loop(0, tb // SUBLANES, group, 0)
        acc_ref[...] = jnp.zeros_like(acc_ref)

    h = h_ref[...]
    per = PEER_SUB // nk
    for sub in range(PEER_CH // PEER_SUB):
        rows = slice(sub * PEER_SUB, (sub + 1) * PEER_SUB)
        s = _gelu(_nt_dot(h, u_ref[rows, :]))
        a0 = j * (PEER_CH // nk) + sub * per
        m = jnp.concatenate([m_ref[pl.ds(a0 + a, tb, stride=M_PITCH), :] for a in range(per)], axis=1)
        acc_ref[...] += jnp.dot((m * s).astype(BF16), v_ref[rows, :], preferred_element_type=F32)

    @pl.when(j == pl.num_programs(1) - 1)
    def _():
        gt2 = mod_ref[:, 5 * D_MODEL:6 * D_MODEL]
        xn = x_ref[...] + gt2 * acc_ref[...]
        o_ref[...] = _rms(xn, fg_ref[...]) if final else xn


def _peer(x, h2, e, g, mod4, row_fn, layer, final_g, u_tab, v_tab, final):
    n = x.shape[0]
    tb = PEER_TB
    nsel = PEER_HEADS * PEER_TOPK
    n_exp = u_tab.shape[0]
    return pl.pallas_call(
        functools.partial(_peer_kernel, final=final),
        grid=(n // tb, n_exp // PEER_CH),
        in_specs=[
            pl.BlockSpec((tb, D_MODEL), lambda i, j: (i, 0)),
            pl.BlockSpec((tb, D_MODEL), lambda i, j: (i, 0)),
            pl.BlockSpec((tb, nsel), lambda i, j: (i, 0)),
            pl.BlockSpec((tb, nsel), lambda i, j: (i, 0)),
            pl.BlockSpec((None, None, 1, 6 * D_MODEL), lambda i, j: (layer, row_fn(i * tb), 0, 0)),
            pl.BlockSpec((1, D_MODEL), lambda i, j: (0, 0)),
            pl.BlockSpec((PEER_CH, D_MODEL), lambda i, j: (j, 0)),
            pl.BlockSpec((PEER_CH, D_MODEL), lambda i, j: (j, 0)),
        ],
        out_specs=pl.BlockSpec((tb, D_MODEL), lambda i, j: (i, 0)),
        out_shape=jax.ShapeDtypeStruct((n, D_MODEL), F32),
        scratch_shapes=[pltpu.VMEM((tb * M_PITCH, LANES), F32), pltpu.VMEM((tb, D_MODEL), F32)],
        compiler_params=_params("parallel", "arbitrary"),
        name="peer",
    )(x, h2, e, g, mod4, final_g, u_tab, v_tab)


def kernel(x_prompt, x_sample, state_ret, c, c_ctx, w_mod, b_mod, norm_g, w_in, w_out, gmlp_ws, gmlp_b,
           gmlp_ln_g, pool_w, pool_scale, ret_decay, peer_wq, peer_subkeys, peer_u, peer_v, final_norm_g):
    batch, seq, d = x_prompt.shape
    dec_batch, dec_seq, _ = x_sample.shape
    depth = w_mod.shape[0]
    assert d == D_MODEL and dec_batch + 1 <= MOD_ROWS and dec_seq % GRID_W == 0

    cond = jnp.zeros((MOD_ROWS, d), F32).at[0].set(c_ctx).at[1:1 + dec_batch].set(c)
    mod = _modulation(cond, w_mod, b_mod)
    mod4 = mod.reshape(depth, MOD_ROWS, 1, 6 * d)
    ctx_row = lambda tok0: 0
    lat_row = lambda tok0: 1 + tok0 // dec_seq

    xc = x_prompt.reshape(batch * seq, d)
    xs = x_sample.reshape(dec_batch * dec_seq, d)
    fg = final_norm_g.reshape(1, d)
    new_states = []
    for l in range(depth):
        last = l == depth - 1
        w_in_l = w_in[l].astype(BF16)
        w_out_l = w_out[l].astype(BF16)
        ws_l = gmlp_ws[l].astype(BF16)
        bs_t = gmlp_b[l].T
        ln_g = gmlp_ln_g[l].reshape(1, d)
        pw_l = pool_w[l].astype(BF16)
        ps_l = pool_scale[l].reshape(1, d)
        wq_l = peer_wq[l].astype(BF16)
        sk_l = peer_subkeys[l].reshape(2 * PEER_HEADS, PEER_NKEYS, PEER_DKEY // 2).astype(BF16)
        u_l = peer_u[l].astype(BF16)
        v_l = peer_v[l].astype(BF16)
        g1 = norm_g[l, 0].reshape(1, d)
        g2 = norm_g[l, 1].reshape(1, d)

        def path(x, nseq, n, row_fn, grid, s0):
            proj = _proj(x, mod4, row_fn, l, g1, w_in_l)
            ya = _gmlp(proj, ws_l, bs_t, ln_g)
            yb = _pool(proj, pw_l, ps_l, nseq, n, grid)
            ret = _retention(proj, ret_decay[l], nseq, n, s0, l)
            yc, s_fin = (ret, None) if s0 is not None else ret
            xn, h2 = _merge(x, proj, ya, yb, yc, mod4, row_fn, l, g2, w_out_l)
            e, g = _route(h2, wq_l, sk_l)
            return _peer(xn, h2, e, g, mod4, row_fn, l, fg, u_l, v_l, last), s_fin

        xc, s_fin = path(xc, batch, seq, ctx_row, False, None)
        new_states.append(s_fin)
        xs, _ = path(xs, dec_batch, dec_seq, lat_row, True, state_ret)

    new_state_ret = jnp.stack(new_states, axis=1).astype(x_prompt.dtype)
    return (xc.reshape(batch, seq, d), xs.reshape(dec_batch, dec_seq, d), new_state_ret)
```

```python
import functools

import numpy as np
import jax
import jax.numpy as jnp
from jax import lax
from jax.experimental import pallas as pl
from jax.experimental.pallas import tpu as pltpu

F32 = jnp.float32
BF16 = jnp.bfloat16
I32 = jnp.int32

D_MODEL = 1024
A_GROUPS = 4
A_CHUNK = 128
POOL_WINDOWS = (2, 4, 8, 16)
B_GW = D_MODEL // len(POOL_WINDOWS)
GRID_W = 64
RET_HEADS = 4
RET_DV = D_MODEL // RET_HEADS
RET_DK = RET_DV // 2
RET_CHUNK = 128
PEER_HEADS = 8
PEER_NKEYS = 128
PEER_DKEY = 256
PEER_TOPK = 16
EPS = 1e-6
IN_COLS = 10 * D_MODEL

SUBLANES = 8
LANES = 128
VMEM_LIMIT = 56 * 1024 * 1024

MOD_ROWS = 16
PROJ_TB = 512
PROJ_TN = 2048
GMLP_TB = 256
MERGE_TB = 512
ROUTE_TB = 256
PEER_TB = 512
PEER_CH = 1024
PEER_SUB = 256
PEER_PASSES = 2
M_ROWS = PEER_NKEYS // PEER_PASSES
M_PITCH = M_ROWS + SUBLANES
M_GROUP = 32


def _gelu(x):
    return 0.5 * x * (1.0 + jnp.tanh(0.7978845608028654 * (x + 0.044715 * (x * x * x))))


def _sigmoid(x):
    return 1.0 / (1.0 + jnp.exp(-x))


def _rms(x, g):
    return x * lax.rsqrt(jnp.mean(x * x, axis=-1, keepdims=True) + EPS) * g


def _params(*sem):
    return pltpu.CompilerParams(dimension_semantics=sem, vmem_limit_bytes=VMEM_LIMIT)


def _tree(fn, xs):
    xs = list(xs)
    while len(xs) > 1:
        xs = [fn(xs[i], xs[i + 1]) for i in range(0, len(xs) - 1, 2)] + ([xs[-1]] if len(xs) % 2 else [])
    return xs[0]


def _nt_dot(a, b):
    return lax.dot_general(a, b, (((1,), (1,)), ((), ())), preferred_element_type=F32)


def _mod_kernel(cond_ref, w_ref, b_ref, o_ref):
    c = cond_ref[...]
    s = (c * _sigmoid(c)).astype(BF16)
    o_ref[...] = jnp.dot(s, w_ref[...].astype(BF16), preferred_element_type=F32) + b_ref[...]


def _modulation(cond, w_mod, b_mod):
    depth = w_mod.shape[0]
    nj = w_mod.shape[2] // D_MODEL
    return pl.pallas_call(
        _mod_kernel,
        grid=(depth, nj),
        in_specs=[
            pl.BlockSpec((MOD_ROWS, D_MODEL), lambda l, j: (0, 0)),
            pl.BlockSpec((None, D_MODEL, D_MODEL), lambda l, j: (l, 0, j)),
            pl.BlockSpec((None, 1, D_MODEL), lambda l, j: (l, 0, j)),
        ],
        out_specs=pl.BlockSpec((None, MOD_ROWS, D_MODEL), lambda l, j: (l, 0, j)),
        out_shape=jax.ShapeDtypeStruct((depth, MOD_ROWS, nj * D_MODEL), F32),
        compiler_params=_params("parallel", "parallel"),
        name="mod",
    )(cond, w_mod, b_mod.reshape(depth, 1, nj * D_MODEL))


def _proj_kernel(x_ref, mod_ref, g_ref, w_ref, o_ref, h_ref):
    @pl.when(pl.program_id(1) == 0)
    def _():
        y = _rms(x_ref[...], g_ref[...])
        sh = mod_ref[:, 0:D_MODEL]
        sc = mod_ref[:, D_MODEL:2 * D_MODEL]
        h_ref[...] = (y * (1.0 + sc) + sh).astype(BF16)

    o_ref[...] = jnp.dot(h_ref[...], w_ref[...], preferred_element_type=F32)


def _proj(x, mod4, row_fn, layer, norm_g, w_in):
    n = x.shape[0]
    tb, tn = PROJ_TB, PROJ_TN
    return pl.pallas_call(
        _proj_kernel,
        grid=(n // tb, IN_COLS // tn),
        in_specs=[
            pl.BlockSpec((tb, D_MODEL), lambda i, j: (i, 0)),
            pl.BlockSpec((None, None, 1, 6 * D_MODEL), lambda i, j: (layer, row_fn(i * tb), 0, 0)),
            pl.BlockSpec((1, D_MODEL), lambda i, j: (0, 0)),
            pl.BlockSpec((D_MODEL, tn), lambda i, j: (0, j)),
        ],
        out_specs=pl.BlockSpec((tb, tn), lambda i, j: (i, j)),
        out_shape=jax.ShapeDtypeStruct((n, IN_COLS), F32),
        scratch_shapes=[pltpu.VMEM((tb, D_MODEL), BF16)],
        compiler_params=_params("parallel", "arbitrary"),
        name="proj",
    )(x, mod4, norm_g, w_in)


def _gmlp_kernel(u_ref, v_ref, ws_ref, bs_ref, lng_ref, o_ref):
    gw = D_MODEL // A_GROUPS
    for c in range(GMLP_TB // A_CHUNK):
        rows = slice(c * A_CHUNK, (c + 1) * A_CHUNK)
        v = _gelu(v_ref[rows, :])
        vc = v - jnp.mean(v, axis=-1, keepdims=True)
        vn = vc * lax.rsqrt(jnp.mean(vc * vc, axis=-1, keepdims=True) + EPS) * lng_ref[...]
        vnb = vn.astype(BF16)
        for g in range(A_GROUPS):
            cols = slice(g * gw, (g + 1) * gw)
            sv = jnp.dot(ws_ref[g], vnb[:, cols], preferred_element_type=F32) + bs_ref[:, g:g + 1]
            o_ref[rows, cols] = _gelu(u_ref[rows, cols]) * sv


def _gmlp(proj, ws, bs_t, ln_g):
    n = proj.shape[0]
    tb = GMLP_TB
    return pl.pallas_call(
        _gmlp_kernel,
        grid=(n // tb,),
        in_specs=[
            pl.BlockSpec((tb, D_MODEL), lambda i: (i, 0)),
            pl.BlockSpec((tb, D_MODEL), lambda i: (i, 1)),
            pl.BlockSpec((A_GROUPS, A_CHUNK, A_CHUNK), lambda i: (0, 0, 0)),
            pl.BlockSpec((A_CHUNK, A_GROUPS), lambda i: (0, 0)),
            pl.BlockSpec((1, D_MODEL), lambda i: (0, 0)),
        ],
        out_specs=pl.BlockSpec((tb, D_MODEL), lambda i: (i, 0)),
        out_shape=jax.ShapeDtypeStruct((n, D_MODEL), F32),
        compiler_params=_params("parallel"),
        name="gmlp",
    )(proj, proj, ws, bs_t, ln_g)


POOL_PAD = 16


def _window_count(pos, size, a, b):
    return jnp.minimum(pos + b, size - 1) - jnp.maximum(pos - a, 0) + 1


def _pool_kernel(z_ref, w_ref, scale_ref, o_ref, zp_ref, cp_ref, *, n, grid):
    rows = n // GRID_W
    rpad = cp_ref.shape[0] - n
    tok = lax.broadcasted_iota(I32, (n, 1), 0)
    for gi, w in enumerate(POOL_WINDOWS):
        a = w // 2
        b = w - 1 - a
        cols = slice(gi * B_GW, (gi + 1) * B_GW)
        z = z_ref[:, cols]
        zp_ref[0:POOL_PAD, :] = jnp.zeros((POOL_PAD, B_GW), F32)
        zp_ref[POOL_PAD + n:, :] = jnp.zeros((POOL_PAD, B_GW), F32)
        zp_ref[POOL_PAD:POOL_PAD + n, :] = z
        if grid:
            col = tok & (GRID_W - 1)
            row = tok >> (GRID_W.bit_length() - 1)
            s = jnp.zeros((n, B_GW), F32)
            for d in range(-a, b + 1):
                sh = zp_ref[POOL_PAD + d:POOL_PAD + d + n, :]
                ok = jnp.logical_and(col + d >= 0, col + d < GRID_W)
                s = s + jnp.where(ok, sh, 0.0)
            half = rpad // 2
            cp_ref[0:half, :] = jnp.zeros((half, B_GW), F32)
            cp_ref[half + n:, :] = jnp.zeros((half, B_GW), F32)
            cp_ref[half:half + n, :] = s
            s = jnp.zeros((n, B_GW), F32)
            for d in range(-a, b + 1):
                s = s + cp_ref[half + d * GRID_W:half + d * GRID_W + n, :]
            cnt = (_window_count(row, rows, a, b) * _window_count(col, GRID_W, a, b)).astype(F32)
        else:
            s = jnp.zeros((n, B_GW), F32)
            for d in range(-a, b + 1):
                s = s + zp_ref[POOL_PAD + d:POOL_PAD + d + n, :]
            cnt = _window_count(tok, n, a, b).astype(F32)
        diff = (s / cnt - z).astype(BF16)
        o_ref[:, cols] = jnp.dot(diff, w_ref[gi], preferred_element_type=F32) * scale_ref[:, cols]


def _pool(proj, pool_w, pool_scale, nseq, n, grid):
    rpad = 2 * (max(POOL_WINDOWS) // 2) * GRID_W if grid else 2 * SUBLANES
    return pl.pallas_call(
        functools.partial(_pool_kernel, n=n, grid=grid),
        grid=(nseq,),
        in_specs=[
            pl.BlockSpec((n, D_MODEL), lambda i: (i, 2)),
            pl.BlockSpec((len(POOL_WINDOWS), B_GW, B_GW), lambda i: (0, 0, 0)),
            pl.BlockSpec((1, D_MODEL), lambda i: (0, 0)),
        ],
        out_specs=pl.BlockSpec((n, D_MODEL), lambda i: (i, 0)),
        out_shape=jax.ShapeDtypeStruct((nseq * n, D_MODEL), F32),
        scratch_shapes=[pltpu.VMEM((n + 2 * POOL_PAD, B_GW), F32), pltpu.VMEM((n + rpad, B_GW), F32)],
        compiler_params=_params("parallel"),
        name="pool",
    )(proj, pool_w, pool_scale)


def _log_sigmoid(x):
    return jnp.minimum(x, 0.0) - jnp.log(1.0 + jnp.exp(-jnp.abs(x)))


def _ret_kernel(*refs, n, has_s0):
    if has_s0:
        q_ref, k_ref, v_ref, gf_ref, gb_ref, rd_ref, s0_ref, y_ref, s_ref, dm_ref = refs
        sfin_ref = None
    else:
        q_ref, k_ref, v_ref, gf_ref, gb_ref, rd_ref, y_ref, sfin_ref, s_ref, dm_ref = refs
        s0_ref = None
    cl = RET_CHUNK
    nc = n // cl
    pi = lax.broadcasted_iota(I32, (cl, cl), 0).astype(F32)
    pj = lax.broadcasted_iota(I32, (cl, cl), 1).astype(F32)
    pcol = lax.broadcasted_iota(I32, (cl, 1), 0).astype(F32)
    kscale = RET_DK ** -0.5

    for d in range(2):
        g_ref = gf_ref if d == 0 else gb_ref
        lgs = []
        for h in range(RET_HEADS):
            lg = _log_sigmoid(rd_ref[d:d + 1, h:h + 1])
            lgs.append(lg)
            rel = (pi - pj) if d == 0 else (pj - pi)
            dm_ref[h] = jnp.where(rel >= 0.0, jnp.exp(lg * jnp.maximum(rel, 0.0)), 0.0)
            if has_s0:
                s_ref[h] = s0_ref[d, h]
            else:
                s_ref[h] = jnp.zeros((RET_DK, RET_DV), F32)

        def chunk(ci, carry, d=d, g_ref=g_ref, lgs=lgs):
            c = ci if d == 0 else nc - 1 - ci
            rows = pl.ds(pl.multiple_of(c * cl, cl), cl)
            for h in range(RET_HEADS):
                lg = lgs[h]
                if d == 0:
                    qdec = jnp.exp(lg * (pcol + 1.0))
                    kdec = jnp.exp(lg * (cl - 1.0 - pcol))
                else:
                    qdec = jnp.exp(lg * (cl - pcol))
                    kdec = jnp.exp(lg * pcol)
                cdec = jnp.exp(lg * float(cl))
                q = q_ref[rows, h * RET_DK:(h + 1) * RET_DK]
                k = k_ref[rows, h * RET_DK:(h + 1) * RET_DK] * kscale
                vb = v_ref[rows, h * RET_DV:(h + 1) * RET_DV].astype(BF16)
                sc = _nt_dot(q.astype(BF16), k.astype(BF16)) * dm_ref[h]
                o = jnp.dot(sc.astype(BF16), vb, preferred_element_type=F32)
                s_prev = s_ref[h]
                o = o + jnp.dot((q * qdec).astype(BF16), s_prev.astype(BF16), preferred_element_type=F32)
                kd_t = jnp.transpose(k * kdec).astype(BF16)
                s_ref[h] = cdec * s_prev + jnp.dot(kd_t, vb, preferred_element_type=F32)
                on = o * lax.rsqrt(jnp.mean(o * o, axis=-1, keepdims=True) + EPS)
                g = g_ref[rows, h * RET_DV:(h + 1) * RET_DV]
                yv = g * _sigmoid(g) * on
                if d == 0:
                    y_ref[rows, h * RET_DV:(h + 1) * RET_DV] = yv
                else:
                    y_ref[rows, h * RET_DV:(h + 1) * RET_DV] += yv
            return carry

        lax.fori_loop(0, nc, chunk, 0)
        if not has_s0:
            for h in range(RET_HEADS):
                sfin_ref[d, h] = s_ref[h]


def _retention(proj, ret_decay, nseq, n, s0, layer):
    has_s0 = s0 is not None
    dkb = RET_HEADS * RET_DK
    in_specs = [
        pl.BlockSpec((n, dkb), lambda i: (i, 3 * D_MODEL // dkb)),
        pl.BlockSpec((n, dkb), lambda i: (i, 3 * D_MODEL // dkb + 1)),
        pl.BlockSpec((n, D_MODEL), lambda i: (i, 4)),
        pl.BlockSpec((n, D_MODEL), lambda i: (i, 5)),
        pl.BlockSpec((n, D_MODEL), lambda i: (i, 6)),
        pl.BlockSpec((2, RET_HEADS), lambda i: (0, 0)),
    ]
    args = [proj, proj, proj, proj, proj, ret_decay]
    y_shape = jax.ShapeDtypeStruct((nseq * n, D_MODEL), F32)
    y_spec = pl.BlockSpec((n, D_MODEL), lambda i: (i, 0))
    if has_s0:
        in_specs.append(pl.BlockSpec((None, None, 2, RET_HEADS, RET_DK, RET_DV), lambda i: (i, layer, 0, 0, 0, 0)))
        args.append(s0)
        out_shape, out_specs = y_shape, y_spec
    else:
        out_shape = (y_shape, jax.ShapeDtypeStruct((nseq, 2, RET_HEADS, RET_DK, RET_DV), F32))
        out_specs = (y_spec, pl.BlockSpec((None, 2, RET_HEADS, RET_DK, RET_DV), lambda i: (i, 0, 0, 0, 0)))
    return pl.pallas_call(
        functools.partial(_ret_kernel, n=n, has_s0=has_s0),
        grid=(nseq,),
        in_specs=in_specs,
        out_specs=out_specs,
        out_shape=out_shape,
        scratch_shapes=[pltpu.VMEM((RET_HEADS, RET_DK, RET_DV), F32),
                        pltpu.VMEM((RET_HEADS, RET_CHUNK, RET_CHUNK), F32)],
        compiler_params=_params("parallel"),
        name="ret",
    )(*args)


def _merge_kernel(x_ref, ga_ref, gb_ref, gc_ref, ya_ref, yb_ref, yc_ref, mod_ref, g2_ref, w_ref, xo_ref, h2_ref):
    merged = (_sigmoid(ga_ref[...]) * ya_ref[...] + _sigmoid(gb_ref[...]) * yb_ref[...]
              + _sigmoid(gc_ref[...]) * yc_ref[...])
    o = jnp.dot(merged.astype(BF16), w_ref[...], preferred_element_type=F32)
    gt1 = mod_ref[:, 2 * D_MODEL:3 * D_MODEL]
    sh2 = mod_ref[:, 3 * D_MODEL:4 * D_MODEL]
    sc2 = mod_ref[:, 4 * D_MODEL:5 * D_MODEL]
    xn = x_ref[...] + gt1 * o
    xo_ref[...] = xn
    h2_ref[...] = (_rms(xn, g2_ref[...]) * (1.0 + sc2) + sh2).astype(BF16)


def _merge(x, proj, ya, yb, yc, mod4, row_fn, layer, norm_g2, w_out):
    n = x.shape[0]
    tb = MERGE_TB
    tok = lambda c: pl.BlockSpec((tb, D_MODEL), lambda i: (i, c))
    return pl.pallas_call(
        _merge_kernel,
        grid=(n // tb,),
        in_specs=[
            tok(0), tok(7), tok(8), tok(9), tok(0), tok(0), tok(0),
            pl.BlockSpec((None, None, 1, 6 * D_MODEL), lambda i: (layer, row_fn(i * tb), 0, 0)),
            pl.BlockSpec((1, D_MODEL), lambda i: (0, 0)),
            pl.BlockSpec((D_MODEL, D_MODEL), lambda i: (0, 0)),
        ],
        out_specs=(tok(0), tok(0)),
        out_shape=(jax.ShapeDtypeStruct((n, D_MODEL), F32), jax.ShapeDtypeStruct((n, D_MODEL), BF16)),
        compiler_params=_params("parallel"),
        name="merge",
    )(x, proj, proj, proj, ya, yb, yc, mod4, norm_g2, w_out)


_CAND_PIECES = ((0, 1, 16), (1, 2, 8), (2, 3, 8), (3, 4, 8), (4, 5, 8), (5, 6, 8), (6, 7, 8), (7, 8, 8))
_CAND_TAIL = (8, 16)


def _cand_flat_index():
    flat = []
    for j1, _, nj2 in _CAND_PIECES:
        flat += [j1 * PEER_TOPK + j2 for j2 in range(nj2)]
    flat += [j1 * PEER_TOPK for j1 in range(*_CAND_TAIL)]
    return np.asarray(flat, np.int32)[:, None]


def _route_kernel(h_ref, wq_ref, sk_ref, flat_ref, e_ref, g_ref, q_scr, ts_scr, ti_scr, eo_scr, go_scr):
    tb = h_ref.shape[0]
    nk = PEER_NKEYS
    kd = PEER_DKEY // 2
    nlt = tb // LANES
    q = jnp.dot(h_ref[...], wq_ref[...], preferred_element_type=F32).astype(BF16)
    for hs in range(2 * PEER_HEADS):
        q_scr[hs] = q[:, hs * kd:(hs + 1) * kd]
    base_iota = lax.broadcasted_iota(I32, (SUBLANES, LANES), 0)
    neg = -jnp.inf

    def top_keys(scores):
        nb = nk // SUBLANES
        iotas = [base_iota + r * SUBLANES for r in range(nb)]
        chains = [[x[r * SUBLANES:(r + 1) * SUBLANES, :] for r in range(nb)] for x in scores]
        vals = [[] for _ in scores]
        idxs = [[] for _ in scores]
        for _ in range(PEER_TOPK):
            ms = [jnp.max(_tree(jnp.maximum, xs), axis=0, keepdims=True) for xs in chains]
            firsts = [_tree(jnp.minimum, [jnp.where(xr == m, ir, nk) for xr, ir in zip(xs, iotas)])
                      for xs, m in zip(chains, ms)]
            ids = [jnp.min(f, axis=0, keepdims=True) for f in firsts]
            chains = [[jnp.where(ir == idx, neg, xr) for xr, ir in zip(xs, iotas)]
                      for xs, idx in zip(chains, ids)]
            for ci in range(len(scores)):
                vals[ci].append(ms[ci])
                idxs[ci].append(ids[ci])
        return [(jnp.concatenate(v, axis=0), jnp.concatenate(i, axis=0)) for v, i in zip(vals, idxs)]

    def sides(li, carry):
        h = li // nlt
        lt = li % nlt
        rows = pl.ds(pl.multiple_of(lt * LANES, LANES), LANES)
        tops = top_keys([_nt_dot(sk_ref[2 * h + s], q_scr[2 * h + s, rows, :]) for s in range(2)])
        for s, (vals, idxs) in enumerate(tops):
            ts_scr[lt, 2 * h + s] = vals
            ti_scr[lt, 2 * h + s] = idxs
        return carry

    lax.fori_loop(0, PEER_HEADS * nlt, sides, 0)

    flat = flat_ref[...]
    big = jnp.int32(2 ** 30)

    def head(h, carry):
        for lt in range(nlt):
            s1 = ts_scr[lt, 2 * h]
            s2 = ts_scr[lt, 2 * h + 1]
            e1 = ti_scr[lt, 2 * h] * nk
            e2 = ti_scr[lt, 2 * h + 1]
            cands, ces = [], []
            for j1, _, nj2 in _CAND_PIECES:
                for r in range(0, nj2, SUBLANES):
                    cands.append(s1[j1:j1 + 1, :] + s2[r:r + SUBLANES, :])
                    ces.append(e1[j1:j1 + 1, :] + e2[r:r + SUBLANES, :])
            lo, hi = _CAND_TAIL
            cands.append(s1[lo:hi, :] + s2[0:1, :])
            ces.append(e1[lo:hi, :] + e2[0:1, :])
            keys = [flat[r * SUBLANES:(r + 1) * SUBLANES, :] * (nk * nk) + ce for r, ce in enumerate(ces)]
            vals, exps = [], []
            for _ in range(PEER_TOPK):
                m = jnp.max(_tree(jnp.maximum, cands), axis=0, keepdims=True)
                first = _tree(jnp.minimum, [jnp.where(cr == m, kr, big) for cr, kr in zip(cands, keys)])
                ksel = jnp.min(first, axis=0, keepdims=True)
                vals.append(m)
                exps.append(ksel & (nk * nk - 1))
                cands = [jnp.where(kr == ksel, neg, cr) for cr, kr in zip(cands, keys)]
            best = jnp.concatenate(vals, axis=0)
            p = jnp.exp(best - best[0:1, :])
            go_scr[lt, h] = p / jnp.sum(p, axis=0, keepdims=True)
            eo_scr[lt, h] = jnp.concatenate(exps, axis=0)
        return carry

    lax.fori_loop(0, PEER_HEADS, head, 0)
    nsel = PEER_HEADS * PEER_TOPK
    for lt in range(nlt):
        rows = slice(lt * LANES, (lt + 1) * LANES)
        e_ref[rows, :] = jnp.transpose(eo_scr[lt].reshape(nsel, LANES))
        g_ref[rows, :] = jnp.transpose(go_scr[lt].reshape(nsel, LANES))


def _route(h2, wq, subkeys):
    n = h2.shape[0]
    tb = ROUTE_TB
    nsel = PEER_HEADS * PEER_TOPK
    kd = PEER_DKEY // 2
    flat = _cand_flat_index()
    ncand = flat.shape[0]
    return pl.pallas_call(
        _route_kernel,
        grid=(n // tb,),
        in_specs=[
            pl.BlockSpec((tb, D_MODEL), lambda i: (i, 0)),
            pl.BlockSpec((D_MODEL, PEER_HEADS * PEER_DKEY), lambda i: (0, 0)),
            pl.BlockSpec((2 * PEER_HEADS, PEER_NKEYS, kd), lambda i: (0, 0, 0)),
            pl.BlockSpec((ncand, 1), lambda i: (0, 0)),
        ],
        out_specs=(pl.BlockSpec((tb, nsel), lambda i: (i, 0)), pl.BlockSpec((tb, nsel), lambda i: (i, 0))),
        out_shape=(jax.ShapeDtypeStruct((n, nsel), I32), jax.ShapeDtypeStruct((n, nsel), F32)),
        scratch_shapes=[
            pltpu.VMEM((2 * PEER_HEADS, tb, kd), BF16),
            pltpu.VMEM((tb // LANES, 2 * PEER_HEADS, PEER_TOPK, LANES), F32),
            pltpu.VMEM((tb // LANES, 2 * PEER_HEADS, PEER_TOPK, LANES), I32),
            pltpu.VMEM((tb // LANES, PEER_HEADS, PEER_TOPK, LANES), I32),
            pltpu.VMEM((tb // LANES, PEER_HEADS, PEER_TOPK, LANES), F32),
        ],
        compiler_params=_params("parallel"),
        name="route",
    )(h2, wq, subkeys, jnp.asarray(flat))


def _peer_kernel(x_ref, h_ref, e_ref, g_ref, mod_ref, fg_ref, u_ref, v_ref, o_ref, m_ref, p_ref, acc_ref, *, final):
    tb = x_ref.shape[0]
    nk = PEER_NKEYS
    c = pl.program_id(1)
    n_chunks = pl.num_programs(1) - 1
    chunks_per_pass = n_chunks // PEER_PASSES

    @pl.when(jnp.logical_and(c % chunks_per_pass == 0, c < n_chunks))
    def _():
        a_iota = lax.broadcasted_iota(I32, (M_ROWS, nk), 0) + (c // chunks_per_pass) * M_ROWS
        b_iota = lax.broadcasted_iota(I32, (nk, nk), 0)
        zero = jnp.zeros((M_ROWS, nk), BF16)

        def group(gi, carry):
            t0 = pl.multiple_of(gi * M_GROUP, M_GROUP)
            er = e_ref[pl.ds(t0, M_GROUP), :]
            gr = g_ref[pl.ds(t0, M_GROUP), :]
            for pair in range(M_GROUP // 2):
                at, bt = [], []
                for tt in (2 * pair, 2 * pair + 1):
                    e1 = er[tt:tt + 1, :]
                    at.append(jnp.where(a_iota == (e1 >> 7), gr[tt:tt + 1, :], 0.0).astype(BF16))
                    bt.append(jnp.where(b_iota == (e1 & (nk - 1)), 1.0, 0.0).astype(BF16))
                lhs = jnp.concatenate([jnp.concatenate([at[0], zero], axis=1),
                                       jnp.concatenate([zero, at[1]], axis=1)], axis=0)
                out = _nt_dot(lhs, jnp.concatenate(bt, axis=1))
                r0 = pl.multiple_of((t0 + 2 * pair) * M_PITCH, SUBLANES)
                m_ref[pl.ds(r0, M_ROWS), :] = out[:M_ROWS, :]
                m_ref[pl.ds(r0 + M_PITCH, M_ROWS), :] = out[M_ROWS:, :]
            return carry

        lax.fori_loop(0, tb // M_GROUP, group, 0)

    slot = c % 2

    def mix_previous():
        acc_ref[...] += jnp.dot(p_ref[1 - slot], v_ref[...], preferred_element_type=F32)

    def stage_current():
        h = h_ref[...]
        per = PEER_SUB // nk
        a_base = (c % chunks_per_pass) * (PEER_CH // nk)
        for sub in range(PEER_CH // PEER_SUB):
            rows = slice(sub * PEER_SUB, (sub + 1) * PEER_SUB)
            s = _gelu(_nt_dot(h, u_ref[rows, :]))
            m = jnp.concatenate(
                [m_ref[pl.ds(a_base + sub * per + a, tb, stride=M_PITCH), :] for a in range(per)], axis=1)
            p_ref[slot, :, rows] = (m * s).astype(BF16)

    @pl.when(c == 0)
    def _():
        acc_ref[...] = jnp.zeros_like(acc_ref)
        stage_current()

    @pl.when(jnp.logical_and(c > 0, c < n_chunks))
    def _():
        mix_previous()
        stage_current()

    @pl.when(c == n_chunks)
    def _():
        mix_previous()
        gt2 = mod_ref[:, 5 * D_MODEL:6 * D_MODEL]
        xn = x_ref[...] + gt2 * acc_ref[...]
        o_ref[...] = _rms(xn, fg_ref[...]) if final else xn


def _peer(x, h2, e, g, mod4, row_fn, layer, final_g, u_tab, v_tab, final):
    n = x.shape[0]
    tb = PEER_TB
    nsel = PEER_HEADS * PEER_TOPK
    n_chunks = u_tab.shape[0] // PEER_CH
    return pl.pallas_call(
        functools.partial(_peer_kernel, final=final),
        grid=(n // tb, n_chunks + 1),
        in_specs=[
            pl.BlockSpec((tb, D_MODEL), lambda i, c: (i, 0)),
            pl.BlockSpec((tb, D_MODEL), lambda i, c: (i, 0)),
            pl.BlockSpec((tb, nsel), lambda i, c: (i, 0)),
            pl.BlockSpec((tb, nsel), lambda i, c: (i, 0)),
            pl.BlockSpec((None, None, 1, 6 * D_MODEL), lambda i, c: (layer, row_fn(i * tb), 0, 0)),
            pl.BlockSpec((1, D_MODEL), lambda i, c: (0, 0)),
            pl.BlockSpec((PEER_CH, D_MODEL), lambda i, c: (jnp.minimum(c, n_chunks - 1), 0)),
            pl.BlockSpec((PEER_CH, D_MODEL), lambda i, c: (jnp.maximum(c - 1, 0), 0)),
        ],
        out_specs=pl.BlockSpec((tb, D_MODEL), lambda i, c: (i, 0)),
        out_shape=jax.ShapeDtypeStruct((n, D_MODEL), F32),
        scratch_shapes=[pltpu.VMEM((tb * M_PITCH, LANES), F32),
                        pltpu.VMEM((2, tb, PEER_CH), BF16),
                        pltpu.VMEM((tb, D_MODEL), F32)],
        compiler_params=_params("parallel", "arbitrary"),
        name="peer",
    )(x, h2, e, g, mod4, final_g, u_tab, v_tab)


def kernel(x_prompt, x_sample, state_ret, c, c_ctx, w_mod, b_mod, norm_g, w_in, w_out, gmlp_ws, gmlp_b,
           gmlp_ln_g, pool_w, pool_scale, ret_decay, peer_wq, peer_subkeys, peer_u, peer_v, final_norm_g):
    batch, seq, d = x_prompt.shape
    dec_batch, dec_seq, _ = x_sample.shape
    depth = w_mod.shape[0]
    assert d == D_MODEL and dec_batch + 1 <= MOD_ROWS and dec_seq % GRID_W == 0

    cond = jnp.zeros((MOD_ROWS, d), F32).at[0].set(c_ctx).at[1:1 + dec_batch].set(c)
    mod = _modulation(cond, w_mod, b_mod)
    mod4 = mod.reshape(depth, MOD_ROWS, 1, 6 * d)
    ctx_row = lambda tok0: 0
    lat_row = lambda tok0: 1 + tok0 // dec_seq

    xc = x_prompt.reshape(batch * seq, d)
    xs = x_sample.reshape(dec_batch * dec_seq, d)
    fg = final_norm_g.reshape(1, d)
    new_states = []
    for l in range(depth):
        last = l == depth - 1
        w_in_l = w_in[l].astype(BF16)
        w_out_l = w_out[l].astype(BF16)
        ws_l = gmlp_ws[l].astype(BF16)
        bs_t = gmlp_b[l].T
        ln_g = gmlp_ln_g[l].reshape(1, d)
        pw_l = pool_w[l].astype(BF16)
        ps_l = pool_scale[l].reshape(1, d)
        wq_l = peer_wq[l].astype(BF16)
        sk_l = peer_subkeys[l].reshape(2 * PEER_HEADS, PEER_NKEYS, PEER_DKEY // 2).astype(BF16)
        u_l = peer_u[l].astype(BF16)
        v_l = peer_v[l].astype(BF16)
        g1 = norm_g[l, 0].reshape(1, d)
        g2 = norm_g[l, 1].reshape(1, d)

        def path(x, nseq, n, row_fn, grid, s0):
            proj = _proj(x, mod4, row_fn, l, g1, w_in_l)
            ya = _gmlp(proj, ws_l, bs_t, ln_g)
            yb = _pool(proj, pw_l, ps_l, nseq, n, grid)
            ret = _retention(proj, ret_decay[l], nseq, n, s0, l)
            yc, s_fin = (ret, None) if s0 is not None else ret
            xn, h2 = _merge(x, proj, ya, yb, yc, mod4, row_fn, l, g2, w_out_l)
            e, g = _route(h2, wq_l, sk_l)
            return _peer(xn, h2, e, g, mod4, row_fn, l, fg, u_l, v_l, last), s_fin

        xc, s_fin = path(xc, batch, seq, ctx_row, False, None)
        new_states.append(s_fin)
        xs, _ = path(xs, dec_batch, dec_seq, lat_row, True, state_ret)

    new_state_ret = jnp.stack(new_states, axis=1).astype(x_prompt.dtype)
    return (xc.reshape(batch, seq, d), xs.reshape(dec_batch, dec_seq, d), new_state_ret)
```

```python
import functools

import numpy as np
import jax
import jax.numpy as jnp
from jax import lax
from jax.experimental import pallas as pl
from jax.experimental.pallas import tpu as pltpu

F32 = jnp.float32
BF16 = jnp.bfloat16
I32 = jnp.int32

D_MODEL = 1024
A_GROUPS = 4
A_CHUNK = 128
POOL_WINDOWS = (2, 4, 8, 16)
B_GW = D_MODEL // len(POOL_WINDOWS)
GRID_W = 64
RET_HEADS = 4
RET_DV = D_MODEL // RET_HEADS
RET_DK = RET_DV // 2
RET_CHUNK = 128
PEER_HEADS = 8
PEER_NKEYS = 128
PEER_DKEY = 256
PEER_TOPK = 16
EPS = 1e-6
IN_COLS = 10 * D_MODEL

SUBLANES = 8
LANES = 128
VMEM_LIMIT = 56 * 1024 * 1024

MOD_ROWS = 16
PROJ_TB = 512
PROJ_TN = 2048
GMLP_TB = 256
MERGE_TB = 512
ROUTE_TB = SUBLANES * LANES
PEER_TB = 512
PEER_CH = 1024
PEER_SUB = 256
PEER_PASSES = 2
M_ROWS = PEER_NKEYS // PEER_PASSES
M_PITCH = M_ROWS + SUBLANES
M_GROUP = 32


def _gelu(x):
    return 0.5 * x * (1.0 + jnp.tanh(0.7978845608028654 * (x + 0.044715 * (x * x * x))))


def _sigmoid(x):
    return 1.0 / (1.0 + jnp.exp(-x))


def _rms(x, g):
    return x * lax.rsqrt(jnp.mean(x * x, axis=-1, keepdims=True) + EPS) * g


def _params(*sem):
    return pltpu.CompilerParams(dimension_semantics=sem, vmem_limit_bytes=VMEM_LIMIT)


def _tree(fn, xs):
    xs = list(xs)
    while len(xs) > 1:
        xs = [fn(xs[i], xs[i + 1]) for i in range(0, len(xs) - 1, 2)] + ([xs[-1]] if len(xs) % 2 else [])
    return xs[0]


def _nt_dot(a, b):
    return lax.dot_general(a, b, (((1,), (1,)), ((), ())), preferred_element_type=F32)


def _mod_kernel(cond_ref, w_ref, b_ref, o_ref):
    c = cond_ref[...]
    s = (c * _sigmoid(c)).astype(BF16)
    o_ref[...] = jnp.dot(s, w_ref[...].astype(BF16), preferred_element_type=F32) + b_ref[...]


def _modulation(cond, w_mod, b_mod):
    depth = w_mod.shape[0]
    nj = w_mod.shape[2] // D_MODEL
    return pl.pallas_call(
        _mod_kernel,
        grid=(depth, nj),
        in_specs=[
            pl.BlockSpec((MOD_ROWS, D_MODEL), lambda l, j: (0, 0)),
            pl.BlockSpec((None, D_MODEL, D_MODEL), lambda l, j: (l, 0, j)),
            pl.BlockSpec((None, 1, D_MODEL), lambda l, j: (l, 0, j)),
        ],
        out_specs=pl.BlockSpec((None, MOD_ROWS, D_MODEL), lambda l, j: (l, 0, j)),
        out_shape=jax.ShapeDtypeStruct((depth, MOD_ROWS, nj * D_MODEL), F32),
        compiler_params=_params("parallel", "parallel"),
        name="mod",
    )(cond, w_mod, b_mod.reshape(depth, 1, nj * D_MODEL))


def _proj_kernel(x_ref, mod_ref, g_ref, w_ref, o_ref, h_ref):
    @pl.when(pl.program_id(1) == 0)
    def _():
        y = _rms(x_ref[...], g_ref[...])
        sh = mod_ref[:, 0:D_MODEL]
        sc = mod_ref[:, D_MODEL:2 * D_MODEL]
        h_ref[...] = (y * (1.0 + sc) + sh).astype(BF16)

    o_ref[...] = jnp.dot(h_ref[...], w_ref[...], preferred_element_type=F32)


def _proj(x, mod4, row_fn, layer, norm_g, w_in):
    n = x.shape[0]
    tb, tn = PROJ_TB, PROJ_TN
    return pl.pallas_call(
        _proj_kernel,
        grid=(n // tb, IN_COLS // tn),
        in_specs=[
            pl.BlockSpec((tb, D_MODEL), lambda i, j: (i, 0)),
            pl.BlockSpec((None, None, 1, 6 * D_MODEL), lambda i, j: (layer, row_fn(i * tb), 0, 0)),
            pl.BlockSpec((1, D_MODEL), lambda i, j: (0, 0)),
            pl.BlockSpec((D_MODEL, tn), lambda i, j: (0, j)),
        ],
        out_specs=pl.BlockSpec((tb, tn), lambda i, j: (i, j)),
        out_shape=jax.ShapeDtypeStruct((n, IN_COLS), F32),
        scratch_shapes=[pltpu.VMEM((tb, D_MODEL), BF16)],
        compiler_params=_params("parallel", "arbitrary"),
        name="proj",
    )(x, mod4, norm_g, w_in)


def _gmlp_kernel(u_ref, v_ref, ws_ref, bs_ref, lng_ref, o_ref):
    gw = D_MODEL // A_GROUPS
    for c in range(GMLP_TB // A_CHUNK):
        rows = slice(c * A_CHUNK, (c + 1) * A_CHUNK)
        v = _gelu(v_ref[rows, :])
        vc = v - jnp.mean(v, axis=-1, keepdims=True)
        vn = vc * lax.rsqrt(jnp.mean(vc * vc, axis=-1, keepdims=True) + EPS) * lng_ref[...]
        vnb = vn.astype(BF16)
        for g in range(A_GROUPS):
            cols = slice(g * gw, (g + 1) * gw)
            sv = jnp.dot(ws_ref[g], vnb[:, cols], preferred_element_type=F32) + bs_ref[:, g:g + 1]
            o_ref[rows, cols] = _gelu(u_ref[rows, cols]) * sv


def _gmlp(proj, ws, bs_t, ln_g):
    n = proj.shape[0]
    tb = GMLP_TB
    return pl.pallas_call(
        _gmlp_kernel,
        grid=(n // tb,),
        in_specs=[
            pl.BlockSpec((tb, D_MODEL), lambda i: (i, 0)),
            pl.BlockSpec((tb, D_MODEL), lambda i: (i, 1)),
            pl.BlockSpec((A_GROUPS, A_CHUNK, A_CHUNK), lambda i: (0, 0, 0)),
            pl.BlockSpec((A_CHUNK, A_GROUPS), lambda i: (0, 0)),
            pl.BlockSpec((1, D_MODEL), lambda i: (0, 0)),
        ],
        out_specs=pl.BlockSpec((tb, D_MODEL), lambda i: (i, 0)),
        out_shape=jax.ShapeDtypeStruct((n, D_MODEL), F32),
        compiler_params=_params("parallel"),
        name="gmlp",
    )(proj, proj, ws, bs_t, ln_g)


POOL_PAD = 16


def _window_count(pos, size, a, b):
    return jnp.minimum(pos + b, size - 1) - jnp.maximum(pos - a, 0) + 1


def _pool_kernel(z_ref, w_ref, scale_ref, o_ref, zp_ref, cp_ref, *, n, grid):
    rows = n // GRID_W
    rpad = cp_ref.shape[0] - n
    tok = lax.broadcasted_iota(I32, (n, 1), 0)
    for gi, w in enumerate(POOL_WINDOWS):
        a = w // 2
        b = w - 1 - a
        cols = slice(gi * B_GW, (gi + 1) * B_GW)
        z = z_ref[:, cols]
        zp_ref[0:POOL_PAD, :] = jnp.zeros((POOL_PAD, B_GW), F32)
        zp_ref[POOL_PAD + n:, :] = jnp.zeros((POOL_PAD, B_GW), F32)
        zp_ref[POOL_PAD:POOL_PAD + n, :] = z
        if grid:
            col = tok & (GRID_W - 1)
            row = tok >> (GRID_W.bit_length() - 1)
            s = jnp.zeros((n, B_GW), F32)
            for d in range(-a, b + 1):
                sh = zp_ref[POOL_PAD + d:POOL_PAD + d + n, :]
                ok = jnp.logical_and(col + d >= 0, col + d < GRID_W)
                s = s + jnp.where(ok, sh, 0.0)
            half = rpad // 2
            cp_ref[0:half, :] = jnp.zeros((half, B_GW), F32)
            cp_ref[half + n:, :] = jnp.zeros((half, B_GW), F32)
            cp_ref[half:half + n, :] = s
            s = jnp.zeros((n, B_GW), F32)
            for d in range(-a, b + 1):
                s = s + cp_ref[half + d * GRID_W:half + d * GRID_W + n, :]
            cnt = (_window_count(row, rows, a, b) * _window_count(col, GRID_W, a, b)).astype(F32)
        else:
            s = jnp.zeros((n, B_GW), F32)
            for d in range(-a, b + 1):
                s = s + zp_ref[POOL_PAD + d:POOL_PAD + d + n, :]
            cnt = _window_count(tok, n, a, b).astype(F32)
        diff = (s / cnt - z).astype(BF16)
        o_ref[:, cols] = jnp.dot(diff, w_ref[gi], preferred_element_type=F32) * scale_ref[:, cols]


def _pool(proj, pool_w, pool_scale, nseq, n, grid):
    rpad = 2 * (max(POOL_WINDOWS) // 2) * GRID_W if grid else 2 * SUBLANES
    return pl.pallas_call(
        functools.partial(_pool_kernel, n=n, grid=grid),
        grid=(nseq,),
        in_specs=[
            pl.BlockSpec((n, D_MODEL), lambda i: (i, 2)),
            pl.BlockSpec((len(POOL_WINDOWS), B_GW, B_GW), lambda i: (0, 0, 0)),
            pl.BlockSpec((1, D_MODEL), lambda i: (0, 0)),
        ],
        out_specs=pl.BlockSpec((n, D_MODEL), lambda i: (i, 0)),
        out_shape=jax.ShapeDtypeStruct((nseq * n, D_MODEL), F32),
        scratch_shapes=[pltpu.VMEM((n + 2 * POOL_PAD, B_GW), F32), pltpu.VMEM((n + rpad, B_GW), F32)],
        compiler_params=_params("parallel"),
        name="pool",
    )(proj, pool_w, pool_scale)


def _log_sigmoid(x):
    return jnp.minimum(x, 0.0) - jnp.log(1.0 + jnp.exp(-jnp.abs(x)))


def _ret_kernel(*refs, n, has_s0):
    if has_s0:
        q_ref, k_ref, v_ref, gf_ref, gb_ref, rd_ref, s0_ref, y_ref, s_ref, dm_ref = refs
        sfin_ref = None
    else:
        q_ref, k_ref, v_ref, gf_ref, gb_ref, rd_ref, y_ref, sfin_ref, s_ref, dm_ref = refs
        s0_ref = None
    cl = RET_CHUNK
    nc = n // cl
    pi = lax.broadcasted_iota(I32, (cl, cl), 0).astype(F32)
    pj = lax.broadcasted_iota(I32, (cl, cl), 1).astype(F32)
    pcol = lax.broadcasted_iota(I32, (cl, 1), 0).astype(F32)
    kscale = RET_DK ** -0.5

    for d in range(2):
        g_ref = gf_ref if d == 0 else gb_ref
        lgs = []
        for h in range(RET_HEADS):
            lg = _log_sigmoid(rd_ref[d:d + 1, h:h + 1])
            lgs.append(lg)
            rel = (pi - pj) if d == 0 else (pj - pi)
            dm_ref[h] = jnp.where(rel >= 0.0, jnp.exp(lg * jnp.maximum(rel, 0.0)), 0.0)
            if has_s0:
                s_ref[h] = s0_ref[d, h]
            else:
                s_ref[h] = jnp.zeros((RET_DK, RET_DV), F32)

        def chunk(ci, carry, d=d, g_ref=g_ref, lgs=lgs):
            c = ci if d == 0 else nc - 1 - ci
            rows = pl.ds(pl.multiple_of(c * cl, cl), cl)
            for h in range(RET_HEADS):
                lg = lgs[h]
                if d == 0:
                    qdec = jnp.exp(lg * (pcol + 1.0))
                    kdec = jnp.exp(lg * (cl - 1.0 - pcol))
                else:
                    qdec = jnp.exp(lg * (cl - pcol))
                    kdec = jnp.exp(lg * pcol)
                cdec = jnp.exp(lg * float(cl))
                q = q_ref[rows, h * RET_DK:(h + 1) * RET_DK]
                k = k_ref[rows, h * RET_DK:(h + 1) * RET_DK] * kscale
                vb = v_ref[rows, h * RET_DV:(h + 1) * RET_DV].astype(BF16)
                sc = _nt_dot(q.astype(BF16), k.astype(BF16)) * dm_ref[h]
                o = jnp.dot(sc.astype(BF16), vb, preferred_element_type=F32)
                s_prev = s_ref[h]
                o = o + jnp.dot((q * qdec).astype(BF16), s_prev.astype(BF16), preferred_element_type=F32)
                kd_t = jnp.transpose(k * kdec).astype(BF16)
                s_ref[h] = cdec * s_prev + jnp.dot(kd_t, vb, preferred_element_type=F32)
                on = o * lax.rsqrt(jnp.mean(o * o, axis=-1, keepdims=True) + EPS)
                g = g_ref[rows, h * RET_DV:(h + 1) * RET_DV]
                yv = g * _sigmoid(g) * on
                if d == 0:
                    y_ref[rows, h * RET_DV:(h + 1) * RET_DV] = yv
                else:
                    y_ref[rows, h * RET_DV:(h + 1) * RET_DV] += yv
            return carry

        lax.fori_loop(0, nc, chunk, 0)
        if not has_s0:
            for h in range(RET_HEADS):
                sfin_ref[d, h] = s_ref[h]


def _retention(proj, ret_decay, nseq, n, s0, layer):
    has_s0 = s0 is not None
    dkb = RET_HEADS * RET_DK
    in_specs = [
        pl.BlockSpec((n, dkb), lambda i: (i, 3 * D_MODEL // dkb)),
        pl.BlockSpec((n, dkb), lambda i: (i, 3 * D_MODEL // dkb + 1)),
        pl.BlockSpec((n, D_MODEL), lambda i: (i, 4)),
        pl.BlockSpec((n, D_MODEL), lambda i: (i, 5)),
        pl.BlockSpec((n, D_MODEL), lambda i: (i, 6)),
        pl.BlockSpec((2, RET_HEADS), lambda i: (0, 0)),
    ]
    args = [proj, proj, proj, proj, proj, ret_decay]
    y_shape = jax.ShapeDtypeStruct((nseq * n, D_MODEL), F32)
    y_spec = pl.BlockSpec((n, D_MODEL), lambda i: (i, 0))
    if has_s0:
        in_specs.append(pl.BlockSpec((None, None, 2, RET_HEADS, RET_DK, RET_DV), lambda i: (i, layer, 0, 0, 0, 0)))
        args.append(s0)
        out_shape, out_specs = y_shape, y_spec
    else:
        out_shape = (y_shape, jax.ShapeDtypeStruct((nseq, 2, RET_HEADS, RET_DK, RET_DV), F32))
        out_specs = (y_spec, pl.BlockSpec((None, 2, RET_HEADS, RET_DK, RET_DV), lambda i: (i, 0, 0, 0, 0)))
    return pl.pallas_call(
        functools.partial(_ret_kernel, n=n, has_s0=has_s0),
        grid=(nseq,),
        in_specs=in_specs,
        out_specs=out_specs,
        out_shape=out_shape,
        scratch_shapes=[pltpu.VMEM((RET_HEADS, RET_DK, RET_DV), F32),
                        pltpu.VMEM((RET_HEADS, RET_CHUNK, RET_CHUNK), F32)],
        compiler_params=_params("parallel"),
        name="ret",
    )(*args)


def _merge_kernel(x_ref, ga_ref, gb_ref, gc_ref, ya_ref, yb_ref, yc_ref, mod_ref, g2_ref, w_ref, xo_ref, h2_ref):
    merged = (_sigmoid(ga_ref[...]) * ya_ref[...] + _sigmoid(gb_ref[...]) * yb_ref[...]
              + _sigmoid(gc_ref[...]) * yc_ref[...])
    o = jnp.dot(merged.astype(BF16), w_ref[...], preferred_element_type=F32)
    gt1 = mod_ref[:, 2 * D_MODEL:3 * D_MODEL]
    sh2 = mod_ref[:, 3 * D_MODEL:4 * D_MODEL]
    sc2 = mod_ref[:, 4 * D_MODEL:5 * D_MODEL]
    xn = x_ref[...] + gt1 * o
    xo_ref[...] = xn
    h2_ref[...] = (_rms(xn, g2_ref[...]) * (1.0 + sc2) + sh2).astype(BF16)


def _merge(x, proj, ya, yb, yc, mod4, row_fn, layer, norm_g2, w_out):
    n = x.shape[0]
    tb = MERGE_TB
    tok = lambda c: pl.BlockSpec((tb, D_MODEL), lambda i: (i, c))
    return pl.pallas_call(
        _merge_kernel,
        grid=(n // tb,),
        in_specs=[
            tok(0), tok(7), tok(8), tok(9), tok(0), tok(0), tok(0),
            pl.BlockSpec((None, None, 1, 6 * D_MODEL), lambda i: (layer, row_fn(i * tb), 0, 0)),
            pl.BlockSpec((1, D_MODEL), lambda i: (0, 0)),
            pl.BlockSpec((D_MODEL, D_MODEL), lambda i: (0, 0)),
        ],
        out_specs=(tok(0), tok(0)),
        out_shape=(jax.ShapeDtypeStruct((n, D_MODEL), F32), jax.ShapeDtypeStruct((n, D_MODEL), BF16)),
        compiler_params=_params("parallel"),
        name="merge",
    )(x, proj, proj, proj, ya, yb, yc, mod4, norm_g2, w_out)


_CAND_PAIRS = tuple((j1, j2) for j1 in range(PEER_TOPK) for j2 in range(PEER_TOPK)
                    if (j1 + 1) * (j2 + 1) <= PEER_TOPK)
ROUTE_ACCS = 4


def _scan_max(n_items, load, knock, store):
    per = -(-n_items // ROUTE_ACCS)
    accs = []
    for a0 in range(0, n_items, per):
        bv = bt = None
        for i in range(a0, min(a0 + per, n_items)):
            v, t = load(i)
            v = knock(i, v, t)
            store(i, v)
            if bv is None:
                bv, bt = v, t
            else:
                gt = v > bv
                bv = jnp.where(gt, v, bv)
                bt = jnp.where(gt, t, bt)
        accs.append((bv, bt))
    bv, bt = accs[0]
    for v, t in accs[1:]:
        gt = v > bv
        bv = jnp.where(gt, v, bv)
        bt = jnp.where(gt, t, bt)
    return bv, bt


def _route_kernel(h_ref, wq_ref, sk_ref, e_ref, g_ref, q_scr, x_scr, ts_scr, ti_scr, c_scr, k_scr, eo_scr, go_scr):
    tb = h_ref.shape[0]
    nk = PEER_NKEYS
    kd = PEER_DKEY // 2
    nlt = tb // LANES
    assert nlt == SUBLANES
    neg = -jnp.inf
    tile = lambda i: pl.ds(i * SUBLANES, SUBLANES)
    for hd in range(PEER_HEADS):
        cols = slice(hd * PEER_DKEY, (hd + 1) * PEER_DKEY)
        q = jnp.dot(h_ref[...], wq_ref[:, cols], preferred_element_type=F32).astype(BF16)
        q_scr[2 * hd] = q[:, :kd]
        q_scr[2 * hd + 1] = q[:, kd:]

    def side(hs, carry):
        for j in range(nlt):
            sc = _nt_dot(sk_ref[hs], q_scr[hs, j * LANES:(j + 1) * LANES, :])
            x_scr[pl.ds(j, nk, stride=nlt), :] = sc
        prev = None
        for r in range(PEER_TOPK):
            load = lambda n: (x_scr[tile(n), :], jnp.full((SUBLANES, LANES), n, I32))
            if prev is None:
                knock = lambda n, v, t: v
                store = lambda n, v: None
            else:
                knock = lambda n, v, t, prev=prev: jnp.where(prev == n, neg, v)
                store = lambda n, v: x_scr.__setitem__((tile(n), slice(None)), v)
            bv, bi = _scan_max(nk, load, knock, store)
            ts_scr[hs, r] = bv
            ti_scr[hs, r] = bi
            prev = bi
        return carry

    lax.fori_loop(0, 2 * PEER_HEADS, side, 0)

    def head(h, carry):
        for p, (j1, j2) in enumerate(_CAND_PAIRS):
            c_scr[tile(p), :] = ts_scr[2 * h, j1] + ts_scr[2 * h + 1, j2]
            k_scr[tile(p), :] = ((j1 * PEER_TOPK + j2) * (nk * nk) + ti_scr[2 * h, j1] * nk
                                 + ti_scr[2 * h + 1, j2])
        prev = None
        vals, exps = [], []
        for r in range(PEER_TOPK):
            load = lambda p: (c_scr[tile(p), :], k_scr[tile(p), :])
            if prev is None:
                knock = lambda p, v, t: v
                store = lambda p, v: None
            else:
                knock = lambda p, v, t, prev=prev: jnp.where(t == prev, neg, v)
                store = lambda p, v: c_scr.__setitem__((tile(p), slice(None)), v)
            bv, bk = _scan_max(len(_CAND_PAIRS), load, knock, store)
            vals.append(bv)
            exps.append(bk & (nk * nk - 1))
            prev = bk
        ps = [jnp.exp(v - vals[0]) for v in vals]
        z = _tree(jnp.add, ps)
        for r in range(PEER_TOPK):
            row = pl.ds(pl.multiple_of((h * PEER_TOPK + r) * SUBLANES, SUBLANES), SUBLANES)
            go_scr[row, :] = ps[r] / z
            eo_scr[row, :] = exps[r]
        return carry

    lax.fori_loop(0, PEER_HEADS, head, 0)
    nsel = PEER_HEADS * PEER_TOPK
    for j in range(nlt):
        rows = slice(j * LANES, (j + 1) * LANES)
        e_ref[rows, :] = jnp.transpose(eo_scr[pl.ds(j, nsel, stride=nlt), :])
        g_ref[rows, :] = jnp.transpose(go_scr[pl.ds(j, nsel, stride=nlt), :])


def _route(h2, wq, subkeys):
    n = h2.shape[0]
    tb = ROUTE_TB
    nsel = PEER_HEADS * PEER_TOPK
    kd = PEER_DKEY // 2
    tile_rows = lambda count: count * SUBLANES
    return pl.pallas_call(
        _route_kernel,
        grid=(n // tb,),
        in_specs=[
            pl.BlockSpec((tb, D_MODEL), lambda i: (i, 0)),
            pl.BlockSpec((D_MODEL, PEER_HEADS * PEER_DKEY), lambda i: (0, 0)),
            pl.BlockSpec((2 * PEER_HEADS, PEER_NKEYS, kd), lambda i: (0, 0, 0)),
        ],
        out_specs=(pl.BlockSpec((tb, nsel), lambda i: (i, 0)), pl.BlockSpec((tb, nsel), lambda i: (i, 0))),
        out_shape=(jax.ShapeDtypeStruct((n, nsel), I32), jax.ShapeDtypeStruct((n, nsel), F32)),
        scratch_shapes=[
            pltpu.VMEM((2 * PEER_HEADS, tb, kd), BF16),
            pltpu.VMEM((tile_rows(PEER_NKEYS), LANES), F32),
            pltpu.VMEM((2 * PEER_HEADS, PEER_TOPK, SUBLANES, LANES), F32),
            pltpu.VMEM((2 * PEER_HEADS, PEER_TOPK, SUBLANES, LANES), I32),
            pltpu.VMEM((tile_rows(len(_CAND_PAIRS)), LANES), F32),
            pltpu.VMEM((tile_rows(len(_CAND_PAIRS)), LANES), I32),
            pltpu.VMEM((tile_rows(nsel), LANES), I32),
            pltpu.VMEM((tile_rows(nsel), LANES), F32),
        ],
        compiler_params=_params("parallel"),
        name="route",
    )(h2, wq, subkeys)


def _peer_kernel(x_ref, h_ref, e_ref, g_ref, mod_ref, fg_ref, u_ref, v_ref, o_ref, m_ref, p_ref, acc_ref, *, final):
    tb = x_ref.shape[0]
    nk = PEER_NKEYS
    c = pl.program_id(1)
    n_chunks = pl.num_programs(1) - 1
    chunks_per_pass = n_chunks // PEER_PASSES

    @pl.when(jnp.logical_and(c % chunks_per_pass == 0, c < n_chunks))
    def _():
        a_iota = lax.broadcasted_iota(I32, (M_ROWS, nk), 0) + (c // chunks_per_pass) * M_ROWS
        b_iota = lax.broadcasted_iota(I32, (nk, nk), 0)
        zero = jnp.zeros((M_ROWS, nk), BF16)

        def group(gi, carry):
            t0 = pl.multiple_of(gi * M_GROUP, M_GROUP)
            er = e_ref[pl.ds(t0, M_GROUP), :]
            gr = g_ref[pl.ds(t0, M_GROUP), :]
            for pair in range(M_GROUP // 2):
                at, bt = [], []
                for tt in (2 * pair, 2 * pair + 1):
                    e1 = er[tt:tt + 1, :]
                    at.append(jnp.where(a_iota == (e1 >> 7), gr[tt:tt + 1, :], 0.0).astype(BF16))
                    bt.append(jnp.where(b_iota == (e1 & (nk - 1)), 1.0, 0.0).astype(BF16))
                lhs = jnp.concatenate([jnp.concatenate([at[0], zero], axis=1),
                                       jnp.concatenate([zero, at[1]], axis=1)], axis=0)
                out = _nt_dot(lhs, jnp.concatenate(bt, axis=1))
                r0 = pl.multiple_of((t0 + 2 * pair) * M_PITCH, SUBLANES)
                m_ref[pl.ds(r0, M_ROWS), :] = out[:M_ROWS, :]
                m_ref[pl.ds(r0 + M_PITCH, M_ROWS), :] = out[M_ROWS:, :]
            return carry

        lax.fori_loop(0, tb // M_GROUP, group, 0)

    slot = c % 2

    def mix_previous():
        acc_ref[...] += jnp.dot(p_ref[1 - slot], v_ref[...], preferred_element_type=F32)

    def stage_current():
        h = h_ref[...]
        per = PEER_SUB // nk
        a_base = (c % chunks_per_pass) * (PEER_CH // nk)
        for sub in range(PEER_CH // PEER_SUB):
            rows = slice(sub * PEER_SUB, (sub + 1) * PEER_SUB)
            s = _gelu(_nt_dot(h, u_ref[rows, :]))
            m = jnp.concatenate(
                [m_ref[pl.ds(a_base + sub * per + a, tb, stride=M_PITCH), :] for a in range(per)], axis=1)
            p_ref[slot, :, rows] = (m * s).astype(BF16)

    @pl.when(c == 0)
    def _():
        acc_ref[...] = jnp.zeros_like(acc_ref)
        stage_current()

    @pl.when(jnp.logical_and(c > 0, c < n_chunks))
    def _():
        mix_previous()
        stage_current()

    @pl.when(c == n_chunks)
    def _():
        mix_previous()
        gt2 = mod_ref[:, 5 * D_MODEL:6 * D_MODEL]
        xn = x_ref[...] + gt2 * acc_ref[...]
        o_ref[...] = _rms(xn, fg_ref[...]) if final else xn


def _peer(x, h2, e, g, mod4, row_fn, layer, final_g, u_tab, v_tab, final):
    n = x.shape[0]
    tb = PEER_TB
    nsel = PEER_HEADS * PEER_TOPK
    n_chunks = u_tab.shape[0] // PEER_CH
    return pl.pallas_call(
        functools.partial(_peer_kernel, final=final),
        grid=(n // tb, n_chunks + 1),
        in_specs=[
            pl.BlockSpec((tb, D_MODEL), lambda i, c: (i, 0)),
            pl.BlockSpec((tb, D_MODEL), lambda i, c: (i, 0)),
            pl.BlockSpec((tb, nsel), lambda i, c: (i, 0)),
            pl.BlockSpec((tb, nsel), lambda i, c: (i, 0)),
            pl.BlockSpec((None, None, 1, 6 * D_MODEL), lambda i, c: (layer, row_fn(i * tb), 0, 0)),
            pl.BlockSpec((1, D_MODEL), lambda i, c: (0, 0)),
            pl.BlockSpec((PEER_CH, D_MODEL), lambda i, c: (jnp.minimum(c, n_chunks - 1), 0)),
            pl.BlockSpec((PEER_CH, D_MODEL), lambda i, c: (jnp.maximum(c - 1, 0), 0)),
        ],
        out_specs=pl.BlockSpec((tb, D_MODEL), lambda i, c: (i, 0)),
        out_shape=jax.ShapeDtypeStruct((n, D_MODEL), F32),
        scratch_shapes=[pltpu.VMEM((tb * M_PITCH, LANES), F32),
                        pltpu.VMEM((2, tb, PEER_CH), BF16),
                        pltpu.VMEM((tb, D_MODEL), F32)],
        compiler_params=_params("parallel", "arbitrary"),
        name="peer",
    )(x, h2, e, g, mod4, final_g, u_tab, v_tab)


def kernel(x_prompt, x_sample, state_ret, c, c_ctx, w_mod, b_mod, norm_g, w_in, w_out, gmlp_ws, gmlp_b,
           gmlp_ln_g, pool_w, pool_scale, ret_decay, peer_wq, peer_subkeys, peer_u, peer_v, final_norm_g):
    batch, seq, d = x_prompt.shape
    dec_batch, dec_seq, _ = x_sample.shape
    depth = w_mod.shape[0]
    assert d == D_MODEL and dec_batch + 1 <= MOD_ROWS and dec_seq % GRID_W == 0

    cond = jnp.zeros((MOD_ROWS, d), F32).at[0].set(c_ctx).at[1:1 + dec_batch].set(c)
    mod = _modulation(cond, w_mod, b_mod)
    mod4 = mod.reshape(depth, MOD_ROWS, 1, 6 * d)
    ctx_row = lambda tok0: 0
    lat_row = lambda tok0: 1 + tok0 // dec_seq

    xc = x_prompt.reshape(batch * seq, d)
    xs = x_sample.reshape(dec_batch * dec_seq, d)
    fg = final_norm_g.reshape(1, d)
    new_states = []
    for l in range(depth):
        last = l == depth - 1
        w_in_l = w_in[l].astype(BF16)
        w_out_l = w_out[l].astype(BF16)
        ws_l = gmlp_ws[l].astype(BF16)
        bs_t = gmlp_b[l].T
        ln_g = gmlp_ln_g[l].reshape(1, d)
        pw_l = pool_w[l].astype(BF16)
        ps_l = pool_scale[l].reshape(1, d)
        wq_l = peer_wq[l].astype(BF16)
        sk_l = peer_subkeys[l].reshape(2 * PEER_HEADS, PEER_NKEYS, PEER_DKEY // 2).astype(BF16)
        u_l = peer_u[l].astype(BF16)
        v_l = peer_v[l].astype(BF16)
        g1 = norm_g[l, 0].reshape(1, d)
        g2 = norm_g[l, 1].reshape(1, d)

        def path(x, nseq, n, row_fn, grid, s0):
            proj = _proj(x, mod4, row_fn, l, g1, w_in_l)
            ya = _gmlp(proj, ws_l, bs_t, ln_g)
            yb = _pool(proj, pw_l, ps_l, nseq, n, grid)
            ret = _retention(proj, ret_decay[l], nseq, n, s0, l)
            yc, s_fin = (ret, None) if s0 is not None else ret
            xn, h2 = _merge(x, proj, ya, yb, yc, mod4, row_fn, l, g2, w_out_l)
            e, g = _route(h2, wq_l, sk_l)
            return _peer(xn, h2, e, g, mod4, row_fn, l, fg, u_l, v_l, last), s_fin

        xc, s_fin = path(xc, batch, seq, ctx_row, False, None)
        new_states.append(s_fin)
        xs, _ = path(xs, dec_batch, dec_seq, lat_row, True, state_ret)

    new_state_ret = jnp.stack(new_states, axis=1).astype(x_prompt.dtype)
    return (xc.reshape(batch, seq, d), xs.reshape(dec_batch, dec_seq, d), new_state_ret)
```

```python
import functools

import numpy as np
import jax
import jax.numpy as jnp
from jax import lax
from jax.experimental import pallas as pl
from jax.experimental.pallas import tpu as pltpu

F32 = jnp.float32
BF16 = jnp.bfloat16
I32 = jnp.int32

D_MODEL = 1024
A_GROUPS = 4
A_CHUNK = 128
POOL_WINDOWS = (2, 4, 8, 16)
B_GW = D_MODEL // len(POOL_WINDOWS)
GRID_W = 64
RET_HEADS = 4
RET_DV = D_MODEL // RET_HEADS
RET_DK = RET_DV // 2
RET_CHUNK = 128
PEER_HEADS = 8
PEER_NKEYS = 128
PEER_DKEY = 256
PEER_TOPK = 16
EPS = 1e-6
IN_COLS = 10 * D_MODEL

SUBLANES = 8
LANES = 128
VMEM_LIMIT = 56 * 1024 * 1024

MOD_ROWS = 16
PROJ_TB = 512
PROJ_TN = 2048
PROJ_DTYPE = BF16
GMLP_TB = 256
MERGE_TB = 512
ROUTE_TB = SUBLANES * LANES
PEER_TB = 512
PEER_CH = 1024
PEER_SUB = 256
PEER_PASSES = 2
M_ROWS = PEER_NKEYS // PEER_PASSES
M_PITCH = M_ROWS + SUBLANES
M_GROUP = 32


def _gelu(x):
    return 0.5 * x * (1.0 + jnp.tanh(0.7978845608028654 * (x + 0.044715 * (x * x * x))))


def _sigmoid(x):
    return 1.0 / (1.0 + jnp.exp(-x))


def _rms(x, g):
    return x * lax.rsqrt(jnp.mean(x * x, axis=-1, keepdims=True) + EPS) * g


def _params(*sem):
    return pltpu.CompilerParams(dimension_semantics=sem, vmem_limit_bytes=VMEM_LIMIT)


def _tree(fn, xs):
    xs = list(xs)
    while len(xs) > 1:
        xs = [fn(xs[i], xs[i + 1]) for i in range(0, len(xs) - 1, 2)] + ([xs[-1]] if len(xs) % 2 else [])
    return xs[0]


def _nt_dot(a, b):
    return lax.dot_general(a, b, (((1,), (1,)), ((), ())), preferred_element_type=F32)


def _mod_kernel(cond_ref, w_ref, b_ref, o_ref):
    c = cond_ref[...]
    s = (c * _sigmoid(c)).astype(BF16)
    o_ref[...] = jnp.dot(s, w_ref[...].astype(BF16), preferred_element_type=F32) + b_ref[...]


def _modulation(cond, w_mod, b_mod):
    depth = w_mod.shape[0]
    nj = w_mod.shape[2] // D_MODEL
    return pl.pallas_call(
        _mod_kernel,
        grid=(depth, nj),
        in_specs=[
            pl.BlockSpec((MOD_ROWS, D_MODEL), lambda l, j: (0, 0)),
            pl.BlockSpec((None, D_MODEL, D_MODEL), lambda l, j: (l, 0, j)),
            pl.BlockSpec((None, 1, D_MODEL), lambda l, j: (l, 0, j)),
        ],
        out_specs=pl.BlockSpec((None, MOD_ROWS, D_MODEL), lambda l, j: (l, 0, j)),
        out_shape=jax.ShapeDtypeStruct((depth, MOD_ROWS, nj * D_MODEL), F32),
        compiler_params=_params("parallel", "parallel"),
        name="mod",
    )(cond, w_mod, b_mod.reshape(depth, 1, nj * D_MODEL))


def _proj_kernel(x_ref, mod_ref, g_ref, w_ref, o_ref):
    y = _rms(x_ref[...], g_ref[...])
    sh = mod_ref[:, 0:D_MODEL]
    sc = mod_ref[:, D_MODEL:2 * D_MODEL]
    h = (y * (1.0 + sc) + sh).astype(BF16)
    o_ref[...] = jnp.dot(h, w_ref[...], preferred_element_type=F32).astype(o_ref.dtype)


def _proj(x, mod4, row_fn, layer, norm_g, w_in):
    n = x.shape[0]
    tb, tn = PROJ_TB, PROJ_TN
    return pl.pallas_call(
        _proj_kernel,
        grid=(IN_COLS // tn, n // tb),
        in_specs=[
            pl.BlockSpec((tb, D_MODEL), lambda j, i: (i, 0)),
            pl.BlockSpec((None, None, 1, 6 * D_MODEL), lambda j, i: (layer, row_fn(i * tb), 0, 0)),
            pl.BlockSpec((1, D_MODEL), lambda j, i: (0, 0)),
            pl.BlockSpec((D_MODEL, tn), lambda j, i: (0, j)),
        ],
        out_specs=pl.BlockSpec((tb, tn), lambda j, i: (i, j)),
        out_shape=jax.ShapeDtypeStruct((n, IN_COLS), PROJ_DTYPE),
        compiler_params=_params("parallel", "parallel"),
        name="proj",
    )(x, mod4, norm_g, w_in)


def _gmlp_kernel(u_ref, v_ref, ws_ref, bs_ref, lng_ref, o_ref):
    gw = D_MODEL // A_GROUPS
    for c in range(GMLP_TB // A_CHUNK):
        rows = slice(c * A_CHUNK, (c + 1) * A_CHUNK)
        v = _gelu(v_ref[rows, :].astype(F32))
        vc = v - jnp.mean(v, axis=-1, keepdims=True)
        vn = vc * lax.rsqrt(jnp.mean(vc * vc, axis=-1, keepdims=True) + EPS) * lng_ref[...]
        vnb = vn.astype(BF16)
        for g in range(A_GROUPS):
            cols = slice(g * gw, (g + 1) * gw)
            sv = jnp.dot(ws_ref[g], vnb[:, cols], preferred_element_type=F32) + bs_ref[:, g:g + 1]
            o_ref[rows, cols] = _gelu(u_ref[rows, cols].astype(F32)) * sv


def _gmlp(proj, ws, bs_t, ln_g):
    n = proj.shape[0]
    tb = GMLP_TB
    return pl.pallas_call(
        _gmlp_kernel,
        grid=(n // tb,),
        in_specs=[
            pl.BlockSpec((tb, D_MODEL), lambda i: (i, 0)),
            pl.BlockSpec((tb, D_MODEL), lambda i: (i, 1)),
            pl.BlockSpec((A_GROUPS, A_CHUNK, A_CHUNK), lambda i: (0, 0, 0)),
            pl.BlockSpec((A_CHUNK, A_GROUPS), lambda i: (0, 0)),
            pl.BlockSpec((1, D_MODEL), lambda i: (0, 0)),
        ],
        out_specs=pl.BlockSpec((tb, D_MODEL), lambda i: (i, 0)),
        out_shape=jax.ShapeDtypeStruct((n, D_MODEL), F32),
        compiler_params=_params("parallel"),
        name="gmlp",
    )(proj, proj, ws, bs_t, ln_g)


POOL_PAD = 16


def _window_count(pos, size, a, b):
    return jnp.minimum(pos + b, size - 1) - jnp.maximum(pos - a, 0) + 1


def _pool_kernel(z_ref, w_ref, scale_ref, o_ref, zp_ref, cp_ref, *, n, grid):
    rows = n // GRID_W
    rpad = cp_ref.shape[0] - n
    tok = lax.broadcasted_iota(I32, (n, 1), 0)
    for gi, w in enumerate(POOL_WINDOWS):
        a = w // 2
        b = w - 1 - a
        cols = slice(gi * B_GW, (gi + 1) * B_GW)
        z = z_ref[:, cols].astype(F32)
        zp_ref[0:POOL_PAD, :] = jnp.zeros((POOL_PAD, B_GW), F32)
        zp_ref[POOL_PAD + n:, :] = jnp.zeros((POOL_PAD, B_GW), F32)
        zp_ref[POOL_PAD:POOL_PAD + n, :] = z
        if grid:
            col = tok & (GRID_W - 1)
            row = tok >> (GRID_W.bit_length() - 1)
            s = jnp.zeros((n, B_GW), F32)
            for d in range(-a, b + 1):
                sh = zp_ref[POOL_PAD + d:POOL_PAD + d + n, :]
                ok = jnp.logical_and(col + d >= 0, col + d < GRID_W)
                s = s + jnp.where(ok, sh, 0.0)
            half = rpad // 2
            cp_ref[0:half, :] = jnp.zeros((half, B_GW), F32)
            cp_ref[half + n:, :] = jnp.zeros((half, B_GW), F32)
            cp_ref[half:half + n, :] = s
            s = jnp.zeros((n, B_GW), F32)
            for d in range(-a, b + 1):
                s = s + cp_ref[half + d * GRID_W:half + d * GRID_W + n, :]
            cnt = (_window_count(row, rows, a, b) * _window_count(col, GRID_W, a, b)).astype(F32)
        else:
            s = jnp.zeros((n, B_GW), F32)
            for d in range(-a, b + 1):
                s = s + zp_ref[POOL_PAD + d:POOL_PAD + d + n, :]
            cnt = _window_count(tok, n, a, b).astype(F32)
        diff = (s / cnt - z).astype(BF16)
        o_ref[:, cols] = jnp.dot(diff, w_ref[gi], preferred_element_type=F32) * scale_ref[:, cols]


def _pool(proj, pool_w, pool_scale, nseq, n, grid):
    rpad = 2 * (max(POOL_WINDOWS) // 2) * GRID_W if grid else 2 * SUBLANES
    return pl.pallas_call(
        functools.partial(_pool_kernel, n=n, grid=grid),
        grid=(nseq,),
        in_specs=[
            pl.BlockSpec((n, D_MODEL), lambda i: (i, 2)),
            pl.BlockSpec((len(POOL_WINDOWS), B_GW, B_GW), lambda i: (0, 0, 0)),
            pl.BlockSpec((1, D_MODEL), lambda i: (0, 0)),
        ],
        out_specs=pl.BlockSpec((n, D_MODEL), lambda i: (i, 0)),
        out_shape=jax.ShapeDtypeStruct((nseq * n, D_MODEL), F32),
        scratch_shapes=[pltpu.VMEM((n + 2 * POOL_PAD, B_GW), F32), pltpu.VMEM((n + rpad, B_GW), F32)],
        compiler_params=_params("parallel"),
        name="pool",
    )(proj, pool_w, pool_scale)


def _log_sigmoid(x):
    return jnp.minimum(x, 0.0) - jnp.log(1.0 + jnp.exp(-jnp.abs(x)))


def _ret_kernel(*refs, n, has_s0):
    if has_s0:
        q_ref, k_ref, v_ref, gf_ref, gb_ref, rd_ref, s0_ref, y_ref, s_ref, dm_ref = refs
        sfin_ref = None
    else:
        q_ref, k_ref, v_ref, gf_ref, gb_ref, rd_ref, y_ref, sfin_ref, s_ref, dm_ref = refs
        s0_ref = None
    cl = RET_CHUNK
    nc = n // cl
    pi = lax.broadcasted_iota(I32, (cl, cl), 0).astype(F32)
    pj = lax.broadcasted_iota(I32, (cl, cl), 1).astype(F32)
    pcol = lax.broadcasted_iota(I32, (cl, 1), 0).astype(F32)
    kscale = RET_DK ** -0.5

    for d in range(2):
        g_ref = gf_ref if d == 0 else gb_ref
        lgs = []
        for h in range(RET_HEADS):
            lg = _log_sigmoid(rd_ref[d:d + 1, h:h + 1])
            lgs.append(lg)
            rel = (pi - pj) if d == 0 else (pj - pi)
            dm_ref[h] = jnp.where(rel >= 0.0, jnp.exp(lg * jnp.maximum(rel, 0.0)), 0.0)
            if has_s0:
                s_ref[h] = s0_ref[d, h]
            else:
                s_ref[h] = jnp.zeros((RET_DK, RET_DV), F32)

        def chunk(ci, carry, d=d, g_ref=g_ref, lgs=lgs):
            c = ci if d == 0 else nc - 1 - ci
            rows = pl.ds(pl.multiple_of(c * cl, cl), cl)
            for h in range(RET_HEADS):
                lg = lgs[h]
                if d == 0:
                    qdec = jnp.exp(lg * (pcol + 1.0))
                    kdec = jnp.exp(lg * (cl - 1.0 - pcol))
                else:
                    qdec = jnp.exp(lg * (cl - pcol))
                    kdec = jnp.exp(lg * pcol)
                cdec = jnp.exp(lg * float(cl))
                q = q_ref[rows, h * RET_DK:(h + 1) * RET_DK].astype(F32)
                k = k_ref[rows, h * RET_DK:(h + 1) * RET_DK].astype(F32) * kscale
                vb = v_ref[rows, h * RET_DV:(h + 1) * RET_DV].astype(BF16)
                sc = _nt_dot(q.astype(BF16), k.astype(BF16)) * dm_ref[h]
                o = jnp.dot(sc.astype(BF16), vb, preferred_element_type=F32)
                s_prev = s_ref[h]
                o = o + jnp.dot((q * qdec).astype(BF16), s_prev.astype(BF16), preferred_element_type=F32)
                kd_t = jnp.transpose(k * kdec).astype(BF16)
                s_ref[h] = cdec * s_prev + jnp.dot(kd_t, vb, preferred_element_type=F32)
                on = o * lax.rsqrt(jnp.mean(o * o, axis=-1, keepdims=True) + EPS)
                g = g_ref[rows, h * RET_DV:(h + 1) * RET_DV].astype(F32)
                yv = g * _sigmoid(g) * on
                if d == 0:
                    y_ref[rows, h * RET_DV:(h + 1) * RET_DV] = yv
                else:
                    y_ref[rows, h * RET_DV:(h + 1) * RET_DV] += yv
            return carry

        lax.fori_loop(0, nc, chunk, 0)
        if not has_s0:
            for h in range(RET_HEADS):
                sfin_ref[d, h] = s_ref[h]


def _retention(proj, ret_decay, nseq, n, s0, layer):
    has_s0 = s0 is not None
    dkb = RET_HEADS * RET_DK
    in_specs = [
        pl.BlockSpec((n, dkb), lambda i: (i, 3 * D_MODEL // dkb)),
        pl.BlockSpec((n, dkb), lambda i: (i, 3 * D_MODEL // dkb + 1)),
        pl.BlockSpec((n, D_MODEL), lambda i: (i, 4)),
        pl.BlockSpec((n, D_MODEL), lambda i: (i, 5)),
        pl.BlockSpec((n, D_MODEL), lambda i: (i, 6)),
        pl.BlockSpec((2, RET_HEADS), lambda i: (0, 0)),
    ]
    args = [proj, proj, proj, proj, proj, ret_decay]
    y_shape = jax.ShapeDtypeStruct((nseq * n, D_MODEL), F32)
    y_spec = pl.BlockSpec((n, D_MODEL), lambda i: (i, 0))
    if has_s0:
        in_specs.append(pl.BlockSpec((None, None, 2, RET_HEADS, RET_DK, RET_DV), lambda i: (i, layer, 0, 0, 0, 0)))
        args.append(s0)
        out_shape, out_specs = y_shape, y_spec
    else:
        out_shape = (y_shape, jax.ShapeDtypeStruct((nseq, 2, RET_HEADS, RET_DK, RET_DV), F32))
        out_specs = (y_spec, pl.BlockSpec((None, 2, RET_HEADS, RET_DK, RET_DV), lambda i: (i, 0, 0, 0, 0)))
    return pl.pallas_call(
        functools.partial(_ret_kernel, n=n, has_s0=has_s0),
        grid=(nseq,),
        in_specs=in_specs,
        out_specs=out_specs,
        out_shape=out_shape,
        scratch_shapes=[pltpu.VMEM((RET_HEADS, RET_DK, RET_DV), F32),
                        pltpu.VMEM((RET_HEADS, RET_CHUNK, RET_CHUNK), F32)],
        compiler_params=_params("parallel"),
        name="ret",
    )(*args)


def _merge_kernel(x_ref, ga_ref, gb_ref, gc_ref, ya_ref, yb_ref, yc_ref, mod_ref, g2_ref, w_ref, xo_ref, h2_ref):
    merged = (_sigmoid(ga_ref[...].astype(F32)) * ya_ref[...] + _sigmoid(gb_ref[...].astype(F32)) * yb_ref[...]
              + _sigmoid(gc_ref[...].astype(F32)) * yc_ref[...])
    o = jnp.dot(merged.astype(BF16), w_ref[...], preferred_element_type=F32)
    gt1 = mod_ref[:, 2 * D_MODEL:3 * D_MODEL]
    sh2 = mod_ref[:, 3 * D_MODEL:4 * D_MODEL]
    sc2 = mod_ref[:, 4 * D_MODEL:5 * D_MODEL]
    xn = x_ref[...] + gt1 * o
    xo_ref[...] = xn
    h2_ref[...] = (_rms(xn, g2_ref[...]) * (1.0 + sc2) + sh2).astype(BF16)


def _merge(x, proj, ya, yb, yc, mod4, row_fn, layer, norm_g2, w_out):
    n = x.shape[0]
    tb = MERGE_TB
    tok = lambda c: pl.BlockSpec((tb, D_MODEL), lambda i: (i, c))
    return pl.pallas_call(
        _merge_kernel,
        grid=(n // tb,),
        in_specs=[
            tok(0), tok(7), tok(8), tok(9), tok(0), tok(0), tok(0),
            pl.BlockSpec((None, None, 1, 6 * D_MODEL), lambda i: (layer, row_fn(i * tb), 0, 0)),
            pl.BlockSpec((1, D_MODEL), lambda i: (0, 0)),
            pl.BlockSpec((D_MODEL, D_MODEL), lambda i: (0, 0)),
        ],
        out_specs=(tok(0), tok(0)),
        out_shape=(jax.ShapeDtypeStruct((n, D_MODEL), F32), jax.ShapeDtypeStruct((n, D_MODEL), BF16)),
        compiler_params=_params("parallel"),
        name="merge",
    )(x, proj, proj, proj, ya, yb, yc, mod4, norm_g2, w_out)


_CAND_PAIRS = tuple((j1, j2) for j1 in range(PEER_TOPK) for j2 in range(PEER_TOPK)
                    if (j1 + 1) * (j2 + 1) <= PEER_TOPK)
ROUTE_ACCS = 4


def _scan_max(n_items, load, knock, store):
    per = -(-n_items // ROUTE_ACCS)
    accs = []
    for a0 in range(0, n_items, per):
        bv = bt = None
        for i in range(a0, min(a0 + per, n_items)):
            v, t = load(i)
            v = knock(i, v, t)
            store(i, v)
            if bv is None:
                bv, bt = v, t
            else:
                gt = v > bv
                bv = jnp.where(gt, v, bv)
                bt = jnp.where(gt, t, bt)
        accs.append((bv, bt))
    bv, bt = accs[0]
    for v, t in accs[1:]:
        gt = v > bv
        bv = jnp.where(gt, v, bv)
        bt = jnp.where(gt, t, bt)
    return bv, bt


def _route_kernel(h_ref, wq_ref, sk_ref, e_ref, g_ref, q_scr, x_scr, ts_scr, ti_scr, c_scr, k_scr, eo_scr, go_scr):
    tb = h_ref.shape[0]
    nk = PEER_NKEYS
    kd = PEER_DKEY // 2
    nlt = tb // LANES
    assert nlt == SUBLANES
    neg = -jnp.inf
    tile = lambda i: pl.ds(i * SUBLANES, SUBLANES)
    for hd in range(PEER_HEADS):
        cols = slice(hd * PEER_DKEY, (hd + 1) * PEER_DKEY)
        q = jnp.dot(h_ref[...], wq_ref[:, cols], preferred_element_type=F32).astype(BF16)
        q_scr[2 * hd] = q[:, :kd]
        q_scr[2 * hd + 1] = q[:, kd:]

    def side(hs, carry):
        for j in range(nlt):
            sc = _nt_dot(sk_ref[hs], q_scr[hs, j * LANES:(j + 1) * LANES, :])
            x_scr[pl.ds(j, nk, stride=nlt), :] = sc
        prev = None
        for r in range(PEER_TOPK):
            load = lambda n: (x_scr[tile(n), :], jnp.full((SUBLANES, LANES), n, I32))
            if prev is None:
                knock = lambda n, v, t: v
                store = lambda n, v: None
            else:
                knock = lambda n, v, t, prev=prev: jnp.where(prev == n, neg, v)
                store = lambda n, v: x_scr.__setitem__((tile(n), slice(None)), v)
            bv, bi = _scan_max(nk, load, knock, store)
            ts_scr[hs, r] = bv
            ti_scr[hs, r] = bi
            prev = bi
        return carry

    lax.fori_loop(0, 2 * PEER_HEADS, side, 0)

    def head(h, carry):
        for p, (j1, j2) in enumerate(_CAND_PAIRS):
            c_scr[tile(p), :] = ts_scr[2 * h, j1] + ts_scr[2 * h + 1, j2]
            k_scr[tile(p), :] = ((j1 * PEER_TOPK + j2) * (nk * nk) + ti_scr[2 * h, j1] * nk
                                 + ti_scr[2 * h + 1, j2])
        prev = None
        vals, exps = [], []
        for r in range(PEER_TOPK):
            load = lambda p: (c_scr[tile(p), :], k_scr[tile(p), :])
            if prev is None:
                knock = lambda p, v, t: v
                store = lambda p, v: None
            else:
                knock = lambda p, v, t, prev=prev: jnp.where(t == prev, neg, v)
                store = lambda p, v: c_scr.__setitem__((tile(p), slice(None)), v)
            bv, bk = _scan_max(len(_CAND_PAIRS), load, knock, store)
            vals.append(bv)
            exps.append(bk & (nk * nk - 1))
            prev = bk
        ps = [jnp.exp(v - vals[0]) for v in vals]
        z = _tree(jnp.add, ps)
        for r in range(PEER_TOPK):
            row = pl.ds(pl.multiple_of((h * PEER_TOPK + r) * SUBLANES, SUBLANES), SUBLANES)
            go_scr[row, :] = ps[r] / z
            eo_scr[row, :] = exps[r]
        return carry

    lax.fori_loop(0, PEER_HEADS, head, 0)
    nsel = PEER_HEADS * PEER_TOPK
    for j in range(nlt):
        rows = slice(j * LANES, (j + 1) * LANES)
        e_ref[rows, :] = jnp.transpose(eo_scr[pl.ds(j, nsel, stride=nlt), :])
        g_ref[rows, :] = jnp.transpose(go_scr[pl.ds(j, nsel, stride=nlt), :])


def _route(h2, wq, subkeys):
    n = h2.shape[0]
    tb = ROUTE_TB
    nsel = PEER_HEADS * PEER_TOPK
    kd = PEER_DKEY // 2
    tile_rows = lambda count: count * SUBLANES
    return pl.pallas_call(
        _route_kernel,
        grid=(n // tb,),
        in_specs=[
            pl.BlockSpec((tb, D_MODEL), lambda i: (i, 0)),
            pl.BlockSpec((D_MODEL, PEER_HEADS * PEER_DKEY), lambda i: (0, 0)),
            pl.BlockSpec((2 * PEER_HEADS, PEER_NKEYS, kd), lambda i: (0, 0, 0)),
        ],
        out_specs=(pl.BlockSpec((tb, nsel), lambda i: (i, 0)), pl.BlockSpec((tb, nsel), lambda i: (i, 0))),
        out_shape=(jax.ShapeDtypeStruct((n, nsel), I32), jax.ShapeDtypeStruct((n, nsel), F32)),
        scratch_shapes=[
            pltpu.VMEM((2 * PEER_HEADS, tb, kd), BF16),
            pltpu.VMEM((tile_rows(PEER_NKEYS), LANES), F32),
            pltpu.VMEM((2 * PEER_HEADS, PEER_TOPK, SUBLANES, LANES), F32),
            pltpu.VMEM((2 * PEER_HEADS, PEER_TOPK, SUBLANES, LANES), I32),
            pltpu.VMEM((tile_rows(len(_CAND_PAIRS)), LANES), F32),
            pltpu.VMEM((tile_rows(len(_CAND_PAIRS)), LANES), I32),
            pltpu.VMEM((tile_rows(nsel), LANES), I32),
            pltpu.VMEM((tile_rows(nsel), LANES), F32),
        ],
        compiler_params=_params("parallel"),
        name="route",
    )(h2, wq, subkeys)


def _peer_kernel(x_ref, h_ref, e_ref, g_ref, mod_ref, fg_ref, u_ref, v_ref, o_ref, m_ref, p0_ref, p1_ref, acc_ref,
                 *, final, n_chunks):
    tb = x_ref.shape[0]
    nk = PEER_NKEYS
    c = pl.program_id(1)
    chunks_per_pass = n_chunks // PEER_PASSES

    @pl.when(jnp.logical_and(c % chunks_per_pass == 0, c < n_chunks))
    def _():
        a_iota = lax.broadcasted_iota(I32, (M_ROWS, nk), 0) + (c // chunks_per_pass) * M_ROWS
        b_iota = lax.broadcasted_iota(I32, (nk, nk), 0)
        zero = jnp.zeros((M_ROWS, nk), BF16)

        def group(gi, carry):
            t0 = pl.multiple_of(gi * M_GROUP, M_GROUP)
            er = e_ref[pl.ds(t0, M_GROUP), :]
            gr = g_ref[pl.ds(t0, M_GROUP), :]
            for pair in range(M_GROUP // 2):
                at, bt = [], []
                for tt in (2 * pair, 2 * pair + 1):
                    e1 = er[tt:tt + 1, :]
                    at.append(jnp.where(a_iota == (e1 >> 7), gr[tt:tt + 1, :], 0.0).astype(BF16))
                    bt.append(jnp.where(b_iota == (e1 & (nk - 1)), 1.0, 0.0).astype(BF16))
                lhs = jnp.concatenate([jnp.concatenate([at[0], zero], axis=1),
                                       jnp.concatenate([zero, at[1]], axis=1)], axis=0)
                out = _nt_dot(lhs, jnp.concatenate(bt, axis=1))
                r0 = pl.multiple_of((t0 + 2 * pair) * M_PITCH, SUBLANES)
                m_ref[pl.ds(r0, M_ROWS), :] = out[:M_ROWS, :]
                m_ref[pl.ds(r0 + M_PITCH, M_ROWS), :] = out[M_ROWS:, :]
            return carry

        lax.fori_loop(0, tb // M_GROUP, group, 0)

    def step(prev_ref, cur_ref):
        n_sub = PEER_CH // PEER_SUB
        nw = D_MODEL // n_sub
        per = PEER_SUB // nk
        a_base = (c % chunks_per_pass) * (PEER_CH // nk)
        for sub in range(n_sub):
            if cur_ref is not None:
                rows = slice(sub * PEER_SUB, (sub + 1) * PEER_SUB)
                s = _gelu(_nt_dot(h_ref[...], u_ref[rows, :]))
                m = jnp.concatenate(
                    [m_ref[pl.ds(a_base + sub * per + a, tb, stride=M_PITCH), :] for a in range(per)], axis=1)
                cur_ref[:, rows] = (m * s).astype(BF16)
            if prev_ref is not None:
                cols = slice(sub * nw, (sub + 1) * nw)
                acc_ref[:, cols] += jnp.dot(prev_ref[...], v_ref[:, cols], preferred_element_type=F32)

    stage = (p0_ref, p1_ref)
    steady = jnp.logical_and(c > 0, c < n_chunks)

    @pl.when(c == 0)
    def _():
        acc_ref[...] = jnp.zeros_like(acc_ref)
        step(None, stage[0])

    for parity in range(2):
        @pl.when(jnp.logical_and(steady, c % 2 == parity))
        def _(parity=parity):
            step(stage[1 - parity], stage[parity])

    @pl.when(c == n_chunks)
    def _():
        step(stage[(n_chunks - 1) % 2], None)
        gt2 = mod_ref[:, 5 * D_MODEL:6 * D_MODEL]
        xn = x_ref[...] + gt2 * acc_ref[...]
        o_ref[...] = _rms(xn, fg_ref[...]) if final else xn


def _peer(x, h2, e, g, mod4, row_fn, layer, final_g, u_tab, v_tab, final):
    n = x.shape[0]
    tb = PEER_TB
    nsel = PEER_HEADS * PEER_TOPK
    n_chunks = u_tab.shape[0] // PEER_CH
    return pl.pallas_call(
        functools.partial(_peer_kernel, final=final, n_chunks=n_chunks),
        grid=(n // tb, n_chunks + 1),
        in_specs=[
            pl.BlockSpec((tb, D_MODEL), lambda i, c: (i, 0)),
            pl.BlockSpec((tb, D_MODEL), lambda i, c: (i, 0)),
            pl.BlockSpec((tb, nsel), lambda i, c: (i, 0)),
            pl.BlockSpec((tb, nsel), lambda i, c: (i, 0)),
            pl.BlockSpec((None, None, 1, 6 * D_MODEL), lambda i, c: (layer, row_fn(i * tb), 0, 0)),
            pl.BlockSpec((1, D_MODEL), lambda i, c: (0, 0)),
            pl.BlockSpec((PEER_CH, D_MODEL), lambda i, c: (jnp.minimum(c, n_chunks - 1), 0)),
            pl.BlockSpec((PEER_CH, D_MODEL), lambda i, c: (jnp.maximum(c - 1, 0), 0)),
        ],
        out_specs=pl.BlockSpec((tb, D_MODEL), lambda i, c: (i, 0)),
        out_shape=jax.ShapeDtypeStruct((n, D_MODEL), F32),
        scratch_shapes=[pltpu.VMEM((tb * M_PITCH, LANES), F32),
                        pltpu.VMEM((tb, PEER_CH), BF16),
                        pltpu.VMEM((tb, PEER_CH), BF16),
                        pltpu.VMEM((tb, D_MODEL), F32)],
        compiler_params=_params("parallel", "arbitrary"),
        name="peer",
    )(x, h2, e, g, mod4, final_g, u_tab, v_tab)


def kernel(x_prompt, x_sample, state_ret, c, c_ctx, w_mod, b_mod, norm_g, w_in, w_out, gmlp_ws, gmlp_b,
           gmlp_ln_g, pool_w, pool_scale, ret_decay, peer_wq, peer_subkeys, peer_u, peer_v, final_norm_g):
    batch, seq, d = x_prompt.shape
    dec_batch, dec_seq, _ = x_sample.shape
    depth = w_mod.shape[0]
    assert d == D_MODEL and dec_batch + 1 <= MOD_ROWS and dec_seq % GRID_W == 0

    cond = jnp.zeros((MOD_ROWS, d), F32).at[0].set(c_ctx).at[1:1 + dec_batch].set(c)
    mod = _modulation(cond, w_mod, b_mod)
    mod4 = mod.reshape(depth, MOD_ROWS, 1, 6 * d)
    ctx_row = lambda tok0: 0
    lat_row = lambda tok0: 1 + tok0 // dec_seq

    xc = x_prompt.reshape(batch * seq, d)
    xs = x_sample.reshape(dec_batch * dec_seq, d)
    fg = final_norm_g.reshape(1, d)
    new_states = []
    for l in range(depth):
        last = l == depth - 1
        w_in_l = w_in[l].astype(BF16)
        w_out_l = w_out[l].astype(BF16)
        ws_l = gmlp_ws[l].astype(BF16)
        bs_t = gmlp_b[l].T
        ln_g = gmlp_ln_g[l].reshape(1, d)
        pw_l = pool_w[l].astype(BF16)
        ps_l = pool_scale[l].reshape(1, d)
        wq_l = peer_wq[l].astype(BF16)
        sk_l = peer_subkeys[l].reshape(2 * PEER_HEADS, PEER_NKEYS, PEER_DKEY // 2).astype(BF16)
        u_l = peer_u[l].astype(BF16)
        v_l = peer_v[l].astype(BF16)
        g1 = norm_g[l, 0].reshape(1, d)
        g2 = norm_g[l, 1].reshape(1, d)

        def path(x, nseq, n, row_fn, grid, s0):
            proj = _proj(x, mod4, row_fn, l, g1, w_in_l)
            ya = _gmlp(proj, ws_l, bs_t, ln_g)
            yb = _pool(proj, pw_l, ps_l, nseq, n, grid)
            ret = _retention(proj, ret_decay[l], nseq, n, s0, l)
            yc, s_fin = (ret, None) if s0 is not None else ret
            xn, h2 = _merge(x, proj, ya, yb, yc, mod4, row_fn, l, g2, w_out_l)
            e, g = _route(h2, wq_l, sk_l)
            return _peer(xn, h2, e, g, mod4, row_fn, l, fg, u_l, v_l, last), s_fin

        xc, s_fin = path(xc, batch, seq, ctx_row, False, None)
        new_states.append(s_fin)
        xs, _ = path(xs, dec_batch, dec_seq, lat_row, True, state_ret)

    new_state_ret = jnp.stack(new_states, axis=1).astype(x_prompt.dtype)
    return (xc.reshape(batch, seq, d), xs.reshape(dec_batch, dec_seq, d), new_state_ret)
```

```python
import functools

import numpy as np
import jax
import jax.numpy as jnp
from jax import lax
from jax.experimental import pallas as pl
from jax.experimental.pallas import tpu as pltpu

F32 = jnp.float32
BF16 = jnp.bfloat16
I32 = jnp.int32

D_MODEL = 1024
A_GROUPS = 4
A_CHUNK = 128
POOL_WINDOWS = (2, 4, 8, 16)
B_GW = D_MODEL // len(POOL_WINDOWS)
GRID_W = 64
RET_HEADS = 4
RET_DV = D_MODEL // RET_HEADS
RET_DK = RET_DV // 2
RET_CHUNK = 128
PEER_HEADS = 8
PEER_NKEYS = 128
PEER_DKEY = 256
PEER_TOPK = 16
EPS = 1e-6
IN_COLS = 10 * D_MODEL

SUBLANES = 8
LANES = 128
VMEM_LIMIT = 56 * 1024 * 1024

MOD_ROWS = 16
PROJ_TB = 512
PROJ_TN = 2048
PROJ_DTYPE = BF16
Y_DTYPE = BF16
GMLP_TB = 256
MERGE_TB = 512
ROUTE_TB = SUBLANES * LANES
PEER_TB = 512
PEER_CH = 1024
PEER_SUB = 256
PEER_PASSES = 2
M_ROWS = PEER_NKEYS // PEER_PASSES
M_PITCH = M_ROWS + SUBLANES
M_GROUP = 32


def _gelu(x):
    return 0.5 * x * (1.0 + jnp.tanh(0.7978845608028654 * (x + 0.044715 * (x * x * x))))


def _sigmoid(x):
    return 1.0 / (1.0 + jnp.exp(-x))


def _rms(x, g):
    return x * lax.rsqrt(jnp.mean(x * x, axis=-1, keepdims=True) + EPS) * g


def _params(*sem):
    return pltpu.CompilerParams(dimension_semantics=sem, vmem_limit_bytes=VMEM_LIMIT)


def _tree(fn, xs):
    xs = list(xs)
    while len(xs) > 1:
        xs = [fn(xs[i], xs[i + 1]) for i in range(0, len(xs) - 1, 2)] + ([xs[-1]] if len(xs) % 2 else [])
    return xs[0]


def _nt_dot(a, b):
    return lax.dot_general(a, b, (((1,), (1,)), ((), ())), preferred_element_type=F32)


def _mod_kernel(cond_ref, w_ref, b_ref, o_ref):
    c = cond_ref[...]
    s = (c * _sigmoid(c)).astype(BF16)
    o_ref[...] = jnp.dot(s, w_ref[...].astype(BF16), preferred_element_type=F32) + b_ref[...]


def _modulation(cond, w_mod, b_mod):
    depth = w_mod.shape[0]
    nj = w_mod.shape[2] // D_MODEL
    return pl.pallas_call(
        _mod_kernel,
        grid=(depth, nj),
        in_specs=[
            pl.BlockSpec((MOD_ROWS, D_MODEL), lambda l, j: (0, 0)),
            pl.BlockSpec((None, D_MODEL, D_MODEL), lambda l, j: (l, 0, j)),
            pl.BlockSpec((None, 1, D_MODEL), lambda l, j: (l, 0, j)),
        ],
        out_specs=pl.BlockSpec((None, MOD_ROWS, D_MODEL), lambda l, j: (l, 0, j)),
        out_shape=jax.ShapeDtypeStruct((depth, MOD_ROWS, nj * D_MODEL), F32),
        compiler_params=_params("parallel", "parallel"),
        name="mod",
    )(cond, w_mod, b_mod.reshape(depth, 1, nj * D_MODEL))


def _proj_kernel(x_ref, mod_ref, g_ref, w_ref, o_ref, wb_ref):
    @pl.when(pl.program_id(1) == 0)
    def _():
        wb_ref[...] = w_ref[...].astype(BF16)

    y = _rms(x_ref[...], g_ref[...])
    sh = mod_ref[:, 0:D_MODEL]
    sc = mod_ref[:, D_MODEL:2 * D_MODEL]
    h = (y * (1.0 + sc) + sh).astype(BF16)
    o_ref[...] = jnp.dot(h, wb_ref[...], preferred_element_type=F32).astype(o_ref.dtype)


def _proj(x, mod4, row_fn, layer, norm_g, w_in):
    n = x.shape[0]
    tb, tn = PROJ_TB, PROJ_TN
    return pl.pallas_call(
        _proj_kernel,
        grid=(IN_COLS // tn, n // tb),
        in_specs=[
            pl.BlockSpec((tb, D_MODEL), lambda j, i: (i, 0)),
            pl.BlockSpec((None, None, 1, 6 * D_MODEL), lambda j, i: (layer, row_fn(i * tb), 0, 0)),
            pl.BlockSpec((None, None, 1, D_MODEL), lambda j, i: (layer, 0, 0, 0)),
            pl.BlockSpec((None, D_MODEL, tn), lambda j, i: (layer, 0, j)),
        ],
        out_specs=pl.BlockSpec((tb, tn), lambda j, i: (i, j)),
        out_shape=jax.ShapeDtypeStruct((n, IN_COLS), PROJ_DTYPE),
        scratch_shapes=[pltpu.VMEM((D_MODEL, tn), BF16)],
        compiler_params=_params("parallel", "arbitrary"),
        name="proj",
    )(x, mod4, norm_g, w_in)


def _gmlp_kernel(u_ref, v_ref, ws_ref, bs_ref, lng_ref, o_ref):
    gw = D_MODEL // A_GROUPS
    for c in range(GMLP_TB // A_CHUNK):
        rows = slice(c * A_CHUNK, (c + 1) * A_CHUNK)
        v = _gelu(v_ref[rows, :].astype(F32))
        vc = v - jnp.mean(v, axis=-1, keepdims=True)
        vn = vc * lax.rsqrt(jnp.mean(vc * vc, axis=-1, keepdims=True) + EPS) * lng_ref[...]
        vnb = vn.astype(BF16)
        for g in range(A_GROUPS):
            cols = slice(g * gw, (g + 1) * gw)
            sv = jnp.dot(ws_ref[g].astype(BF16), vnb[:, cols], preferred_element_type=F32) + bs_ref[:, g:g + 1]
            o_ref[rows, cols] = (_gelu(u_ref[rows, cols].astype(F32)) * sv).astype(o_ref.dtype)


def _gmlp(proj, layer, ws, bs_t, ln_g):
    n = proj.shape[0]
    tb = GMLP_TB
    return pl.pallas_call(
        _gmlp_kernel,
        grid=(n // tb,),
        in_specs=[
            pl.BlockSpec((tb, D_MODEL), lambda i: (i, 0)),
            pl.BlockSpec((tb, D_MODEL), lambda i: (i, 1)),
            pl.BlockSpec((None, A_GROUPS, A_CHUNK, A_CHUNK), lambda i: (layer, 0, 0, 0)),
            pl.BlockSpec((None, A_CHUNK, A_GROUPS), lambda i: (layer, 0, 0)),
            pl.BlockSpec((None, 1, D_MODEL), lambda i: (layer, 0, 0)),
        ],
        out_specs=pl.BlockSpec((tb, D_MODEL), lambda i: (i, 0)),
        out_shape=jax.ShapeDtypeStruct((n, D_MODEL), Y_DTYPE),
        compiler_params=_params("parallel"),
        name="gmlp",
    )(proj, proj, ws, bs_t, ln_g)


POOL_PAD = 16


def _window_count(pos, size, a, b):
    return jnp.minimum(pos + b, size - 1) - jnp.maximum(pos - a, 0) + 1


def _pool_kernel(z_ref, w_ref, scale_ref, o_ref, zp_ref, cp_ref, *, n, grid):
    rows = n // GRID_W
    rpad = cp_ref.shape[0] - n
    tok = lax.broadcasted_iota(I32, (n, 1), 0)
    for gi, w in enumerate(POOL_WINDOWS):
        a = w // 2
        b = w - 1 - a
        cols = slice(gi * B_GW, (gi + 1) * B_GW)
        z = z_ref[:, cols].astype(F32)
        zp_ref[0:POOL_PAD, :] = jnp.zeros((POOL_PAD, B_GW), F32)
        zp_ref[POOL_PAD + n:, :] = jnp.zeros((POOL_PAD, B_GW), F32)
        zp_ref[POOL_PAD:POOL_PAD + n, :] = z
        if grid:
            col = tok & (GRID_W - 1)
            row = tok >> (GRID_W.bit_length() - 1)
            s = jnp.zeros((n, B_GW), F32)
            for d in range(-a, b + 1):
                sh = zp_ref[POOL_PAD + d:POOL_PAD + d + n, :]
                ok = jnp.logical_and(col + d >= 0, col + d < GRID_W)
                s = s + jnp.where(ok, sh, 0.0)
            half = rpad // 2
            cp_ref[0:half, :] = jnp.zeros((half, B_GW), F32)
            cp_ref[half + n:, :] = jnp.zeros((half, B_GW), F32)
            cp_ref[half:half + n, :] = s
            s = jnp.zeros((n, B_GW), F32)
            for d in range(-a, b + 1):
                s = s + cp_ref[half + d * GRID_W:half + d * GRID_W + n, :]
            cnt = (_window_count(row, rows, a, b) * _window_count(col, GRID_W, a, b)).astype(F32)
        else:
            s = jnp.zeros((n, B_GW), F32)
            for d in range(-a, b + 1):
                s = s + zp_ref[POOL_PAD + d:POOL_PAD + d + n, :]
            cnt = _window_count(tok, n, a, b).astype(F32)
        diff = (s / cnt - z).astype(BF16)
        o_ref[:, cols] = (jnp.dot(diff, w_ref[gi].astype(BF16), preferred_element_type=F32)
                          * scale_ref[:, cols]).astype(o_ref.dtype)


def _pool(proj, layer, pool_w, pool_scale, nseq, n, grid):
    rpad = 2 * (max(POOL_WINDOWS) // 2) * GRID_W if grid else 2 * SUBLANES
    return pl.pallas_call(
        functools.partial(_pool_kernel, n=n, grid=grid),
        grid=(nseq,),
        in_specs=[
            pl.BlockSpec((n, D_MODEL), lambda i: (i, 2)),
            pl.BlockSpec((None, len(POOL_WINDOWS), B_GW, B_GW), lambda i: (layer, 0, 0, 0)),
            pl.BlockSpec((None, 1, D_MODEL), lambda i: (layer, 0, 0)),
        ],
        out_specs=pl.BlockSpec((n, D_MODEL), lambda i: (i, 0)),
        out_shape=jax.ShapeDtypeStruct((nseq * n, D_MODEL), Y_DTYPE),
        scratch_shapes=[pltpu.VMEM((n + 2 * POOL_PAD, B_GW), F32), pltpu.VMEM((n + rpad, B_GW), F32)],
        compiler_params=_params("parallel"),
        name="pool",
    )(proj, pool_w, pool_scale)


def _log_sigmoid(x):
    return jnp.minimum(x, 0.0) - jnp.log(1.0 + jnp.exp(-jnp.abs(x)))


def _ret_kernel(*refs, n, has_s0):
    if has_s0:
        q_ref, k_ref, v_ref, gf_ref, gb_ref, rd_ref, s0_ref, y_ref, s_ref, dm_ref, yf_ref = refs
        sfin_ref = None
    else:
        q_ref, k_ref, v_ref, gf_ref, gb_ref, rd_ref, y_ref, sfin_ref, s_ref, dm_ref, yf_ref = refs
        s0_ref = None
    cl = RET_CHUNK
    nc = n // cl
    pi = lax.broadcasted_iota(I32, (cl, cl), 0).astype(F32)
    pj = lax.broadcasted_iota(I32, (cl, cl), 1).astype(F32)
    pcol = lax.broadcasted_iota(I32, (cl, 1), 0).astype(F32)
    kscale = RET_DK ** -0.5

    for d in range(2):
        g_ref = gf_ref if d == 0 else gb_ref
        lgs = []
        for h in range(RET_HEADS):
            lg = _log_sigmoid(rd_ref[d:d + 1, h:h + 1])
            lgs.append(lg)
            rel = (pi - pj) if d == 0 else (pj - pi)
            dm_ref[h] = jnp.where(rel >= 0.0, jnp.exp(lg * jnp.maximum(rel, 0.0)), 0.0)
            if has_s0:
                s_ref[h] = s0_ref[d, h]
            else:
                s_ref[h] = jnp.zeros((RET_DK, RET_DV), F32)

        def chunk(ci, carry, d=d, g_ref=g_ref, lgs=lgs):
            c = ci if d == 0 else nc - 1 - ci
            rows = pl.ds(pl.multiple_of(c * cl, cl), cl)
            for h in range(RET_HEADS):
                lg = lgs[h]
                if d == 0:
                    qdec = jnp.exp(lg * (pcol + 1.0))
                    kdec = jnp.exp(lg * (cl - 1.0 - pcol))
                else:
                    qdec = jnp.exp(lg * (cl - pcol))
                    kdec = jnp.exp(lg * pcol)
                cdec = jnp.exp(lg * float(cl))
                q = q_ref[rows, h * RET_DK:(h + 1) * RET_DK].astype(F32)
                k = k_ref[rows, h * RET_DK:(h + 1) * RET_DK].astype(F32) * kscale
                vb = v_ref[rows, h * RET_DV:(h + 1) * RET_DV].astype(BF16)
                sc = _nt_dot(q.astype(BF16), k.astype(BF16)) * dm_ref[h]
                o = jnp.dot(sc.astype(BF16), vb, preferred_element_type=F32)
                s_prev = s_ref[h]
                o = o + jnp.dot((q * qdec).astype(BF16), s_prev.astype(BF16), preferred_element_type=F32)
                kd_t = jnp.transpose(k * kdec).astype(BF16)
                s_ref[h] = cdec * s_prev + jnp.dot(kd_t, vb, preferred_element_type=F32)
                on = o * lax.rsqrt(jnp.mean(o * o, axis=-1, keepdims=True) + EPS)
                g = g_ref[rows, h * RET_DV:(h + 1) * RET_DV].astype(F32)
                yv = g * _sigmoid(g) * on
                cols = slice(h * RET_DV, (h + 1) * RET_DV)
                if d == 0:
                    yf_ref[rows, cols] = yv
                else:
                    y_ref[rows, cols] = (yf_ref[rows, cols] + yv).astype(y_ref.dtype)
            return carry

        lax.fori_loop(0, nc, chunk, 0)
        if not has_s0:
            for h in range(RET_HEADS):
                sfin_ref[d, h] = s_ref[h]


def _retention(proj, ret_decay, nseq, n, s0, layer):
    has_s0 = s0 is not None
    dkb = RET_HEADS * RET_DK
    in_specs = [
        pl.BlockSpec((n, dkb), lambda i: (i, 3 * D_MODEL // dkb)),
        pl.BlockSpec((n, dkb), lambda i: (i, 3 * D_MODEL // dkb + 1)),
        pl.BlockSpec((n, D_MODEL), lambda i: (i, 4)),
        pl.BlockSpec((n, D_MODEL), lambda i: (i, 5)),
        pl.BlockSpec((n, D_MODEL), lambda i: (i, 6)),
        pl.BlockSpec((None, 2, RET_HEADS), lambda i: (layer, 0, 0)),
    ]
    args = [proj, proj, proj, proj, proj, ret_decay]
    y_shape = jax.ShapeDtypeStruct((nseq * n, D_MODEL), Y_DTYPE)
    y_spec = pl.BlockSpec((n, D_MODEL), lambda i: (i, 0))
    if has_s0:
        in_specs.append(pl.BlockSpec((None, None, 2, RET_HEADS, RET_DK, RET_DV), lambda i: (i, layer, 0, 0, 0, 0)))
        args.append(s0)
        out_shape, out_specs = y_shape, y_spec
    else:
        out_shape = (y_shape, jax.ShapeDtypeStruct((nseq, 2, RET_HEADS, RET_DK, RET_DV), F32))
        out_specs = (y_spec, pl.BlockSpec((None, 2, RET_HEADS, RET_DK, RET_DV), lambda i: (i, 0, 0, 0, 0)))
    return pl.pallas_call(
        functools.partial(_ret_kernel, n=n, has_s0=has_s0),
        grid=(nseq,),
        in_specs=in_specs,
        out_specs=out_specs,
        out_shape=out_shape,
        scratch_shapes=[pltpu.VMEM((RET_HEADS, RET_DK, RET_DV), F32),
                        pltpu.VMEM((RET_HEADS, RET_CHUNK, RET_CHUNK), F32),
                        pltpu.VMEM((n, D_MODEL), F32)],
        compiler_params=_params("parallel"),
        name="ret",
    )(*args)


def _merge_kernel(x_ref, ga_ref, gb_ref, gc_ref, ya_ref, yb_ref, yc_ref, mod_ref, g2_ref, w_ref, xo_ref, h2_ref):
    merged = (_sigmoid(ga_ref[...].astype(F32)) * ya_ref[...] + _sigmoid(gb_ref[...].astype(F32)) * yb_ref[...]
              + _sigmoid(gc_ref[...].astype(F32)) * yc_ref[...])
    o = jnp.dot(merged.astype(BF16), w_ref[...].astype(BF16), preferred_element_type=F32)
    gt1 = mod_ref[:, 2 * D_MODEL:3 * D_MODEL]
    sh2 = mod_ref[:, 3 * D_MODEL:4 * D_MODEL]
    sc2 = mod_ref[:, 4 * D_MODEL:5 * D_MODEL]
    xn = x_ref[...] + gt1 * o
    xo_ref[...] = xn
    h2_ref[...] = (_rms(xn, g2_ref[...]) * (1.0 + sc2) + sh2).astype(BF16)


def _merge(x, proj, ya, yb, yc, mod4, row_fn, layer, norm_g2, w_out):
    n = x.shape[0]
    tb = MERGE_TB
    tok = lambda c: pl.BlockSpec((tb, D_MODEL), lambda i: (i, c))
    return pl.pallas_call(
        _merge_kernel,
        grid=(n // tb,),
        in_specs=[
            tok(0), tok(7), tok(8), tok(9), tok(0), tok(0), tok(0),
            pl.BlockSpec((None, None, 1, 6 * D_MODEL), lambda i: (layer, row_fn(i * tb), 0, 0)),
            pl.BlockSpec((None, None, 1, D_MODEL), lambda i: (layer, 1, 0, 0)),
            pl.BlockSpec((None, D_MODEL, D_MODEL), lambda i: (layer, 0, 0)),
        ],
        out_specs=(tok(0), tok(0)),
        out_shape=(jax.ShapeDtypeStruct((n, D_MODEL), F32), jax.ShapeDtypeStruct((n, D_MODEL), BF16)),
        compiler_params=_params("parallel"),
        name="merge",
    )(x, proj, proj, proj, ya, yb, yc, mod4, norm_g2, w_out)


_CAND_PAIRS = tuple((j1, j2) for j1 in range(PEER_TOPK) for j2 in range(PEER_TOPK)
                    if (j1 + 1) * (j2 + 1) <= PEER_TOPK)
ROUTE_ACCS = 4


def _scan_max(n_items, load, knock, store):
    per = -(-n_items // ROUTE_ACCS)
    accs = []
    for a0 in range(0, n_items, per):
        bv = bt = None
        for i in range(a0, min(a0 + per, n_items)):
            v, t = load(i)
            v = knock(i, v, t)
            store(i, v)
            if bv is None:
                bv, bt = v, t
            else:
                gt = v > bv
                bv = jnp.where(gt, v, bv)
                bt = jnp.where(gt, t, bt)
        accs.append((bv, bt))
    bv, bt = accs[0]
    for v, t in accs[1:]:
        gt = v > bv
        bv = jnp.where(gt, v, bv)
        bt = jnp.where(gt, t, bt)
    return bv, bt


def _route_kernel(h_ref, wq_ref, sk_ref, e_ref, g_ref, q_scr, x_scr, ts_scr, ti_scr, c_scr, k_scr, eo_scr, go_scr):
    tb = h_ref.shape[0]
    nk = PEER_NKEYS
    kd = PEER_DKEY // 2
    nlt = tb // LANES
    assert nlt == SUBLANES
    neg = -jnp.inf
    tile = lambda i: pl.ds(i * SUBLANES, SUBLANES)
    for hd in range(PEER_HEADS):
        cols = slice(hd * PEER_DKEY, (hd + 1) * PEER_DKEY)
        q = jnp.dot(h_ref[...], wq_ref[:, cols].astype(BF16), preferred_element_type=F32).astype(BF16)
        q_scr[2 * hd] = q[:, :kd]
        q_scr[2 * hd + 1] = q[:, kd:]

    def side(hs, carry):
        sk = sk_ref[hs].astype(BF16)
        for j in range(nlt):
            sc = _nt_dot(sk, q_scr[hs, j * LANES:(j + 1) * LANES, :])
            x_scr[pl.ds(j, nk, stride=nlt), :] = sc
        prev = None
        for r in range(PEER_TOPK):
            load = lambda n: (x_scr[tile(n), :], jnp.full((SUBLANES, LANES), n, I32))
            if prev is None:
                knock = lambda n, v, t: v
                store = lambda n, v: None
            else:
                knock = lambda n, v, t, prev=prev: jnp.where(prev == n, neg, v)
                store = lambda n, v: x_scr.__setitem__((tile(n), slice(None)), v)
            bv, bi = _scan_max(nk, load, knock, store)
            ts_scr[hs, r] = bv
            ti_scr[hs, r] = bi
            prev = bi
        return carry

    lax.fori_loop(0, 2 * PEER_HEADS, side, 0)

    def head(h, carry):
        for p, (j1, j2) in enumerate(_CAND_PAIRS):
            c_scr[tile(p), :] = ts_scr[2 * h, j1] + ts_scr[2 * h + 1, j2]
            k_scr[tile(p), :] = ((j1 * PEER_TOPK + j2) * (nk * nk) + ti_scr[2 * h, j1] * nk
                                 + ti_scr[2 * h + 1, j2])
        prev = None
        vals, exps = [], []
        for r in range(PEER_TOPK):
            load = lambda p: (c_scr[tile(p), :], k_scr[tile(p), :])
            if prev is None:
                knock = lambda p, v, t: v
                store = lambda p, v: None
            else:
                knock = lambda p, v, t, prev=prev: jnp.where(t == prev, neg, v)
                store = lambda p, v: c_scr.__setitem__((tile(p), slice(None)), v)
            bv, bk = _scan_max(len(_CAND_PAIRS), load, knock, store)
            vals.append(bv)
            exps.append(bk & (nk * nk - 1))
            prev = bk
        ps = [jnp.exp(v - vals[0]) for v in vals]
        z = _tree(jnp.add, ps)
        for r in range(PEER_TOPK):
            row = pl.ds(pl.multiple_of((h * PEER_TOPK + r) * SUBLANES, SUBLANES), SUBLANES)
            go_scr[row, :] = ps[r] / z
            eo_scr[row, :] = exps[r]
        return carry

    lax.fori_loop(0, PEER_HEADS, head, 0)
    nsel = PEER_HEADS * PEER_TOPK
    for j in range(nlt):
        rows = slice(j * LANES, (j + 1) * LANES)
        e_ref[rows, :] = jnp.transpose(eo_scr[pl.ds(j, nsel, stride=nlt), :])
        g_ref[rows, :] = jnp.transpose(go_scr[pl.ds(j, nsel, stride=nlt), :])


def _route(h2, layer, wq, subkeys):
    n = h2.shape[0]
    tb = ROUTE_TB
    nsel = PEER_HEADS * PEER_TOPK
    kd = PEER_DKEY // 2
    tile_rows = lambda count: count * SUBLANES
    return pl.pallas_call(
        _route_kernel,
        grid=(n // tb,),
        in_specs=[
            pl.BlockSpec((tb, D_MODEL), lambda i: (i, 0)),
            pl.BlockSpec((None, D_MODEL, PEER_HEADS * PEER_DKEY), lambda i: (layer, 0, 0)),
            pl.BlockSpec((None, 2 * PEER_HEADS, PEER_NKEYS, kd), lambda i: (layer, 0, 0, 0)),
        ],
        out_specs=(pl.BlockSpec((tb, nsel), lambda i: (i, 0)), pl.BlockSpec((tb, nsel), lambda i: (i, 0))),
        out_shape=(jax.ShapeDtypeStruct((n, nsel), I32), jax.ShapeDtypeStruct((n, nsel), F32)),
        scratch_shapes=[
            pltpu.VMEM((2 * PEER_HEADS, tb, kd), BF16),
            pltpu.VMEM((tile_rows(PEER_NKEYS), LANES), F32),
            pltpu.VMEM((2 * PEER_HEADS, PEER_TOPK, SUBLANES, LANES), F32),
            pltpu.VMEM((2 * PEER_HEADS, PEER_TOPK, SUBLANES, LANES), I32),
            pltpu.VMEM((tile_rows(len(_CAND_PAIRS)), LANES), F32),
            pltpu.VMEM((tile_rows(len(_CAND_PAIRS)), LANES), I32),
            pltpu.VMEM((tile_rows(nsel), LANES), I32),
            pltpu.VMEM((tile_rows(nsel), LANES), F32),
        ],
        compiler_params=_params("parallel"),
        name="route",
    )(h2, wq, subkeys)


def _peer_kernel(x_ref, h_ref, e_ref, g_ref, mod_ref, fg_ref, u_ref, v_ref, o_ref, m_ref, p0_ref, p1_ref, acc_ref,
                 *, final, n_chunks):
    tb = x_ref.shape[0]
    nk = PEER_NKEYS
    c = pl.program_id(1)
    chunks_per_pass = n_chunks // PEER_PASSES

    @pl.when(jnp.logical_and(c % chunks_per_pass == 0, c < n_chunks))
    def _():
        a_iota = lax.broadcasted_iota(I32, (M_ROWS, nk), 0) + (c // chunks_per_pass) * M_ROWS
        b_iota = lax.broadcasted_iota(I32, (nk, nk), 0)
        zero = jnp.zeros((M_ROWS, nk), BF16)

        def group(gi, carry):
            t0 = pl.multiple_of(gi * M_GROUP, M_GROUP)
            er = e_ref[pl.ds(t0, M_GROUP), :]
            gr = g_ref[pl.ds(t0, M_GROUP), :]
            for pair in range(M_GROUP // 2):
                at, bt = [], []
                for tt in (2 * pair, 2 * pair + 1):
                    e1 = er[tt:tt + 1, :]
                    at.append(jnp.where(a_iota == (e1 >> 7), gr[tt:tt + 1, :], 0.0).astype(BF16))
                    bt.append(jnp.where(b_iota == (e1 & (nk - 1)), 1.0, 0.0).astype(BF16))
                lhs = jnp.concatenate([jnp.concatenate([at[0], zero], axis=1),
                                       jnp.concatenate([zero, at[1]], axis=1)], axis=0)
                out = _nt_dot(lhs, jnp.concatenate(bt, axis=1))
                r0 = pl.multiple_of((t0 + 2 * pair) * M_PITCH, SUBLANES)
                m_ref[pl.ds(r0, M_ROWS), :] = out[:M_ROWS, :]
                m_ref[pl.ds(r0 + M_PITCH, M_ROWS), :] = out[M_ROWS:, :]
            return carry

        lax.fori_loop(0, tb // M_GROUP, group, 0)

    def step(prev_ref, cur_ref):
        n_sub = PEER_CH // PEER_SUB
        nw = D_MODEL // n_sub
        per = PEER_SUB // nk
        a_base = (c % chunks_per_pass) * (PEER_CH // nk)
        for sub in range(n_sub):
            if cur_ref is not None:
                rows = slice(sub * PEER_SUB, (sub + 1) * PEER_SUB)
                s = _gelu(_nt_dot(h_ref[...], u_ref[rows, :]))
                m = jnp.concatenate(
                    [m_ref[pl.ds(a_base + sub * per + a, tb, stride=M_PITCH), :] for a in range(per)], axis=1)
                cur_ref[:, rows] = (m * s).astype(BF16)
            if prev_ref is not None:
                cols = slice(sub * nw, (sub + 1) * nw)
                acc_ref[:, cols] += jnp.dot(prev_ref[...], v_ref[:, cols], preferred_element_type=F32)

    stage = (p0_ref, p1_ref)
    steady = jnp.logical_and(c > 0, c < n_chunks)

    @pl.when(c == 0)
    def _():
        acc_ref[...] = jnp.zeros_like(acc_ref)
        step(None, stage[0])

    for parity in range(2):
        @pl.when(jnp.logical_and(steady, c % 2 == parity))
        def _(parity=parity):
            step(stage[1 - parity], stage[parity])

    @pl.when(c == n_chunks)
    def _():
        step(stage[(n_chunks - 1) % 2], None)
        gt2 = mod_ref[:, 5 * D_MODEL:6 * D_MODEL]
        xn = x_ref[...] + gt2 * acc_ref[...]
        o_ref[...] = _rms(xn, fg_ref[...]) if final else xn


def _peer(x, h2, e, g, mod4, row_fn, layer, final_g, u_tab, v_tab, final):
    n = x.shape[0]
    tb = PEER_TB
    nsel = PEER_HEADS * PEER_TOPK
    n_chunks = u_tab.shape[1] // PEER_CH
    return pl.pallas_call(
        functools.partial(_peer_kernel, final=final, n_chunks=n_chunks),
        grid=(n // tb, n_chunks + 1),
        in_specs=[
            pl.BlockSpec((tb, D_MODEL), lambda i, c: (i, 0)),
            pl.BlockSpec((tb, D_MODEL), lambda i, c: (i, 0)),
            pl.BlockSpec((tb, nsel), lambda i, c: (i, 0)),
            pl.BlockSpec((tb, nsel), lambda i, c: (i, 0)),
            pl.BlockSpec((None, None, 1, 6 * D_MODEL), lambda i, c: (layer, row_fn(i * tb), 0, 0)),
            pl.BlockSpec((1, D_MODEL), lambda i, c: (0, 0)),
            pl.BlockSpec((None, PEER_CH, D_MODEL), lambda i, c: (layer, jnp.minimum(c, n_chunks - 1), 0)),
            pl.BlockSpec((None, PEER_CH, D_MODEL), lambda i, c: (layer, jnp.maximum(c - 1, 0), 0)),
        ],
        out_specs=pl.BlockSpec((tb, D_MODEL), lambda i, c: (i, 0)),
        out_shape=jax.ShapeDtypeStruct((n, D_MODEL), F32),
        scratch_shapes=[pltpu.VMEM((tb * M_PITCH, LANES), F32),
                        pltpu.VMEM((tb, PEER_CH), BF16),
                        pltpu.VMEM((tb, PEER_CH), BF16),
                        pltpu.VMEM((tb, D_MODEL), F32)],
        compiler_params=_params("parallel", "arbitrary"),
        name="peer",
    )(x, h2, e, g, mod4, final_g, u_tab, v_tab)


def kernel(x_prompt, x_sample, state_ret, c, c_ctx, w_mod, b_mod, norm_g, w_in, w_out, gmlp_ws, gmlp_b,
           gmlp_ln_g, pool_w, pool_scale, ret_decay, peer_wq, peer_subkeys, peer_u, peer_v, final_norm_g):
    batch, seq, d = x_prompt.shape
    dec_batch, dec_seq, _ = x_sample.shape
    depth = w_mod.shape[0]
    assert d == D_MODEL and dec_batch + 1 <= MOD_ROWS and dec_seq % GRID_W == 0

    cond = jnp.zeros((MOD_ROWS, d), F32).at[0].set(c_ctx).at[1:1 + dec_batch].set(c)
    mod = _modulation(cond, w_mod, b_mod)
    mod4 = mod.reshape(depth, MOD_ROWS, 1, 6 * d)
    ctx_row = lambda tok0: 0
    lat_row = lambda tok0: 1 + tok0 // dec_seq

    xc = x_prompt.reshape(batch * seq, d)
    xs = x_sample.reshape(dec_batch * dec_seq, d)
    fg = final_norm_g.reshape(1, d)
    norm_g4 = norm_g.reshape(depth, 2, 1, d)
    bs_t = jnp.swapaxes(gmlp_b, 1, 2)
    ln_g = gmlp_ln_g.reshape(depth, 1, d)
    ps = pool_scale.reshape(depth, 1, d)
    sk = peer_subkeys.reshape(depth, 2 * PEER_HEADS, PEER_NKEYS, PEER_DKEY // 2)
    u_tab = peer_u.astype(BF16)
    v_tab = peer_v.astype(BF16)
    new_states = []
    for l in range(depth):
        last = l == depth - 1

        def path(x, nseq, n, row_fn, grid, s0):
            proj = _proj(x, mod4, row_fn, l, norm_g4, w_in)
            ya = _gmlp(proj, l, gmlp_ws, bs_t, ln_g)
            yb = _pool(proj, l, pool_w, ps, nseq, n, grid)
            ret = _retention(proj, ret_decay, nseq, n, s0, l)
            yc, s_fin = (ret, None) if s0 is not None else ret
            xn, h2 = _merge(x, proj, ya, yb, yc, mod4, row_fn, l, norm_g4, w_out)
            e, g = _route(h2, l, peer_wq, sk)
            return _peer(xn, h2, e, g, mod4, row_fn, l, fg, u_tab, v_tab, last), s_fin

        xc, s_fin = path(xc, batch, seq, ctx_row, False, None)
        new_states.append(s_fin)
        xs, _ = path(xs, dec_batch, dec_seq, lat_row, True, state_ret)

    new_state_ret = jnp.stack(new_states, axis=1).astype(x_prompt.dtype)
    return (xc.reshape(batch, seq, d), xs.reshape(dec_batch, dec_seq, d), new_state_ret)
```

```python
import functools

import numpy as np
import jax
import jax.numpy as jnp
from jax import lax
from jax.experimental import pallas as pl
from jax.experimental.pallas import tpu as pltpu

F32 = jnp.float32
BF16 = jnp.bfloat16
I32 = jnp.int32

D_MODEL = 1024
A_GROUPS = 4
A_CHUNK = 128
POOL_WINDOWS = (2, 4, 8, 16)
B_GW = D_MODEL // len(POOL_WINDOWS)
GRID_W = 64
RET_HEADS = 4
RET_DV = D_MODEL // RET_HEADS
RET_DK = RET_DV // 2
RET_CHUNK = 128
PEER_HEADS = 8
PEER_NKEYS = 128
PEER_DKEY = 256
PEER_TOPK = 16
EPS = 1e-6
IN_COLS = 10 * D_MODEL

SUBLANES = 8
LANES = 128
VMEM_LIMIT = 56 * 1024 * 1024

MOD_ROWS = 16
PROJ_TB = 512
PROJ_TN = 2048
PROJ_DTYPE = BF16
Y_DTYPE = BF16
GMLP_TB = 256
MERGE_TB = 512
ROUTE_TB = SUBLANES * LANES
ROUTE_STEPS = 8
ROUTE_MTOK = ROUTE_TB // ROUTE_STEPS
PEER_TB = 1024
PEER_CH = 1024
PEER_SUB = 256


def _gelu(x):
    return 0.5 * x * (1.0 + jnp.tanh(0.7978845608028654 * (x + 0.044715 * (x * x * x))))


def _sigmoid(x):
    return 1.0 / (1.0 + jnp.exp(-x))


def _rms(x, g):
    return x * lax.rsqrt(jnp.mean(x * x, axis=-1, keepdims=True) + EPS) * g


def _params(*sem):
    return pltpu.CompilerParams(dimension_semantics=sem, vmem_limit_bytes=VMEM_LIMIT)


def _tree(fn, xs):
    xs = list(xs)
    while len(xs) > 1:
        xs = [fn(xs[i], xs[i + 1]) for i in range(0, len(xs) - 1, 2)] + ([xs[-1]] if len(xs) % 2 else [])
    return xs[0]


def _nt_dot(a, b):
    return lax.dot_general(a, b, (((1,), (1,)), ((), ())), preferred_element_type=F32)


def _mod_kernel(cond_ref, w_ref, b_ref, o_ref):
    c = cond_ref[...]
    s = (c * _sigmoid(c)).astype(BF16)
    o_ref[...] = jnp.dot(s, w_ref[...].astype(BF16), preferred_element_type=F32) + b_ref[...]


def _modulation(cond, w_mod, b_mod):
    depth = w_mod.shape[0]
    nj = w_mod.shape[2] // D_MODEL
    return pl.pallas_call(
        _mod_kernel,
        grid=(depth, nj),
        in_specs=[
            pl.BlockSpec((MOD_ROWS, D_MODEL), lambda l, j: (0, 0)),
            pl.BlockSpec((None, D_MODEL, D_MODEL), lambda l, j: (l, 0, j)),
            pl.BlockSpec((None, 1, D_MODEL), lambda l, j: (l, 0, j)),
        ],
        out_specs=pl.BlockSpec((None, MOD_ROWS, D_MODEL), lambda l, j: (l, 0, j)),
        out_shape=jax.ShapeDtypeStruct((depth, MOD_ROWS, nj * D_MODEL), F32),
        compiler_params=_params("parallel", "parallel"),
        name="mod",
    )(cond, w_mod, b_mod.reshape(depth, 1, nj * D_MODEL))


def _proj_kernel(x_ref, mod_ref, g_ref, w_ref, o_ref, wb_ref):
    @pl.when(pl.program_id(1) == 0)
    def _():
        wb_ref[...] = w_ref[...].astype(BF16)

    y = _rms(x_ref[...], g_ref[...])
    sh = mod_ref[:, 0:D_MODEL]
    sc = mod_ref[:, D_MODEL:2 * D_MODEL]
    h = (y * (1.0 + sc) + sh).astype(BF16)
    o_ref[...] = jnp.dot(h, wb_ref[...], preferred_element_type=F32).astype(o_ref.dtype)


def _proj(x, mod4, row_fn, layer, norm_g, w_in):
    n = x.shape[0]
    tb, tn = PROJ_TB, PROJ_TN
    return pl.pallas_call(
        _proj_kernel,
        grid=(IN_COLS // tn, n // tb),
        in_specs=[
            pl.BlockSpec((tb, D_MODEL), lambda j, i: (i, 0)),
            pl.BlockSpec((None, None, 1, 6 * D_MODEL), lambda j, i: (layer, row_fn(i * tb), 0, 0)),
            pl.BlockSpec((None, None, 1, D_MODEL), lambda j, i: (layer, 0, 0, 0)),
            pl.BlockSpec((None, D_MODEL, tn), lambda j, i: (layer, 0, j)),
        ],
        out_specs=pl.BlockSpec((tb, tn), lambda j, i: (i, j)),
        out_shape=jax.ShapeDtypeStruct((n, IN_COLS), PROJ_DTYPE),
        scratch_shapes=[pltpu.VMEM((D_MODEL, tn), BF16)],
        compiler_params=_params("parallel", "arbitrary"),
        name="proj",
    )(x, mod4, norm_g, w_in)


def _gmlp_kernel(u_ref, v_ref, ws_ref, bs_ref, lng_ref, o_ref):
    gw = D_MODEL // A_GROUPS
    for c in range(GMLP_TB // A_CHUNK):
        rows = slice(c * A_CHUNK, (c + 1) * A_CHUNK)
        v = _gelu(v_ref[rows, :].astype(F32))
        vc = v - jnp.mean(v, axis=-1, keepdims=True)
        vn = vc * lax.rsqrt(jnp.mean(vc * vc, axis=-1, keepdims=True) + EPS) * lng_ref[...]
        vnb = vn.astype(BF16)
        for g in range(A_GROUPS):
            cols = slice(g * gw, (g + 1) * gw)
            sv = jnp.dot(ws_ref[g].astype(BF16), vnb[:, cols], preferred_element_type=F32) + bs_ref[:, g:g + 1]
            o_ref[rows, cols] = (_gelu(u_ref[rows, cols].astype(F32)) * sv).astype(o_ref.dtype)


def _gmlp(proj, layer, ws, bs_t, ln_g):
    n = proj.shape[0]
    tb = GMLP_TB
    return pl.pallas_call(
        _gmlp_kernel,
        grid=(n // tb,),
        in_specs=[
            pl.BlockSpec((tb, D_MODEL), lambda i: (i, 0)),
            pl.BlockSpec((tb, D_MODEL), lambda i: (i, 1)),
            pl.BlockSpec((None, A_GROUPS, A_CHUNK, A_CHUNK), lambda i: (layer, 0, 0, 0)),
            pl.BlockSpec((None, A_CHUNK, A_GROUPS), lambda i: (layer, 0, 0)),
            pl.BlockSpec((None, 1, D_MODEL), lambda i: (layer, 0, 0)),
        ],
        out_specs=pl.BlockSpec((tb, D_MODEL), lambda i: (i, 0)),
        out_shape=jax.ShapeDtypeStruct((n, D_MODEL), Y_DTYPE),
        compiler_params=_params("parallel"),
        name="gmlp",
    )(proj, proj, ws, bs_t, ln_g)


POOL_PAD = 16


def _window_count(pos, size, a, b):
    return jnp.minimum(pos + b, size - 1) - jnp.maximum(pos - a, 0) + 1


def _pool_kernel(z_ref, w_ref, scale_ref, o_ref, zp_ref, cp_ref, *, n, grid):
    rows = n // GRID_W
    rpad = cp_ref.shape[0] - n
    tok = lax.broadcasted_iota(I32, (n, 1), 0)
    for gi, w in enumerate(POOL_WINDOWS):
        a = w // 2
        b = w - 1 - a
        cols = slice(gi * B_GW, (gi + 1) * B_GW)
        z = z_ref[:, cols].astype(F32)
        zp_ref[0:POOL_PAD, :] = jnp.zeros((POOL_PAD, B_GW), F32)
        zp_ref[POOL_PAD + n:, :] = jnp.zeros((POOL_PAD, B_GW), F32)
        zp_ref[POOL_PAD:POOL_PAD + n, :] = z
        if grid:
            col = tok & (GRID_W - 1)
            row = tok >> (GRID_W.bit_length() - 1)
            s = jnp.zeros((n, B_GW), F32)
            for d in range(-a, b + 1):
                sh = zp_ref[POOL_PAD + d:POOL_PAD + d + n, :]
                ok = jnp.logical_and(col + d >= 0, col + d < GRID_W)
                s = s + jnp.where(ok, sh, 0.0)
            half = rpad // 2
            cp_ref[0:half, :] = jnp.zeros((half, B_GW), F32)
            cp_ref[half + n:, :] = jnp.zeros((half, B_GW), F32)
            cp_ref[half:half + n, :] = s
            s = jnp.zeros((n, B_GW), F32)
            for d in range(-a, b + 1):
                s = s + cp_ref[half + d * GRID_W:half + d * GRID_W + n, :]
            cnt = (_window_count(row, rows, a, b) * _window_count(col, GRID_W, a, b)).astype(F32)
        else:
            s = jnp.zeros((n, B_GW), F32)
            for d in range(-a, b + 1):
                s = s + zp_ref[POOL_PAD + d:POOL_PAD + d + n, :]
            cnt = _window_count(tok, n, a, b).astype(F32)
        diff = (s / cnt - z).astype(BF16)
        o_ref[:, cols] = (jnp.dot(diff, w_ref[gi].astype(BF16), preferred_element_type=F32)
                          * scale_ref[:, cols]).astype(o_ref.dtype)


def _pool(proj, layer, pool_w, pool_scale, nseq, n, grid):
    rpad = 2 * (max(POOL_WINDOWS) // 2) * GRID_W if grid else 2 * SUBLANES
    return pl.pallas_call(
        functools.partial(_pool_kernel, n=n, grid=grid),
        grid=(nseq,),
        in_specs=[
            pl.BlockSpec((n, D_MODEL), lambda i: (i, 2)),
            pl.BlockSpec((None, len(POOL_WINDOWS), B_GW, B_GW), lambda i: (layer, 0, 0, 0)),
            pl.BlockSpec((None, 1, D_MODEL), lambda i: (layer, 0, 0)),
        ],
        out_specs=pl.BlockSpec((n, D_MODEL), lambda i: (i, 0)),
        out_shape=jax.ShapeDtypeStruct((nseq * n, D_MODEL), Y_DTYPE),
        scratch_shapes=[pltpu.VMEM((n + 2 * POOL_PAD, B_GW), F32), pltpu.VMEM((n + rpad, B_GW), F32)],
        compiler_params=_params("parallel"),
        name="pool",
    )(proj, pool_w, pool_scale)


def _log_sigmoid(x):
    return jnp.minimum(x, 0.0) - jnp.log(1.0 + jnp.exp(-jnp.abs(x)))


def _ret_kernel(*refs, n, has_s0):
    if has_s0:
        q_ref, k_ref, v_ref, gf_ref, gb_ref, rd_ref, s0_ref, y_ref, s_ref, dm_ref, yf_ref = refs
        sfin_ref = None
    else:
        q_ref, k_ref, v_ref, gf_ref, gb_ref, rd_ref, y_ref, sfin_ref, s_ref, dm_ref, yf_ref = refs
        s0_ref = None
    cl = RET_CHUNK
    nc = n // cl
    pi = lax.broadcasted_iota(I32, (cl, cl), 0).astype(F32)
    pj = lax.broadcasted_iota(I32, (cl, cl), 1).astype(F32)
    pcol = lax.broadcasted_iota(I32, (cl, 1), 0).astype(F32)
    kscale = RET_DK ** -0.5

    for d in range(2):
        g_ref = gf_ref if d == 0 else gb_ref
        lgs = []
        for h in range(RET_HEADS):
            lg = _log_sigmoid(rd_ref[d:d + 1, h:h + 1])
            lgs.append(lg)
            rel = (pi - pj) if d == 0 else (pj - pi)
            dm_ref[h] = jnp.where(rel >= 0.0, jnp.exp(lg * jnp.maximum(rel, 0.0)), 0.0)
            if has_s0:
                s_ref[h] = s0_ref[d, h]
            else:
                s_ref[h] = jnp.zeros((RET_DK, RET_DV), F32)

        def chunk(ci, carry, d=d, g_ref=g_ref, lgs=lgs):
            c = ci if d == 0 else nc - 1 - ci
            rows = pl.ds(pl.multiple_of(c * cl, cl), cl)
            for h in range(RET_HEADS):
                lg = lgs[h]
                if d == 0:
                    qdec = jnp.exp(lg * (pcol + 1.0))
                    kdec = jnp.exp(lg * (cl - 1.0 - pcol))
                else:
                    qdec = jnp.exp(lg * (cl - pcol))
                    kdec = jnp.exp(lg * pcol)
                cdec = jnp.exp(lg * float(cl))
                q = q_ref[rows, h * RET_DK:(h + 1) * RET_DK].astype(F32)
                k = k_ref[rows, h * RET_DK:(h + 1) * RET_DK].astype(F32) * kscale
                vb = v_ref[rows, h * RET_DV:(h + 1) * RET_DV].astype(BF16)
                sc = _nt_dot(q.astype(BF16), k.astype(BF16)) * dm_ref[h]
                o = jnp.dot(sc.astype(BF16), vb, preferred_element_type=F32)
                s_prev = s_ref[h]
                o = o + jnp.dot((q * qdec).astype(BF16), s_prev.astype(BF16), preferred_element_type=F32)
                kd_t = jnp.transpose(k * kdec).astype(BF16)
                s_ref[h] = cdec * s_prev + jnp.dot(kd_t, vb, preferred_element_type=F32)
                on = o * lax.rsqrt(jnp.mean(o * o, axis=-1, keepdims=True) + EPS)
                g = g_ref[rows, h * RET_DV:(h + 1) * RET_DV].astype(F32)
                yv = g * _sigmoid(g) * on
                cols = slice(h * RET_DV, (h + 1) * RET_DV)
                if d == 0:
                    yf_ref[rows, cols] = yv
                else:
                    y_ref[rows, cols] = (yf_ref[rows, cols] + yv).astype(y_ref.dtype)
            return carry

        lax.fori_loop(0, nc, chunk, 0)
        if not has_s0:
            for h in range(RET_HEADS):
                sfin_ref[d, h] = s_ref[h]


def _retention(proj, ret_decay, nseq, n, s0, layer):
    has_s0 = s0 is not None
    dkb = RET_HEADS * RET_DK
    in_specs = [
        pl.BlockSpec((n, dkb), lambda i: (i, 3 * D_MODEL // dkb)),
        pl.BlockSpec((n, dkb), lambda i: (i, 3 * D_MODEL // dkb + 1)),
        pl.BlockSpec((n, D_MODEL), lambda i: (i, 4)),
        pl.BlockSpec((n, D_MODEL), lambda i: (i, 5)),
        pl.BlockSpec((n, D_MODEL), lambda i: (i, 6)),
        pl.BlockSpec((None, 2, RET_HEADS), lambda i: (layer, 0, 0)),
    ]
    args = [proj, proj, proj, proj, proj, ret_decay]
    y_shape = jax.ShapeDtypeStruct((nseq * n, D_MODEL), Y_DTYPE)
    y_spec = pl.BlockSpec((n, D_MODEL), lambda i: (i, 0))
    if has_s0:
        in_specs.append(pl.BlockSpec((None, None, 2, RET_HEADS, RET_DK, RET_DV), lambda i: (i, layer, 0, 0, 0, 0)))
        args.append(s0)
        out_shape, out_specs = y_shape, y_spec
    else:
        out_shape = (y_shape, jax.ShapeDtypeStruct((nseq, 2, RET_HEADS, RET_DK, RET_DV), F32))
        out_specs = (y_spec, pl.BlockSpec((None, 2, RET_HEADS, RET_DK, RET_DV), lambda i: (i, 0, 0, 0, 0)))
    return pl.pallas_call(
        functools.partial(_ret_kernel, n=n, has_s0=has_s0),
        grid=(nseq,),
        in_specs=in_specs,
        out_specs=out_specs,
        out_shape=out_shape,
        scratch_shapes=[pltpu.VMEM((RET_HEADS, RET_DK, RET_DV), F32),
                        pltpu.VMEM((RET_HEADS, RET_CHUNK, RET_CHUNK), F32),
                        pltpu.VMEM((n, D_MODEL), F32)],
        compiler_params=_params("parallel"),
        name="ret",
    )(*args)


def _merge_kernel(x_ref, ga_ref, gb_ref, gc_ref, ya_ref, yb_ref, yc_ref, mod_ref, g2_ref, w_ref, xo_ref, h2_ref):
    merged = (_sigmoid(ga_ref[...].astype(F32)) * ya_ref[...] + _sigmoid(gb_ref[...].astype(F32)) * yb_ref[...]
              + _sigmoid(gc_ref[...].astype(F32)) * yc_ref[...])
    o = jnp.dot(merged.astype(BF16), w_ref[...].astype(BF16), preferred_element_type=F32)
    gt1 = mod_ref[:, 2 * D_MODEL:3 * D_MODEL]
    sh2 = mod_ref[:, 3 * D_MODEL:4 * D_MODEL]
    sc2 = mod_ref[:, 4 * D_MODEL:5 * D_MODEL]
    xn = x_ref[...] + gt1 * o
    xo_ref[...] = xn
    h2_ref[...] = (_rms(xn, g2_ref[...]) * (1.0 + sc2) + sh2).astype(BF16)


def _merge(x, proj, ya, yb, yc, mod4, row_fn, layer, norm_g2, w_out):
    n = x.shape[0]
    tb = MERGE_TB
    tok = lambda c: pl.BlockSpec((tb, D_MODEL), lambda i: (i, c))
    return pl.pallas_call(
        _merge_kernel,
        grid=(n // tb,),
        in_specs=[
            tok(0), tok(7), tok(8), tok(9), tok(0), tok(0), tok(0),
            pl.BlockSpec((None, None, 1, 6 * D_MODEL), lambda i: (layer, row_fn(i * tb), 0, 0)),
            pl.BlockSpec((None, None, 1, D_MODEL), lambda i: (layer, 1, 0, 0)),
            pl.BlockSpec((None, D_MODEL, D_MODEL), lambda i: (layer, 0, 0)),
        ],
        out_specs=(tok(0), tok(0)),
        out_shape=(jax.ShapeDtypeStruct((n, D_MODEL), F32), jax.ShapeDtypeStruct((n, D_MODEL), BF16)),
        compiler_params=_params("parallel"),
        name="merge",
    )(x, proj, proj, proj, ya, yb, yc, mod4, norm_g2, w_out)


_CAND_PAIRS = tuple((j1, j2) for j1 in range(PEER_TOPK) for j2 in range(PEER_TOPK)
                    if (j1 + 1) * (j2 + 1) <= PEER_TOPK)
ROUTE_ACCS = 4


def _scan_max(n_items, load, knock, store):
    per = -(-n_items // ROUTE_ACCS)
    accs = []
    for a0 in range(0, n_items, per):
        bv = bt = None
        for i in range(a0, min(a0 + per, n_items)):
            v, t = load(i)
            v = knock(i, v, t)
            store(i, v)
            if bv is None:
                bv, bt = v, t
            else:
                gt = v > bv
                bv = jnp.where(gt, v, bv)
                bt = jnp.where(gt, t, bt)
        accs.append((bv, bt))
    bv, bt = accs[0]
    for v, t in accs[1:]:
        gt = v > bv
        bv = jnp.where(gt, v, bv)
        bt = jnp.where(gt, t, bt)
    return bv, bt


def _route_kernel(h_ref, wq_ref, sk_ref, m_ref, q_scr, x_scr, ts_scr, ti_scr, c_scr, k_scr, eo_scr, go_scr,
                  pe_scr, pg_scr, *, n_blocks):
    tb = h_ref.shape[0]
    nk = PEER_NKEYS
    kd = PEER_DKEY // 2
    nlt = tb // LANES
    assert nlt == SUBLANES
    neg = -jnp.inf
    tile = lambda i: pl.ds(i * SUBLANES, SUBLANES)
    blk = pl.program_id(0)
    step = pl.program_id(1)
    live = blk < n_blocks

    @pl.when(jnp.logical_and(blk == 0, step == 0))
    def _():
        pe_scr[...] = jnp.zeros_like(pe_scr)
        pg_scr[...] = jnp.zeros_like(pg_scr)

    @pl.when(jnp.logical_and(live, step == 0))
    def _():
        for hd in range(PEER_HEADS):
            cols = slice(hd * PEER_DKEY, (hd + 1) * PEER_DKEY)
            q = jnp.dot(h_ref[...], wq_ref[:, cols].astype(BF16), preferred_element_type=F32).astype(BF16)
            q_scr[2 * hd] = q[:, :kd]
            q_scr[2 * hd + 1] = q[:, kd:]

    def build_gates():
        t0 = step * ROUTE_MTOK
        key_iota = lax.broadcasted_iota(I32, (nk, nk), 0)
        zero = jnp.zeros((nk, nk), BF16)
        for grp in range(ROUTE_MTOK // SUBLANES):
            rows = pl.ds(pl.multiple_of(t0 + grp * SUBLANES, SUBLANES), SUBLANES)
            er = pe_scr[rows, :]
            gr = pg_scr[rows, :]
            for pair in range(SUBLANES // 2):
                at, bt = [], []
                for tt in (2 * pair, 2 * pair + 1):
                    e1 = er[tt:tt + 1, :]
                    at.append(jnp.where(key_iota == (e1 >> 7), gr[tt:tt + 1, :], 0.0).astype(BF16))
                    bt.append(jnp.where(key_iota == (e1 & (nk - 1)), 1.0, 0.0).astype(BF16))
                lhs = jnp.concatenate([jnp.concatenate([at[0], zero], axis=1),
                                       jnp.concatenate([zero, at[1]], axis=1)], axis=0)
                out = _nt_dot(lhs, jnp.concatenate(bt, axis=1))
                m_ref[grp * SUBLANES + 2 * pair] = out[:nk, :].astype(m_ref.dtype)
                m_ref[grp * SUBLANES + 2 * pair + 1] = out[nk:, :].astype(m_ref.dtype)

    def side(hs):
        sk = sk_ref[hs].astype(BF16)
        for j in range(nlt):
            sc = _nt_dot(sk, q_scr[hs, j * LANES:(j + 1) * LANES, :])
            x_scr[pl.ds(j, nk, stride=nlt), :] = sc
        prev = None
        for r in range(PEER_TOPK):
            load = lambda n: (x_scr[tile(n), :], jnp.full((SUBLANES, LANES), n, I32))
            if prev is None:
                knock = lambda n, v, t: v
                store = lambda n, v: None
            else:
                knock = lambda n, v, t, prev=prev: jnp.where(prev == n, neg, v)
                store = lambda n, v: x_scr.__setitem__((tile(n), slice(None)), v)
            bv, bi = _scan_max(nk, load, knock, store)
            ts_scr[hs, r] = bv
            ti_scr[hs, r] = bi
            prev = bi

    sides_per_step = 2 * PEER_HEADS // ROUTE_STEPS

    @pl.when(live)
    def _():
        for k in range(sides_per_step):
            side(step * sides_per_step + k)
        build_gates()

    @pl.when(blk == n_blocks)
    def _():
        build_gates()

    def head(h, carry):
        for p, (j1, j2) in enumerate(_CAND_PAIRS):
            c_scr[tile(p), :] = ts_scr[2 * h, j1] + ts_scr[2 * h + 1, j2]
            k_scr[tile(p), :] = ((j1 * PEER_TOPK + j2) * (nk * nk) + ti_scr[2 * h, j1] * nk
                                 + ti_scr[2 * h + 1, j2])
        prev = None
        vals, exps = [], []
        for r in range(PEER_TOPK):
            load = lambda p: (c_scr[tile(p), :], k_scr[tile(p), :])
            if prev is None:
                knock = lambda p, v, t: v
                store = lambda p, v: None
            else:
                knock = lambda p, v, t, prev=prev: jnp.where(t == prev, neg, v)
                store = lambda p, v: c_scr.__setitem__((tile(p), slice(None)), v)
            bv, bk = _scan_max(len(_CAND_PAIRS), load, knock, store)
            vals.append(bv)
            exps.append(bk & (nk * nk - 1))
            prev = bk
        ps = [jnp.exp(v - vals[0]) for v in vals]
        z = _tree(jnp.add, ps)
        for r in range(PEER_TOPK):
            row = pl.ds(pl.multiple_of((h * PEER_TOPK + r) * SUBLANES, SUBLANES), SUBLANES)
            go_scr[row, :] = ps[r] / z
            eo_scr[row, :] = exps[r]
        return carry

    @pl.when(jnp.logical_and(live, step == ROUTE_STEPS - 1))
    def _():
        lax.fori_loop(0, PEER_HEADS, head, 0)
        nsel = PEER_HEADS * PEER_TOPK
        for j in range(nlt):
            rows = slice(j * LANES, (j + 1) * LANES)
            pe_scr[rows, :] = jnp.transpose(eo_scr[pl.ds(j, nsel, stride=nlt), :])
            pg_scr[rows, :] = jnp.transpose(go_scr[pl.ds(j, nsel, stride=nlt), :])


def _route(h2, layer, wq, subkeys):
    n = h2.shape[0]
    tb = ROUTE_TB
    n_blocks = n // tb
    nsel = PEER_HEADS * PEER_TOPK
    kd = PEER_DKEY // 2
    nk = PEER_NKEYS
    tile_rows = lambda count: count * SUBLANES
    gates = pl.pallas_call(
        functools.partial(_route_kernel, n_blocks=n_blocks),
        grid=(n_blocks + 1, ROUTE_STEPS),
        in_specs=[
            pl.BlockSpec((tb, D_MODEL), lambda i, s: (jnp.minimum(i, n_blocks - 1), 0)),
            pl.BlockSpec((None, D_MODEL, PEER_HEADS * PEER_DKEY), lambda i, s: (layer, 0, 0)),
            pl.BlockSpec((None, 2 * PEER_HEADS, PEER_NKEYS, kd), lambda i, s: (layer, 0, 0, 0)),
        ],
        out_specs=pl.BlockSpec((ROUTE_MTOK, nk, nk),
                               lambda i, s: (jnp.maximum((i - 1) * ROUTE_STEPS + s, 0), 0, 0)),
        out_shape=jax.ShapeDtypeStruct((n, nk, nk), BF16),
        scratch_shapes=[
            pltpu.VMEM((2 * PEER_HEADS, tb, kd), BF16),
            pltpu.VMEM((tile_rows(PEER_NKEYS), LANES), F32),
            pltpu.VMEM((2 * PEER_HEADS, PEER_TOPK, SUBLANES, LANES), F32),
            pltpu.VMEM((2 * PEER_HEADS, PEER_TOPK, SUBLANES, LANES), I32),
            pltpu.VMEM((tile_rows(len(_CAND_PAIRS)), LANES), F32),
            pltpu.VMEM((tile_rows(len(_CAND_PAIRS)), LANES), I32),
            pltpu.VMEM((tile_rows(nsel), LANES), I32),
            pltpu.VMEM((tile_rows(nsel), LANES), F32),
            pltpu.VMEM((tb, nsel), I32),
            pltpu.VMEM((tb, nsel), F32),
        ],
        compiler_params=_params("arbitrary", "arbitrary"),
        name="route",
    )(h2, wq, subkeys)
    return gates.reshape(n, nk * nk)


def _peer_kernel(x_ref, h_ref, m_ref, mod_ref, fg_ref, u_ref, v_ref, o_ref, p0_ref, p1_ref, acc_ref,
                 *, final, n_chunks):
    c = pl.program_id(1)

    def step(prev_ref, cur_ref):
        n_sub = PEER_CH // PEER_SUB
        nw = D_MODEL // n_sub
        for sub in range(n_sub):
            if cur_ref is not None:
                rows = slice(sub * PEER_SUB, (sub + 1) * PEER_SUB)
                s = _gelu(_nt_dot(h_ref[...], u_ref[rows, :]))
                cur_ref[:, rows] = (m_ref[:, rows].astype(F32) * s).astype(BF16)
            if prev_ref is not None:
                cols = slice(sub * nw, (sub + 1) * nw)
                acc_ref[:, cols] += jnp.dot(prev_ref[...], v_ref[:, cols], preferred_element_type=F32)

    stage = (p0_ref, p1_ref)
    steady = jnp.logical_and(c > 0, c < n_chunks)

    @pl.when(c == 0)
    def _():
        acc_ref[...] = jnp.zeros_like(acc_ref)
        step(None, stage[0])

    for parity in range(2):
        @pl.when(jnp.logical_and(steady, c % 2 == parity))
        def _(parity=parity):
            step(stage[1 - parity], stage[parity])

    @pl.when(c == n_chunks)
    def _():
        step(stage[(n_chunks - 1) % 2], None)
        gt2 = mod_ref[:, 5 * D_MODEL:6 * D_MODEL]
        xn = x_ref[...] + gt2 * acc_ref[...]
        o_ref[...] = _rms(xn, fg_ref[...]) if final else xn


def _peer(x, h2, gates, mod4, row_fn, layer, final_g, u_tab, v_tab, final):
    n = x.shape[0]
    tb = PEER_TB
    n_chunks = u_tab.shape[1] // PEER_CH
    return pl.pallas_call(
        functools.partial(_peer_kernel, final=final, n_chunks=n_chunks),
        grid=(n // tb, n_chunks + 1),
        in_specs=[
            pl.BlockSpec((tb, D_MODEL), lambda i, c: (i, 0)),
            pl.BlockSpec((tb, D_MODEL), lambda i, c: (i, 0)),
            pl.BlockSpec((tb, PEER_CH), lambda i, c: (i, jnp.minimum(c, n_chunks - 1))),
            pl.BlockSpec((None, None, 1, 6 * D_MODEL), lambda i, c: (layer, row_fn(i * tb), 0, 0)),
            pl.BlockSpec((1, D_MODEL), lambda i, c: (0, 0)),
            pl.BlockSpec((None, PEER_CH, D_MODEL), lambda i, c: (layer, jnp.minimum(c, n_chunks - 1), 0)),
            pl.BlockSpec((None, PEER_CH, D_MODEL), lambda i, c: (layer, jnp.maximum(c - 1, 0), 0)),
        ],
        out_specs=pl.BlockSpec((tb, D_MODEL), lambda i, c: (i, 0)),
        out_shape=jax.ShapeDtypeStruct((n, D_MODEL), F32),
        scratch_shapes=[pltpu.VMEM((tb, PEER_CH), BF16),
                        pltpu.VMEM((tb, PEER_CH), BF16),
                        pltpu.VMEM((tb, D_MODEL), F32)],
        compiler_params=_params("parallel", "arbitrary"),
        name="peer",
    )(x, h2, gates, mod4, final_g, u_tab, v_tab)


def kernel(x_prompt, x_sample, state_ret, c, c_ctx, w_mod, b_mod, norm_g, w_in, w_out, gmlp_ws, gmlp_b,
           gmlp_ln_g, pool_w, pool_scale, ret_decay, peer_wq, peer_subkeys, peer_u, peer_v, final_norm_g):
    batch, seq, d = x_prompt.shape
    dec_batch, dec_seq, _ = x_sample.shape
    depth = w_mod.shape[0]
    assert d == D_MODEL and dec_batch + 1 <= MOD_ROWS and dec_seq % GRID_W == 0

    cond = jnp.zeros((MOD_ROWS, d), F32).at[0].set(c_ctx).at[1:1 + dec_batch].set(c)
    mod = _modulation(cond, w_mod, b_mod)
    mod4 = mod.reshape(depth, MOD_ROWS, 1, 6 * d)
    ctx_row = lambda tok0: 0
    lat_row = lambda tok0: 1 + tok0 // dec_seq

    xc = x_prompt.reshape(batch * seq, d)
    xs = x_sample.reshape(dec_batch * dec_seq, d)
    fg = final_norm_g.reshape(1, d)
    norm_g4 = norm_g.reshape(depth, 2, 1, d)
    bs_t = jnp.swapaxes(gmlp_b, 1, 2)
    ln_g = gmlp_ln_g.reshape(depth, 1, d)
    ps = pool_scale.reshape(depth, 1, d)
    sk = peer_subkeys.reshape(depth, 2 * PEER_HEADS, PEER_NKEYS, PEER_DKEY // 2)
    u_tab = peer_u.astype(BF16)
    v_tab = peer_v.astype(BF16)
    new_states = []
    for l in range(depth):
        last = l == depth - 1

        def path(x, nseq, n, row_fn, grid, s0):
            proj = _proj(x, mod4, row_fn, l, norm_g4, w_in)
            ya = _gmlp(proj, l, gmlp_ws, bs_t, ln_g)
            yb = _pool(proj, l, pool_w, ps, nseq, n, grid)
            ret = _retention(proj, ret_decay, nseq, n, s0, l)
            yc, s_fin = (ret, None) if s0 is not None else ret
            xn, h2 = _merge(x, proj, ya, yb, yc, mod4, row_fn, l, norm_g4, w_out)
            gates = _route(h2, l, peer_wq, sk)
            return _peer(xn, h2, gates, mod4, row_fn, l, fg, u_tab, v_tab, last), s_fin

        xc, s_fin = path(xc, batch, seq, ctx_row, False, None)
        new_states.append(s_fin)
        xs, _ = path(xs, dec_batch, dec_seq, lat_row, True, state_ret)

    new_state_ret = jnp.stack(new_states, axis=1).astype(x_prompt.dtype)
    return (xc.reshape(batch, seq, d), xs.reshape(dec_batch, dec_seq, d), new_state_ret)
```

```python
import functools

import numpy as np
import jax
import jax.numpy as jnp
from jax import lax
from jax.experimental import pallas as pl
from jax.experimental.pallas import tpu as pltpu

F32 = jnp.float32
BF16 = jnp.bfloat16
I32 = jnp.int32

D_MODEL = 1024
A_GROUPS = 4
A_CHUNK = 128
POOL_WINDOWS = (2, 4, 8, 16)
B_GW = D_MODEL // len(POOL_WINDOWS)
GRID_W = 64
RET_HEADS = 4
RET_DV = D_MODEL // RET_HEADS
RET_DK = RET_DV // 2
RET_CHUNK = 128
PEER_HEADS = 8
PEER_NKEYS = 128
PEER_DKEY = 256
PEER_TOPK = 16
EPS = 1e-6
IN_COLS = 10 * D_MODEL

SUBLANES = 8
LANES = 128
VMEM_LIMIT = 56 * 1024 * 1024

MOD_ROWS = 16
PROJ_TB = 512
PROJ_TN = 2048
PROJ_DTYPE = BF16
Y_DTYPE = BF16
GMLP_TB = 256
MERGE_TB = 512
ROUTE_TB = SUBLANES * LANES
ROUTE_STEPS = 8
ROUTE_MTOK = ROUTE_TB // ROUTE_STEPS
PEER_TB = 1024
PEER_CH = 1024
PEER_SUB = 256


def _gelu(x):
    return 0.5 * x * (1.0 + jnp.tanh(0.7978845608028654 * (x + 0.044715 * (x * x * x))))


def _sigmoid(x):
    return 1.0 / (1.0 + jnp.exp(-x))


def _rms(x, g):
    return x * lax.rsqrt(jnp.mean(x * x, axis=-1, keepdims=True) + EPS) * g


def _params(*sem):
    return pltpu.CompilerParams(dimension_semantics=sem, vmem_limit_bytes=VMEM_LIMIT)


def _tree(fn, xs):
    xs = list(xs)
    while len(xs) > 1:
        xs = [fn(xs[i], xs[i + 1]) for i in range(0, len(xs) - 1, 2)] + ([xs[-1]] if len(xs) % 2 else [])
    return xs[0]


def _nt_dot(a, b):
    return lax.dot_general(a, b, (((1,), (1,)), ((), ())), preferred_element_type=F32)


def _mod_kernel(cond_ref, w_ref, b_ref, o_ref):
    c = cond_ref[...]
    s = (c * _sigmoid(c)).astype(BF16)
    o_ref[...] = jnp.dot(s, w_ref[...].astype(BF16), preferred_element_type=F32) + b_ref[...]


def _modulation(cond, w_mod, b_mod):
    depth = w_mod.shape[0]
    nj = w_mod.shape[2] // D_MODEL
    return pl.pallas_call(
        _mod_kernel,
        grid=(depth, nj),
        in_specs=[
            pl.BlockSpec((MOD_ROWS, D_MODEL), lambda l, j: (0, 0)),
            pl.BlockSpec((None, D_MODEL, D_MODEL), lambda l, j: (l, 0, j)),
            pl.BlockSpec((None, 1, D_MODEL), lambda l, j: (l, 0, j)),
        ],
        out_specs=pl.BlockSpec((None, MOD_ROWS, D_MODEL), lambda l, j: (l, 0, j)),
        out_shape=jax.ShapeDtypeStruct((depth, MOD_ROWS, nj * D_MODEL), F32),
        compiler_params=_params("parallel", "parallel"),
        name="mod",
    )(cond, w_mod, b_mod.reshape(depth, 1, nj * D_MODEL))


def _proj_kernel(x_ref, mod_ref, g_ref, w_ref, o_ref, wb_ref):
    @pl.when(pl.program_id(1) == 0)
    def _():
        wb_ref[...] = w_ref[...].astype(BF16)

    y = _rms(x_ref[...], g_ref[...])
    sh = mod_ref[:, 0:D_MODEL]
    sc = mod_ref[:, D_MODEL:2 * D_MODEL]
    h = (y * (1.0 + sc) + sh).astype(BF16)
    o_ref[...] = jnp.dot(h, wb_ref[...], preferred_element_type=F32).astype(o_ref.dtype)


def _proj(x, mod4, row_fn, layer, norm_g, w_in):
    n = x.shape[0]
    tb, tn = PROJ_TB, PROJ_TN
    return pl.pallas_call(
        _proj_kernel,
        grid=(IN_COLS // tn, n // tb),
        in_specs=[
            pl.BlockSpec((tb, D_MODEL), lambda j, i: (i, 0)),
            pl.BlockSpec((None, None, 1, 6 * D_MODEL), lambda j, i: (layer, row_fn(i * tb), 0, 0)),
            pl.BlockSpec((None, None, 1, D_MODEL), lambda j, i: (layer, 0, 0, 0)),
            pl.BlockSpec((None, D_MODEL, tn), lambda j, i: (layer, 0, j)),
        ],
        out_specs=pl.BlockSpec((tb, tn), lambda j, i: (i, j)),
        out_shape=jax.ShapeDtypeStruct((n, IN_COLS), PROJ_DTYPE),
        scratch_shapes=[pltpu.VMEM((D_MODEL, tn), BF16)],
        compiler_params=_params("parallel", "arbitrary"),
        name="proj",
    )(x, mod4, norm_g, w_in)


def _gmlp_kernel(u_ref, v_ref, ws_ref, bs_ref, lng_ref, o_ref):
    gw = D_MODEL // A_GROUPS
    for c in range(GMLP_TB // A_CHUNK):
        rows = slice(c * A_CHUNK, (c + 1) * A_CHUNK)
        v = _gelu(v_ref[rows, :].astype(F32))
        vc = v - jnp.mean(v, axis=-1, keepdims=True)
        vn = vc * lax.rsqrt(jnp.mean(vc * vc, axis=-1, keepdims=True) + EPS) * lng_ref[...]
        vnb = vn.astype(BF16)
        for g in range(A_GROUPS):
            cols = slice(g * gw, (g + 1) * gw)
            sv = jnp.dot(ws_ref[g].astype(BF16), vnb[:, cols], preferred_element_type=F32) + bs_ref[:, g:g + 1]
            o_ref[rows, cols] = (_gelu(u_ref[rows, cols].astype(F32)) * sv).astype(o_ref.dtype)


def _gmlp(proj, layer, ws, bs_t, ln_g):
    n = proj.shape[0]
    tb = GMLP_TB
    return pl.pallas_call(
        _gmlp_kernel,
        grid=(n // tb,),
        in_specs=[
            pl.BlockSpec((tb, D_MODEL), lambda i: (i, 0)),
            pl.BlockSpec((tb, D_MODEL), lambda i: (i, 1)),
            pl.BlockSpec((None, A_GROUPS, A_CHUNK, A_CHUNK), lambda i: (layer, 0, 0, 0)),
            pl.BlockSpec((None, A_CHUNK, A_GROUPS), lambda i: (layer, 0, 0)),
            pl.BlockSpec((None, 1, D_MODEL), lambda i: (layer, 0, 0)),
        ],
        out_specs=pl.BlockSpec((tb, D_MODEL), lambda i: (i, 0)),
        out_shape=jax.ShapeDtypeStruct((n, D_MODEL), Y_DTYPE),
        compiler_params=_params("parallel"),
        name="gmlp",
    )(proj, proj, ws, bs_t, ln_g)


POOL_PAD = 16


def _window_count(pos, size, a, b):
    return jnp.minimum(pos + b, size - 1) - jnp.maximum(pos - a, 0) + 1


def _pool_kernel(z_ref, w_ref, scale_ref, o_ref, zp_ref, cp_ref, *, n, grid):
    rows = n // GRID_W
    rpad = cp_ref.shape[0] - n
    tok = lax.broadcasted_iota(I32, (n, 1), 0)
    for gi, w in enumerate(POOL_WINDOWS):
        a = w // 2
        b = w - 1 - a
        cols = slice(gi * B_GW, (gi + 1) * B_GW)
        z = z_ref[:, cols].astype(F32)
        zp_ref[0:POOL_PAD, :] = jnp.zeros((POOL_PAD, B_GW), F32)
        zp_ref[POOL_PAD + n:, :] = jnp.zeros((POOL_PAD, B_GW), F32)
        zp_ref[POOL_PAD:POOL_PAD + n, :] = z
        if grid:
            col = tok & (GRID_W - 1)
            row = tok >> (GRID_W.bit_length() - 1)
            s = jnp.zeros((n, B_GW), F32)
            for d in range(-a, b + 1):
                sh = zp_ref[POOL_PAD + d:POOL_PAD + d + n, :]
                ok = jnp.logical_and(col + d >= 0, col + d < GRID_W)
                s = s + jnp.where(ok, sh, 0.0)
            half = rpad // 2
            cp_ref[0:half, :] = jnp.zeros((half, B_GW), F32)
            cp_ref[half + n:, :] = jnp.zeros((half, B_GW), F32)
            cp_ref[half:half + n, :] = s
            s = jnp.zeros((n, B_GW), F32)
            for d in range(-a, b + 1):
                s = s + cp_ref[half + d * GRID_W:half + d * GRID_W + n, :]
            cnt = (_window_count(row, rows, a, b) * _window_count(col, GRID_W, a, b)).astype(F32)
        else:
            s = jnp.zeros((n, B_GW), F32)
            for d in range(-a, b + 1):
                s = s + zp_ref[POOL_PAD + d:POOL_PAD + d + n, :]
            cnt = _window_count(tok, n, a, b).astype(F32)
        diff = (s / cnt - z).astype(BF16)
        o_ref[:, cols] = (jnp.dot(diff, w_ref[gi].astype(BF16), preferred_element_type=F32)
                          * scale_ref[:, cols]).astype(o_ref.dtype)


def _pool(proj, layer, pool_w, pool_scale, nseq, n, grid):
    rpad = 2 * (max(POOL_WINDOWS) // 2) * GRID_W if grid else 2 * SUBLANES
    return pl.pallas_call(
        functools.partial(_pool_kernel, n=n, grid=grid),
        grid=(nseq,),
        in_specs=[
            pl.BlockSpec((n, D_MODEL), lambda i: (i, 2)),
            pl.BlockSpec((None, len(POOL_WINDOWS), B_GW, B_GW), lambda i: (layer, 0, 0, 0)),
            pl.BlockSpec((None, 1, D_MODEL), lambda i: (layer, 0, 0)),
        ],
        out_specs=pl.BlockSpec((n, D_MODEL), lambda i: (i, 0)),
        out_shape=jax.ShapeDtypeStruct((nseq * n, D_MODEL), Y_DTYPE),
        scratch_shapes=[pltpu.VMEM((n + 2 * POOL_PAD, B_GW), F32), pltpu.VMEM((n + rpad, B_GW), F32)],
        compiler_params=_params("parallel"),
        name="pool",
    )(proj, pool_w, pool_scale)


def _log_sigmoid(x):
    return jnp.minimum(x, 0.0) - jnp.log(1.0 + jnp.exp(-jnp.abs(x)))


def _ret_kernel(*refs, n, has_s0):
    if has_s0:
        q_ref, k_ref, v_ref, gf_ref, gb_ref, rd_ref, s0_ref, y_ref, s_ref, dm_ref, yf_ref = refs
        sfin_ref = None
    else:
        q_ref, k_ref, v_ref, gf_ref, gb_ref, rd_ref, y_ref, sfin_ref, s_ref, dm_ref, yf_ref = refs
        s0_ref = None
    cl = RET_CHUNK
    nc = n // cl
    pi = lax.broadcasted_iota(I32, (cl, cl), 0).astype(F32)
    pj = lax.broadcasted_iota(I32, (cl, cl), 1).astype(F32)
    pcol = lax.broadcasted_iota(I32, (cl, 1), 0).astype(F32)
    kscale = RET_DK ** -0.5

    for d in range(2):
        g_ref = gf_ref if d == 0 else gb_ref
        lgs = []
        for h in range(RET_HEADS):
            lg = _log_sigmoid(rd_ref[d:d + 1, h:h + 1])
            lgs.append(lg)
            rel = (pi - pj) if d == 0 else (pj - pi)
            dm_ref[h] = jnp.where(rel >= 0.0, jnp.exp(lg * jnp.maximum(rel, 0.0)), 0.0)
            if has_s0:
                s_ref[h] = s0_ref[d, h]
            else:
                s_ref[h] = jnp.zeros((RET_DK, RET_DV), F32)

        def chunk(ci, carry, d=d, g_ref=g_ref, lgs=lgs):
            c = ci if d == 0 else nc - 1 - ci
            rows = pl.ds(pl.multiple_of(c * cl, cl), cl)
            for h in range(RET_HEADS):
                lg = lgs[h]
                if d == 0:
                    qdec = jnp.exp(lg * (pcol + 1.0))
                    kdec = jnp.exp(lg * (cl - 1.0 - pcol))
                else:
                    qdec = jnp.exp(lg * (cl - pcol))
                    kdec = jnp.exp(lg * pcol)
                cdec = jnp.exp(lg * float(cl))
                q = q_ref[rows, h * RET_DK:(h + 1) * RET_DK].astype(F32)
                k = k_ref[rows, h * RET_DK:(h + 1) * RET_DK].astype(F32) * kscale
                vb = v_ref[rows, h * RET_DV:(h + 1) * RET_DV].astype(BF16)
                sc = _nt_dot(q.astype(BF16), k.astype(BF16)) * dm_ref[h]
                o = jnp.dot(sc.astype(BF16), vb, preferred_element_type=F32)
                s_prev = s_ref[h]
                o = o + jnp.dot((q * qdec).astype(BF16), s_prev.astype(BF16), preferred_element_type=F32)
                kd_t = jnp.transpose(k * kdec).astype(BF16)
                s_ref[h] = cdec * s_prev + jnp.dot(kd_t, vb, preferred_element_type=F32)
                on = o * lax.rsqrt(jnp.mean(o * o, axis=-1, keepdims=True) + EPS)
                g = g_ref[rows, h * RET_DV:(h + 1) * RET_DV].astype(F32)
                yv = g * _sigmoid(g) * on
                cols = slice(h * RET_DV, (h + 1) * RET_DV)
                if d == 0:
                    yf_ref[rows, cols] = yv
                else:
                    y_ref[rows, cols] = (yf_ref[rows, cols] + yv).astype(y_ref.dtype)
            return carry

        lax.fori_loop(0, nc, chunk, 0)
        if not has_s0:
            for h in range(RET_HEADS):
                sfin_ref[d, h] = s_ref[h]


def _retention(proj, ret_decay, nseq, n, s0, layer):
    has_s0 = s0 is not None
    dkb = RET_HEADS * RET_DK
    in_specs = [
        pl.BlockSpec((n, dkb), lambda i: (i, 3 * D_MODEL // dkb)),
        pl.BlockSpec((n, dkb), lambda i: (i, 3 * D_MODEL // dkb + 1)),
        pl.BlockSpec((n, D_MODEL), lambda i: (i, 4)),
        pl.BlockSpec((n, D_MODEL), lambda i: (i, 5)),
        pl.BlockSpec((n, D_MODEL), lambda i: (i, 6)),
        pl.BlockSpec((None, 2, RET_HEADS), lambda i: (layer, 0, 0)),
    ]
    args = [proj, proj, proj, proj, proj, ret_decay]
    y_shape = jax.ShapeDtypeStruct((nseq * n, D_MODEL), Y_DTYPE)
    y_spec = pl.BlockSpec((n, D_MODEL), lambda i: (i, 0))
    if has_s0:
        in_specs.append(pl.BlockSpec((None, None, 2, RET_HEADS, RET_DK, RET_DV), lambda i: (i, layer, 0, 0, 0, 0)))
        args.append(s0)
        out_shape, out_specs = y_shape, y_spec
    else:
        out_shape = (y_shape, jax.ShapeDtypeStruct((nseq, 2, RET_HEADS, RET_DK, RET_DV), F32))
        out_specs = (y_spec, pl.BlockSpec((None, 2, RET_HEADS, RET_DK, RET_DV), lambda i: (i, 0, 0, 0, 0)))
    return pl.pallas_call(
        functools.partial(_ret_kernel, n=n, has_s0=has_s0),
        grid=(nseq,),
        in_specs=in_specs,
        out_specs=out_specs,
        out_shape=out_shape,
        scratch_shapes=[pltpu.VMEM((RET_HEADS, RET_DK, RET_DV), F32),
                        pltpu.VMEM((RET_HEADS, RET_CHUNK, RET_CHUNK), F32),
                        pltpu.VMEM((n, D_MODEL), F32)],
        compiler_params=_params("parallel"),
        name="ret",
    )(*args)


def _merge_kernel(x_ref, ga_ref, gb_ref, gc_ref, ya_ref, yb_ref, yc_ref, mod_ref, g2_ref, w_ref, xo_ref, h2_ref):
    merged = (_sigmoid(ga_ref[...].astype(F32)) * ya_ref[...] + _sigmoid(gb_ref[...].astype(F32)) * yb_ref[...]
              + _sigmoid(gc_ref[...].astype(F32)) * yc_ref[...])
    o = jnp.dot(merged.astype(BF16), w_ref[...].astype(BF16), preferred_element_type=F32)
    gt1 = mod_ref[:, 2 * D_MODEL:3 * D_MODEL]
    sh2 = mod_ref[:, 3 * D_MODEL:4 * D_MODEL]
    sc2 = mod_ref[:, 4 * D_MODEL:5 * D_MODEL]
    xn = x_ref[...] + gt1 * o
    xo_ref[...] = xn
    h2_ref[...] = (_rms(xn, g2_ref[...]) * (1.0 + sc2) + sh2).astype(BF16)


def _merge(x, proj, ya, yb, yc, mod4, row_fn, layer, norm_g2, w_out):
    n = x.shape[0]
    tb = MERGE_TB
    tok = lambda c: pl.BlockSpec((tb, D_MODEL), lambda i: (i, c))
    return pl.pallas_call(
        _merge_kernel,
        grid=(n // tb,),
        in_specs=[
            tok(0), tok(7), tok(8), tok(9), tok(0), tok(0), tok(0),
            pl.BlockSpec((None, None, 1, 6 * D_MODEL), lambda i: (layer, row_fn(i * tb), 0, 0)),
            pl.BlockSpec((None, None, 1, D_MODEL), lambda i: (layer, 1, 0, 0)),
            pl.BlockSpec((None, D_MODEL, D_MODEL), lambda i: (layer, 0, 0)),
        ],
        out_specs=(tok(0), tok(0)),
        out_shape=(jax.ShapeDtypeStruct((n, D_MODEL), F32), jax.ShapeDtypeStruct((n, D_MODEL), BF16)),
        compiler_params=_params("parallel"),
        name="merge",
    )(x, proj, proj, proj, ya, yb, yc, mod4, norm_g2, w_out)


_CAND_PAIRS = tuple((j1, j2) for j1 in range(PEER_TOPK) for j2 in range(PEER_TOPK)
                    if (j1 + 1) * (j2 + 1) <= PEER_TOPK)
ROUTE_ACCS = 4


def _scan_max(n_items, load, knock, store):
    per = -(-n_items // ROUTE_ACCS)
    accs = []
    for a0 in range(0, n_items, per):
        bv = bt = None
        for i in range(a0, min(a0 + per, n_items)):
            v, t = load(i)
            v = knock(i, v, t)
            store(i, v)
            if bv is None:
                bv, bt = v, t
            else:
                gt = v > bv
                bv = jnp.where(gt, v, bv)
                bt = jnp.where(gt, t, bt)
        accs.append((bv, bt))
    bv, bt = accs[0]
    for v, t in accs[1:]:
        gt = v > bv
        bv = jnp.where(gt, v, bv)
        bt = jnp.where(gt, t, bt)
    return bv, bt


def _route_kernel(h_ref, wq_ref, sk_ref, m_ref, q_scr, x_scr, ts_scr, ti_scr, c_scr, k_scr, eo_scr, go_scr,
                  pe_scr, pg_scr, *, n_blocks):
    tb = h_ref.shape[0]
    nk = PEER_NKEYS
    kd = PEER_DKEY // 2
    nlt = tb // LANES
    assert nlt == SUBLANES
    neg = -jnp.inf
    tile = lambda i: pl.ds(i * SUBLANES, SUBLANES)
    blk = pl.program_id(0)
    step = pl.program_id(1)
    live = blk < n_blocks

    @pl.when(jnp.logical_and(blk == 0, step == 0))
    def _():
        pe_scr[...] = jnp.zeros_like(pe_scr)
        pg_scr[...] = jnp.zeros_like(pg_scr)

    @pl.when(jnp.logical_and(live, step == 0))
    def _():
        for hd in range(PEER_HEADS):
            cols = slice(hd * PEER_DKEY, (hd + 1) * PEER_DKEY)
            q = jnp.dot(h_ref[...], wq_ref[:, cols].astype(BF16), preferred_element_type=F32).astype(BF16)
            q_scr[2 * hd] = q[:, :kd]
            q_scr[2 * hd + 1] = q[:, kd:]

    def build_gates():
        t0 = step * ROUTE_MTOK
        key_iota = lax.broadcasted_iota(I32, (nk, nk), 0)
        zero = jnp.zeros((nk, nk), BF16)
        for grp in range(ROUTE_MTOK // SUBLANES):
            rows = pl.ds(pl.multiple_of(t0 + grp * SUBLANES, SUBLANES), SUBLANES)
            er = pe_scr[rows, :]
            gr = pg_scr[rows, :]
            mats = []
            for pair in range(SUBLANES // 2):
                at, bt = [], []
                for tt in (2 * pair, 2 * pair + 1):
                    e1 = er[tt:tt + 1, :]
                    at.append(jnp.where(key_iota == (e1 >> 7), gr[tt:tt + 1, :], 0.0).astype(BF16))
                    bt.append(jnp.where(key_iota == (e1 & (nk - 1)), 1.0, 0.0).astype(BF16))
                lhs = jnp.concatenate([jnp.concatenate([at[0], zero], axis=1),
                                       jnp.concatenate([zero, at[1]], axis=1)], axis=0)
                out = _nt_dot(lhs, jnp.concatenate(bt, axis=1))
                mats += [out[:nk, :], out[nk:, :]]
            m_ref[:, grp * SUBLANES:(grp + 1) * SUBLANES, :] = jnp.swapaxes(jnp.stack(mats, axis=0), 0, 1)

    def side(hs):
        sk = sk_ref[hs].astype(BF16)
        for j in range(nlt):
            sc = _nt_dot(sk, q_scr[hs, j * LANES:(j + 1) * LANES, :])
            x_scr[pl.ds(j, nk, stride=nlt), :] = sc
        prev = None
        for r in range(PEER_TOPK):
            load = lambda n: (x_scr[tile(n), :], jnp.full((SUBLANES, LANES), n, I32))
            if prev is None:
                knock = lambda n, v, t: v
                store = lambda n, v: None
            else:
                knock = lambda n, v, t, prev=prev: jnp.where(prev == n, neg, v)
                store = lambda n, v: x_scr.__setitem__((tile(n), slice(None)), v)
            bv, bi = _scan_max(nk, load, knock, store)
            ts_scr[hs, r] = bv
            ti_scr[hs, r] = bi
            prev = bi

    sides_per_step = 2 * PEER_HEADS // ROUTE_STEPS

    @pl.when(live)
    def _():
        for k in range(sides_per_step):
            side(step * sides_per_step + k)
        build_gates()

    @pl.when(blk == n_blocks)
    def _():
        build_gates()

    def head(h, carry):
        for p, (j1, j2) in enumerate(_CAND_PAIRS):
            c_scr[tile(p), :] = ts_scr[2 * h, j1] + ts_scr[2 * h + 1, j2]
            k_scr[tile(p), :] = ((j1 * PEER_TOPK + j2) * (nk * nk) + ti_scr[2 * h, j1] * nk
                                 + ti_scr[2 * h + 1, j2])
        prev = None
        vals, exps = [], []
        for r in range(PEER_TOPK):
            load = lambda p: (c_scr[tile(p), :], k_scr[tile(p), :])
            if prev is None:
                knock = lambda p, v, t: v
                store = lambda p, v: None
            else:
                knock = lambda p, v, t, prev=prev: jnp.where(t == prev, neg, v)
                store = lambda p, v: c_scr.__setitem__((tile(p), slice(None)), v)
            bv, bk = _scan_max(len(_CAND_PAIRS), load, knock, store)
            vals.append(bv)
            exps.append(bk & (nk * nk - 1))
            prev = bk
        ps = [jnp.exp(v - vals[0]) for v in vals]
        z = _tree(jnp.add, ps)
        for r in range(PEER_TOPK):
            row = pl.ds(pl.multiple_of((h * PEER_TOPK + r) * SUBLANES, SUBLANES), SUBLANES)
            go_scr[row, :] = ps[r] / z
            eo_scr[row, :] = exps[r]
        return carry

    @pl.when(jnp.logical_and(live, step == ROUTE_STEPS - 1))
    def _():
        lax.fori_loop(0, PEER_HEADS, head, 0)
        nsel = PEER_HEADS * PEER_TOPK
        for j in range(nlt):
            rows = slice(j * LANES, (j + 1) * LANES)
            pe_scr[rows, :] = jnp.transpose(eo_scr[pl.ds(j, nsel, stride=nlt), :])
            pg_scr[rows, :] = jnp.transpose(go_scr[pl.ds(j, nsel, stride=nlt), :])


def _route(h2, layer, wq, subkeys):
    n = h2.shape[0]
    tb = ROUTE_TB
    n_blocks = n // tb
    nsel = PEER_HEADS * PEER_TOPK
    kd = PEER_DKEY // 2
    nk = PEER_NKEYS
    tile_rows = lambda count: count * SUBLANES
    gates = pl.pallas_call(
        functools.partial(_route_kernel, n_blocks=n_blocks),
        grid=(n_blocks + 1, ROUTE_STEPS),
        in_specs=[
            pl.BlockSpec((tb, D_MODEL), lambda i, s: (jnp.minimum(i, n_blocks - 1), 0)),
            pl.BlockSpec((None, D_MODEL, PEER_HEADS * PEER_DKEY), lambda i, s: (layer, 0, 0)),
            pl.BlockSpec((None, 2 * PEER_HEADS, PEER_NKEYS, kd), lambda i, s: (layer, 0, 0, 0)),
        ],
        out_specs=pl.BlockSpec((nk, ROUTE_MTOK, nk),
                               lambda i, s: (0, jnp.maximum((i - 1) * ROUTE_STEPS + s, 0), 0)),
        out_shape=jax.ShapeDtypeStruct((nk, n, nk), F32),
        scratch_shapes=[
            pltpu.VMEM((2 * PEER_HEADS, tb, kd), BF16),
            pltpu.VMEM((tile_rows(PEER_NKEYS), LANES), F32),
            pltpu.VMEM((2 * PEER_HEADS, PEER_TOPK, SUBLANES, LANES), F32),
            pltpu.VMEM((2 * PEER_HEADS, PEER_TOPK, SUBLANES, LANES), I32),
            pltpu.VMEM((tile_rows(len(_CAND_PAIRS)), LANES), F32),
            pltpu.VMEM((tile_rows(len(_CAND_PAIRS)), LANES), I32),
            pltpu.VMEM((tile_rows(nsel), LANES), I32),
            pltpu.VMEM((tile_rows(nsel), LANES), F32),
            pltpu.VMEM((tb, nsel), I32),
            pltpu.VMEM((tb, nsel), F32),
        ],
        compiler_params=_params("arbitrary", "arbitrary"),
        name="route",
    )(h2, wq, subkeys)
    return gates


def _peer_kernel(x_ref, h_ref, m_ref, mod_ref, fg_ref, u_ref, v_ref, o_ref, p0_ref, p1_ref, acc_ref,
                 *, final, n_chunks):
    c = pl.program_id(1)

    def step(prev_ref, cur_ref):
        n_sub = PEER_CH // PEER_SUB
        nw = D_MODEL // n_sub
        for sub in range(n_sub):
            if cur_ref is not None:
                rows = slice(sub * PEER_SUB, (sub + 1) * PEER_SUB)
                s = _gelu(_nt_dot(h_ref[...], u_ref[rows, :]))
                per = PEER_SUB // PEER_NKEYS
                m = jnp.concatenate([m_ref[sub * per + a] for a in range(per)], axis=1)
                cur_ref[:, rows] = (m * s).astype(BF16)
            if prev_ref is not None:
                cols = slice(sub * nw, (sub + 1) * nw)
                acc_ref[:, cols] += jnp.dot(prev_ref[...], v_ref[:, cols], preferred_element_type=F32)

    stage = (p0_ref, p1_ref)
    steady = jnp.logical_and(c > 0, c < n_chunks)

    @pl.when(c == 0)
    def _():
        acc_ref[...] = jnp.zeros_like(acc_ref)
        step(None, stage[0])

    for parity in range(2):
        @pl.when(jnp.logical_and(steady, c % 2 == parity))
        def _(parity=parity):
            step(stage[1 - parity], stage[parity])

    @pl.when(c == n_chunks)
    def _():
        step(stage[(n_chunks - 1) % 2], None)
        gt2 = mod_ref[:, 5 * D_MODEL:6 * D_MODEL]
        xn = x_ref[...] + gt2 * acc_ref[...]
        o_ref[...] = _rms(xn, fg_ref[...]) if final else xn


def _peer(x, h2, gates, mod4, row_fn, layer, final_g, u_tab, v_tab, final):
    n = x.shape[0]
    tb = PEER_TB
    n_chunks = u_tab.shape[1] // PEER_CH
    return pl.pallas_call(
        functools.partial(_peer_kernel, final=final, n_chunks=n_chunks),
        grid=(n // tb, n_chunks + 1),
        in_specs=[
            pl.BlockSpec((tb, D_MODEL), lambda i, c: (i, 0)),
            pl.BlockSpec((tb, D_MODEL), lambda i, c: (i, 0)),
            pl.BlockSpec((PEER_CH // PEER_NKEYS, tb, PEER_NKEYS), lambda i, c: (jnp.minimum(c, n_chunks - 1), i, 0)),
            pl.BlockSpec((None, None, 1, 6 * D_MODEL), lambda i, c: (layer, row_fn(i * tb), 0, 0)),
            pl.BlockSpec((1, D_MODEL), lambda i, c: (0, 0)),
            pl.BlockSpec((None, PEER_CH, D_MODEL), lambda i, c: (layer, jnp.minimum(c, n_chunks - 1), 0)),
            pl.BlockSpec((None, PEER_CH, D_MODEL), lambda i, c: (layer, jnp.maximum(c - 1, 0), 0)),
        ],
        out_specs=pl.BlockSpec((tb, D_MODEL), lambda i, c: (i, 0)),
        out_shape=jax.ShapeDtypeStruct((n, D_MODEL), F32),
        scratch_shapes=[pltpu.VMEM((tb, PEER_CH), BF16),
                        pltpu.VMEM((tb, PEER_CH), BF16),
                        pltpu.VMEM((tb, D_MODEL), F32)],
        compiler_params=_params("parallel", "arbitrary"),
        name="peer",
    )(x, h2, gates, mod4, final_g, u_tab, v_tab)


def kernel(x_prompt, x_sample, state_ret, c, c_ctx, w_mod, b_mod, norm_g, w_in, w_out, gmlp_ws, gmlp_b,
           gmlp_ln_g, pool_w, pool_scale, ret_decay, peer_wq, peer_subkeys, peer_u, peer_v, final_norm_g):
    batch, seq, d = x_prompt.shape
    dec_batch, dec_seq, _ = x_sample.shape
    depth = w_mod.shape[0]
    assert d == D_MODEL and dec_batch + 1 <= MOD_ROWS and dec_seq % GRID_W == 0

    cond = jnp.zeros((MOD_ROWS, d), F32).at[0].set(c_ctx).at[1:1 + dec_batch].set(c)
    mod = _modulation(cond, w_mod, b_mod)
    mod4 = mod.reshape(depth, MOD_ROWS, 1, 6 * d)
    ctx_row = lambda tok0: 0
    lat_row = lambda tok0: 1 + tok0 // dec_seq

    xc = x_prompt.reshape(batch * seq, d)
    xs = x_sample.reshape(dec_batch * dec_seq, d)
    fg = final_norm_g.reshape(1, d)
    norm_g4 = norm_g.reshape(depth, 2, 1, d)
    bs_t = jnp.swapaxes(gmlp_b, 1, 2)
    ln_g = gmlp_ln_g.reshape(depth, 1, d)
    ps = pool_scale.reshape(depth, 1, d)
    sk = peer_subkeys.reshape(depth, 2 * PEER_HEADS, PEER_NKEYS, PEER_DKEY // 2)
    u_tab = peer_u.astype(BF16)
    v_tab = peer_v.astype(BF16)
    new_states = []
    for l in range(depth):
        last = l == depth - 1

        def path(x, nseq, n, row_fn, grid, s0):
            proj = _proj(x, mod4, row_fn, l, norm_g4, w_in)
            ya = _gmlp(proj, l, gmlp_ws, bs_t, ln_g)
            yb = _pool(proj, l, pool_w, ps, nseq, n, grid)
            ret = _retention(proj, ret_decay, nseq, n, s0, l)
            yc, s_fin = (ret, None) if s0 is not None else ret
            xn, h2 = _merge(x, proj, ya, yb, yc, mod4, row_fn, l, norm_g4, w_out)
            gates = _route(h2, l, peer_wq, sk)
            return _peer(xn, h2, gates, mod4, row_fn, l, fg, u_tab, v_tab, last), s_fin

        xc, s_fin = path(xc, batch, seq, ctx_row, False, None)
        new_states.append(s_fin)
        xs, _ = path(xs, dec_batch, dec_seq, lat_row, True, state_ret)

    new_state_ret = jnp.stack(new_states, axis=1).astype(x_prompt.dtype)
    return (xc.reshape(batch, seq, d), xs.reshape(dec_batch, dec_seq, d), new_state_ret)
```

```python
import functools

import numpy as np
import jax
import jax.numpy as jnp
from jax import lax
from jax.experimental import pallas as pl
from jax.experimental.pallas import tpu as pltpu

F32 = jnp.float32
BF16 = jnp.bfloat16
I32 = jnp.int32

D_MODEL = 1024
A_GROUPS = 4
A_CHUNK = 128
POOL_WINDOWS = (2, 4, 8, 16)
B_GW = D_MODEL // len(POOL_WINDOWS)
GRID_W = 64
RET_HEADS = 4
RET_DV = D_MODEL // RET_HEADS
RET_DK = RET_DV // 2
RET_CHUNK = 128
PEER_HEADS = 8
PEER_NKEYS = 128
PEER_DKEY = 256
PEER_TOPK = 16
EPS = 1e-6
IN_COLS = 10 * D_MODEL

SUBLANES = 8
LANES = 128
VMEM_LIMIT = 56 * 1024 * 1024

MOD_ROWS = 16
PROJ_TB = 512
PROJ_TN = 2048
PROJ_DTYPE = BF16
Y_DTYPE = BF16
GMLP_TB = 256
MERGE_TB = 512
ROUTE_TB = SUBLANES * LANES
ROUTE_STEPS = 8
ROUTE_MTOK = ROUTE_TB // ROUTE_STEPS
ROUTE_MPITCH = ROUTE_MTOK + SUBLANES
PEER_TB = 1024
PEER_CH = 1024
PEER_SUB = 256


def _gelu(x):
    return 0.5 * x * (1.0 + jnp.tanh(0.7978845608028654 * (x + 0.044715 * (x * x * x))))


def _sigmoid(x):
    return 1.0 / (1.0 + jnp.exp(-x))


def _rms(x, g):
    return x * lax.rsqrt(jnp.mean(x * x, axis=-1, keepdims=True) + EPS) * g


def _params(*sem):
    return pltpu.CompilerParams(dimension_semantics=sem, vmem_limit_bytes=VMEM_LIMIT)


def _tree(fn, xs):
    xs = list(xs)
    while len(xs) > 1:
        xs = [fn(xs[i], xs[i + 1]) for i in range(0, len(xs) - 1, 2)] + ([xs[-1]] if len(xs) % 2 else [])
    return xs[0]


def _nt_dot(a, b):
    return lax.dot_general(a, b, (((1,), (1,)), ((), ())), preferred_element_type=F32)


def _mod_kernel(cond_ref, w_ref, b_ref, o_ref):
    c = cond_ref[...]
    s = (c * _sigmoid(c)).astype(BF16)
    o_ref[...] = jnp.dot(s, w_ref[...].astype(BF16), preferred_element_type=F32) + b_ref[...]


def _modulation(cond, w_mod, b_mod):
    depth = w_mod.shape[0]
    nj = w_mod.shape[2] // D_MODEL
    return pl.pallas_call(
        _mod_kernel,
        grid=(depth, nj),
        in_specs=[
            pl.BlockSpec((MOD_ROWS, D_MODEL), lambda l, j: (0, 0)),
            pl.BlockSpec((None, D_MODEL, D_MODEL), lambda l, j: (l, 0, j)),
            pl.BlockSpec((None, 1, D_MODEL), lambda l, j: (l, 0, j)),
        ],
        out_specs=pl.BlockSpec((None, MOD_ROWS, D_MODEL), lambda l, j: (l, 0, j)),
        out_shape=jax.ShapeDtypeStruct((depth, MOD_ROWS, nj * D_MODEL), F32),
        compiler_params=_params("parallel", "parallel"),
        name="mod",
    )(cond, w_mod, b_mod.reshape(depth, 1, nj * D_MODEL))


def _proj_kernel(x_ref, mod_ref, g_ref, w_ref, o_ref, wb_ref):
    @pl.when(pl.program_id(1) == 0)
    def _():
        wb_ref[...] = w_ref[...].astype(BF16)

    y = _rms(x_ref[...], g_ref[...])
    sh = mod_ref[:, 0:D_MODEL]
    sc = mod_ref[:, D_MODEL:2 * D_MODEL]
    h = (y * (1.0 + sc) + sh).astype(BF16)
    o_ref[...] = jnp.dot(h, wb_ref[...], preferred_element_type=F32).astype(o_ref.dtype)


def _proj(x, mod4, row_fn, layer, norm_g, w_in):
    n = x.shape[0]
    tb, tn = PROJ_TB, PROJ_TN
    return pl.pallas_call(
        _proj_kernel,
        grid=(IN_COLS // tn, n // tb),
        in_specs=[
            pl.BlockSpec((tb, D_MODEL), lambda j, i: (i, 0)),
            pl.BlockSpec((None, None, 1, 6 * D_MODEL), lambda j, i: (layer, row_fn(i * tb), 0, 0)),
            pl.BlockSpec((None, None, 1, D_MODEL), lambda j, i: (layer, 0, 0, 0)),
            pl.BlockSpec((None, D_MODEL, tn), lambda j, i: (layer, 0, j)),
        ],
        out_specs=pl.BlockSpec((tb, tn), lambda j, i: (i, j)),
        out_shape=jax.ShapeDtypeStruct((n, IN_COLS), PROJ_DTYPE),
        scratch_shapes=[pltpu.VMEM((D_MODEL, tn), BF16)],
        compiler_params=_params("parallel", "arbitrary"),
        name="proj",
    )(x, mod4, norm_g, w_in)


def _gmlp_kernel(u_ref, v_ref, ws_ref, bs_ref, lng_ref, o_ref):
    gw = D_MODEL // A_GROUPS
    for c in range(GMLP_TB // A_CHUNK):
        rows = slice(c * A_CHUNK, (c + 1) * A_CHUNK)
        v = _gelu(v_ref[rows, :].astype(F32))
        vc = v - jnp.mean(v, axis=-1, keepdims=True)
        vn = vc * lax.rsqrt(jnp.mean(vc * vc, axis=-1, keepdims=True) + EPS) * lng_ref[...]
        vnb = vn.astype(BF16)
        for g in range(A_GROUPS):
            cols = slice(g * gw, (g + 1) * gw)
            sv = jnp.dot(ws_ref[g].astype(BF16), vnb[:, cols], preferred_element_type=F32) + bs_ref[:, g:g + 1]
            o_ref[rows, cols] = (_gelu(u_ref[rows, cols].astype(F32)) * sv).astype(o_ref.dtype)


def _gmlp(proj, layer, ws, bs_t, ln_g):
    n = proj.shape[0]
    tb = GMLP_TB
    return pl.pallas_call(
        _gmlp_kernel,
        grid=(n // tb,),
        in_specs=[
            pl.BlockSpec((tb, D_MODEL), lambda i: (i, 0)),
            pl.BlockSpec((tb, D_MODEL), lambda i: (i, 1)),
            pl.BlockSpec((None, A_GROUPS, A_CHUNK, A_CHUNK), lambda i: (layer, 0, 0, 0)),
            pl.BlockSpec((None, A_CHUNK, A_GROUPS), lambda i: (layer, 0, 0)),
            pl.BlockSpec((None, 1, D_MODEL), lambda i: (layer, 0, 0)),
        ],
        out_specs=pl.BlockSpec((tb, D_MODEL), lambda i: (i, 0)),
        out_shape=jax.ShapeDtypeStruct((n, D_MODEL), Y_DTYPE),
        compiler_params=_params("parallel"),
        name="gmlp",
    )(proj, proj, ws, bs_t, ln_g)


POOL_PAD = 16


def _window_count(pos, size, a, b):
    return jnp.minimum(pos + b, size - 1) - jnp.maximum(pos - a, 0) + 1


def _pool_kernel(z_ref, w_ref, scale_ref, o_ref, zp_ref, cp_ref, *, n, grid):
    rows = n // GRID_W
    rpad = cp_ref.shape[0] - n
    tok = lax.broadcasted_iota(I32, (n, 1), 0)
    for gi, w in enumerate(POOL_WINDOWS):
        a = w // 2
        b = w - 1 - a
        cols = slice(gi * B_GW, (gi + 1) * B_GW)
        z = z_ref[:, cols].astype(F32)
        zp_ref[0:POOL_PAD, :] = jnp.zeros((POOL_PAD, B_GW), F32)
        zp_ref[POOL_PAD + n:, :] = jnp.zeros((POOL_PAD, B_GW), F32)
        zp_ref[POOL_PAD:POOL_PAD + n, :] = z
        if grid:
            col = tok & (GRID_W - 1)
            row = tok >> (GRID_W.bit_length() - 1)
            s = jnp.zeros((n, B_GW), F32)
            for d in range(-a, b + 1):
                sh = zp_ref[POOL_PAD + d:POOL_PAD + d + n, :]
                ok = jnp.logical_and(col + d >= 0, col + d < GRID_W)
                s = s + jnp.where(ok, sh, 0.0)
            half = rpad // 2
            cp_ref[0:half, :] = jnp.zeros((half, B_GW), F32)
            cp_ref[half + n:, :] = jnp.zeros((half, B_GW), F32)
            cp_ref[half:half + n, :] = s
            s = jnp.zeros((n, B_GW), F32)
            for d in range(-a, b + 1):
                s = s + cp_ref[half + d * GRID_W:half + d * GRID_W + n, :]
            cnt = (_window_count(row, rows, a, b) * _window_count(col, GRID_W, a, b)).astype(F32)
        else:
            s = jnp.zeros((n, B_GW), F32)
            for d in range(-a, b + 1):
                s = s + zp_ref[POOL_PAD + d:POOL_PAD + d + n, :]
            cnt = _window_count(tok, n, a, b).astype(F32)
        diff = (s / cnt - z).astype(BF16)
        o_ref[:, cols] = (jnp.dot(diff, w_ref[gi].astype(BF16), preferred_element_type=F32)
                          * scale_ref[:, cols]).astype(o_ref.dtype)


def _pool(proj, layer, pool_w, pool_scale, nseq, n, grid):
    rpad = 2 * (max(POOL_WINDOWS) // 2) * GRID_W if grid else 2 * SUBLANES
    return pl.pallas_call(
        functools.partial(_pool_kernel, n=n, grid=grid),
        grid=(nseq,),
        in_specs=[
            pl.BlockSpec((n, D_MODEL), lambda i: (i, 2)),
            pl.BlockSpec((None, len(POOL_WINDOWS), B_GW, B_GW), lambda i: (layer, 0, 0, 0)),
            pl.BlockSpec((None, 1, D_MODEL), lambda i: (layer, 0, 0)),
        ],
        out_specs=pl.BlockSpec((n, D_MODEL), lambda i: (i, 0)),
        out_shape=jax.ShapeDtypeStruct((nseq * n, D_MODEL), Y_DTYPE),
        scratch_shapes=[pltpu.VMEM((n + 2 * POOL_PAD, B_GW), F32), pltpu.VMEM((n + rpad, B_GW), F32)],
        compiler_params=_params("parallel"),
        name="pool",
    )(proj, pool_w, pool_scale)


def _log_sigmoid(x):
    return jnp.minimum(x, 0.0) - jnp.log(1.0 + jnp.exp(-jnp.abs(x)))


def _ret_kernel(*refs, n, has_s0):
    if has_s0:
        q_ref, k_ref, v_ref, gf_ref, gb_ref, rd_ref, s0_ref, y_ref, s_ref, dm_ref, yf_ref = refs
        sfin_ref = None
    else:
        q_ref, k_ref, v_ref, gf_ref, gb_ref, rd_ref, y_ref, sfin_ref, s_ref, dm_ref, yf_ref = refs
        s0_ref = None
    cl = RET_CHUNK
    nc = n // cl
    pi = lax.broadcasted_iota(I32, (cl, cl), 0).astype(F32)
    pj = lax.broadcasted_iota(I32, (cl, cl), 1).astype(F32)
    pcol = lax.broadcasted_iota(I32, (cl, 1), 0).astype(F32)
    kscale = RET_DK ** -0.5

    for d in range(2):
        g_ref = gf_ref if d == 0 else gb_ref
        lgs = []
        for h in range(RET_HEADS):
            lg = _log_sigmoid(rd_ref[d:d + 1, h:h + 1])
            lgs.append(lg)
            rel = (pi - pj) if d == 0 else (pj - pi)
            dm_ref[h] = jnp.where(rel >= 0.0, jnp.exp(lg * jnp.maximum(rel, 0.0)), 0.0)
            if has_s0:
                s_ref[h] = s0_ref[d, h]
            else:
                s_ref[h] = jnp.zeros((RET_DK, RET_DV), F32)

        def chunk(ci, carry, d=d, g_ref=g_ref, lgs=lgs):
            c = ci if d == 0 else nc - 1 - ci
            rows = pl.ds(pl.multiple_of(c * cl, cl), cl)
            for h in range(RET_HEADS):
                lg = lgs[h]
                if d == 0:
                    qdec = jnp.exp(lg * (pcol + 1.0))
                    kdec = jnp.exp(lg * (cl - 1.0 - pcol))
                else:
                    qdec = jnp.exp(lg * (cl - pcol))
                    kdec = jnp.exp(lg * pcol)
                cdec = jnp.exp(lg * float(cl))
                q = q_ref[rows, h * RET_DK:(h + 1) * RET_DK].astype(F32)
                k = k_ref[rows, h * RET_DK:(h + 1) * RET_DK].astype(F32) * kscale
                vb = v_ref[rows, h * RET_DV:(h + 1) * RET_DV].astype(BF16)
                sc = _nt_dot(q.astype(BF16), k.astype(BF16)) * dm_ref[h]
                o = jnp.dot(sc.astype(BF16), vb, preferred_element_type=F32)
                s_prev = s_ref[h]
                o = o + jnp.dot((q * qdec).astype(BF16), s_prev.astype(BF16), preferred_element_type=F32)
                kd_t = jnp.transpose(k * kdec).astype(BF16)
                s_ref[h] = cdec * s_prev + jnp.dot(kd_t, vb, preferred_element_type=F32)
                on = o * lax.rsqrt(jnp.mean(o * o, axis=-1, keepdims=True) + EPS)
                g = g_ref[rows, h * RET_DV:(h + 1) * RET_DV].astype(F32)
                yv = g * _sigmoid(g) * on
                cols = slice(h * RET_DV, (h + 1) * RET_DV)
                if d == 0:
                    yf_ref[rows, cols] = yv
                else:
                    y_ref[rows, cols] = (yf_ref[rows, cols] + yv).astype(y_ref.dtype)
            return carry

        lax.fori_loop(0, nc, chunk, 0)
        if not has_s0:
            for h in range(RET_HEADS):
                sfin_ref[d, h] = s_ref[h]


def _retention(proj, ret_decay, nseq, n, s0, layer):
    has_s0 = s0 is not None
    dkb = RET_HEADS * RET_DK
    in_specs = [
        pl.BlockSpec((n, dkb), lambda i: (i, 3 * D_MODEL // dkb)),
        pl.BlockSpec((n, dkb), lambda i: (i, 3 * D_MODEL // dkb + 1)),
        pl.BlockSpec((n, D_MODEL), lambda i: (i, 4)),
        pl.BlockSpec((n, D_MODEL), lambda i: (i, 5)),
        pl.BlockSpec((n, D_MODEL), lambda i: (i, 6)),
        pl.BlockSpec((None, 2, RET_HEADS), lambda i: (layer, 0, 0)),
    ]
    args = [proj, proj, proj, proj, proj, ret_decay]
    y_shape = jax.ShapeDtypeStruct((nseq * n, D_MODEL), Y_DTYPE)
    y_spec = pl.BlockSpec((n, D_MODEL), lambda i: (i, 0))
    if has_s0:
        in_specs.append(pl.BlockSpec((None, None, 2, RET_HEADS, RET_DK, RET_DV), lambda i: (i, layer, 0, 0, 0, 0)))
        args.append(s0)
        out_shape, out_specs = y_shape, y_spec
    else:
        out_shape = (y_shape, jax.ShapeDtypeStruct((nseq, 2, RET_HEADS, RET_DK, RET_DV), F32))
        out_specs = (y_spec, pl.BlockSpec((None, 2, RET_HEADS, RET_DK, RET_DV), lambda i: (i, 0, 0, 0, 0)))
    return pl.pallas_call(
        functools.partial(_ret_kernel, n=n, has_s0=has_s0),
        grid=(nseq,),
        in_specs=in_specs,
        out_specs=out_specs,
        out_shape=out_shape,
        scratch_shapes=[pltpu.VMEM((RET_HEADS, RET_DK, RET_DV), F32),
                        pltpu.VMEM((RET_HEADS, RET_CHUNK, RET_CHUNK), F32),
                        pltpu.VMEM((n, D_MODEL), F32)],
        compiler_params=_params("parallel"),
        name="ret",
    )(*args)


def _merge_kernel(x_ref, ga_ref, gb_ref, gc_ref, ya_ref, yb_ref, yc_ref, mod_ref, g2_ref, w_ref, xo_ref, h2_ref):
    merged = (_sigmoid(ga_ref[...].astype(F32)) * ya_ref[...] + _sigmoid(gb_ref[...].astype(F32)) * yb_ref[...]
              + _sigmoid(gc_ref[...].astype(F32)) * yc_ref[...])
    o = jnp.dot(merged.astype(BF16), w_ref[...].astype(BF16), preferred_element_type=F32)
    gt1 = mod_ref[:, 2 * D_MODEL:3 * D_MODEL]
    sh2 = mod_ref[:, 3 * D_MODEL:4 * D_MODEL]
    sc2 = mod_ref[:, 4 * D_MODEL:5 * D_MODEL]
    xn = x_ref[...] + gt1 * o
    xo_ref[...] = xn
    h2_ref[...] = (_rms(xn, g2_ref[...]) * (1.0 + sc2) + sh2).astype(BF16)


def _merge(x, proj, ya, yb, yc, mod4, row_fn, layer, norm_g2, w_out):
    n = x.shape[0]
    tb = MERGE_TB
    tok = lambda c: pl.BlockSpec((tb, D_MODEL), lambda i: (i, c))
    return pl.pallas_call(
        _merge_kernel,
        grid=(n // tb,),
        in_specs=[
            tok(0), tok(7), tok(8), tok(9), tok(0), tok(0), tok(0),
            pl.BlockSpec((None, None, 1, 6 * D_MODEL), lambda i: (layer, row_fn(i * tb), 0, 0)),
            pl.BlockSpec((None, None, 1, D_MODEL), lambda i: (layer, 1, 0, 0)),
            pl.BlockSpec((None, D_MODEL, D_MODEL), lambda i: (layer, 0, 0)),
        ],
        out_specs=(tok(0), tok(0)),
        out_shape=(jax.ShapeDtypeStruct((n, D_MODEL), F32), jax.ShapeDtypeStruct((n, D_MODEL), BF16)),
        compiler_params=_params("parallel"),
        name="merge",
    )(x, proj, proj, proj, ya, yb, yc, mod4, norm_g2, w_out)


_CAND_PAIRS = tuple((j1, j2) for j1 in range(PEER_TOPK) for j2 in range(PEER_TOPK)
                    if (j1 + 1) * (j2 + 1) <= PEER_TOPK)
ROUTE_ACCS = 4


def _scan_max(n_items, load, knock, store):
    per = -(-n_items // ROUTE_ACCS)
    accs = []
    for a0 in range(0, n_items, per):
        bv = bt = None
        for i in range(a0, min(a0 + per, n_items)):
            v, t = load(i)
            v = knock(i, v, t)
            store(i, v)
            if bv is None:
                bv, bt = v, t
            else:
                gt = v > bv
                bv = jnp.where(gt, v, bv)
                bt = jnp.where(gt, t, bt)
        accs.append((bv, bt))
    bv, bt = accs[0]
    for v, t in accs[1:]:
        gt = v > bv
        bv = jnp.where(gt, v, bv)
        bt = jnp.where(gt, t, bt)
    return bv, bt


def _route_kernel(h_ref, wq_ref, sk_ref, m_ref, q_scr, x_scr, ts_scr, ti_scr, c_scr, k_scr, eo_scr, go_scr,
                  pe_scr, pg_scr, *, n_blocks):
    tb = h_ref.shape[0]
    nk = PEER_NKEYS
    kd = PEER_DKEY // 2
    nlt = tb // LANES
    assert nlt == SUBLANES
    neg = -jnp.inf
    tile = lambda i: pl.ds(i * SUBLANES, SUBLANES)
    blk = pl.program_id(0)
    step = pl.program_id(1)
    live = blk < n_blocks

    @pl.when(jnp.logical_and(blk == 0, step == 0))
    def _():
        pe_scr[...] = jnp.zeros_like(pe_scr)
        pg_scr[...] = jnp.zeros_like(pg_scr)

    @pl.when(jnp.logical_and(live, step == 0))
    def _():
        for hd in range(PEER_HEADS):
            cols = slice(hd * PEER_DKEY, (hd + 1) * PEER_DKEY)
            q = jnp.dot(h_ref[...], wq_ref[:, cols].astype(BF16), preferred_element_type=F32).astype(BF16)
            q_scr[2 * hd] = q[:, :kd]
            q_scr[2 * hd + 1] = q[:, kd:]

    def gate_tasks():
        t0 = step * ROUTE_MTOK
        key_iota = lax.broadcasted_iota(I32, (nk, nk), 0).astype(BF16)
        zero = jnp.zeros((nk, nk), BF16)
        one = jnp.ones((nk, nk), BF16)
        m_flat = m_ref.reshape(nk * ROUTE_MPITCH, nk)
        m_ref[:, ROUTE_MTOK:, :] = jnp.zeros((nk, ROUTE_MPITCH - ROUTE_MTOK, nk), F32)
        group_rows = {}

        def pair_task(grp, pair):
            if pair == 0:
                rows = pl.ds(pl.multiple_of(t0 + grp * SUBLANES, SUBLANES), SUBLANES)
                er = pe_scr[rows, :]
                group_rows[grp] = ((er >> 7).astype(F32).astype(BF16), (er & (nk - 1)).astype(F32).astype(BF16),
                                   pg_scr[rows, :].astype(BF16))
            i1r, i2r, gr = group_rows[grp]
            at, bt = [], []
            for tt in (2 * pair, 2 * pair + 1):
                at.append(jnp.where(key_iota == i1r[tt:tt + 1, :], gr[tt:tt + 1, :], zero))
                bt.append(jnp.where(key_iota == i2r[tt:tt + 1, :], one, zero))
            lhs = jnp.concatenate([jnp.concatenate([at[0], zero], axis=1),
                                   jnp.concatenate([zero, at[1]], axis=1)], axis=0)
            out = _nt_dot(lhs, jnp.concatenate(bt, axis=1))
            tok = grp * SUBLANES + 2 * pair
            m_flat[pl.ds(tok, nk, stride=ROUTE_MPITCH), :] = out[:nk, :]
            m_flat[pl.ds(tok + 1, nk, stride=ROUTE_MPITCH), :] = out[nk:, :]

        return [functools.partial(pair_task, grp, pair)
                for grp in range(ROUTE_MTOK // SUBLANES) for pair in range(SUBLANES // 2)]

    def side(hs, between_rounds):
        sk = sk_ref[hs].astype(BF16)
        for j in range(nlt):
            sc = _nt_dot(sk, q_scr[hs, j * LANES:(j + 1) * LANES, :])
            x_scr[pl.ds(j, nk, stride=nlt), :] = sc
        prev = None
        for r in range(PEER_TOPK):
            load = lambda n: (x_scr[tile(n), :], jnp.full((SUBLANES, LANES), n, I32))
            if prev is None:
                knock = lambda n, v, t: v
                store = lambda n, v: None
            else:
                knock = lambda n, v, t, prev=prev: jnp.where(prev == n, neg, v)
                store = lambda n, v: x_scr.__setitem__((tile(n), slice(None)), v)
            bv, bi = _scan_max(nk, load, knock, store)
            ts_scr[hs, r] = bv
            ti_scr[hs, r] = bi
            prev = bi
            between_rounds()

    sides_per_step = 2 * PEER_HEADS // ROUTE_STEPS

    @pl.when(live)
    def _():
        tasks = gate_tasks()
        per_round = -(-len(tasks) // (sides_per_step * PEER_TOPK))

        def between_rounds():
            for _ in range(min(per_round, len(tasks))):
                tasks.pop(0)()

        for k in range(sides_per_step):
            side(step * sides_per_step + k, between_rounds)
        while tasks:
            tasks.pop(0)()

    @pl.when(blk == n_blocks)
    def _():
        for task in gate_tasks():
            task()

    def head(h, carry):
        for p, (j1, j2) in enumerate(_CAND_PAIRS):
            c_scr[tile(p), :] = ts_scr[2 * h, j1] + ts_scr[2 * h + 1, j2]
            k_scr[tile(p), :] = ((j1 * PEER_TOPK + j2) * (nk * nk) + ti_scr[2 * h, j1] * nk
                                 + ti_scr[2 * h + 1, j2])
        prev = None
        vals, exps = [], []
        for r in range(PEER_TOPK):
            load = lambda p: (c_scr[tile(p), :], k_scr[tile(p), :])
            if prev is None:
                knock = lambda p, v, t: v
                store = lambda p, v: None
            else:
                knock = lambda p, v, t, prev=prev: jnp.where(t == prev, neg, v)
                store = lambda p, v: c_scr.__setitem__((tile(p), slice(None)), v)
            bv, bk = _scan_max(len(_CAND_PAIRS), load, knock, store)
            vals.append(bv)
            exps.append(bk & (nk * nk - 1))
            prev = bk
        ps = [jnp.exp(v - vals[0]) for v in vals]
        z = _tree(jnp.add, ps)
        for r in range(PEER_TOPK):
            row = pl.ds(pl.multiple_of((h * PEER_TOPK + r) * SUBLANES, SUBLANES), SUBLANES)
            go_scr[row, :] = ps[r] / z
            eo_scr[row, :] = exps[r]
        return carry

    @pl.when(jnp.logical_and(live, step == ROUTE_STEPS - 1))
    def _():
        lax.fori_loop(0, PEER_HEADS, head, 0)
        nsel = PEER_HEADS * PEER_TOPK
        for j in range(nlt):
            rows = slice(j * LANES, (j + 1) * LANES)
            pe_scr[rows, :] = jnp.transpose(eo_scr[pl.ds(j, nsel, stride=nlt), :])
            pg_scr[rows, :] = jnp.transpose(go_scr[pl.ds(j, nsel, stride=nlt), :])


def _route(h2, layer, wq, subkeys):
    n = h2.shape[0]
    tb = ROUTE_TB
    n_blocks = n // tb
    nsel = PEER_HEADS * PEER_TOPK
    kd = PEER_DKEY // 2
    nk = PEER_NKEYS
    tile_rows = lambda count: count * SUBLANES
    gates = pl.pallas_call(
        functools.partial(_route_kernel, n_blocks=n_blocks),
        grid=(n_blocks + 1, ROUTE_STEPS),
        in_specs=[
            pl.BlockSpec((tb, D_MODEL), lambda i, s: (jnp.minimum(i, n_blocks - 1), 0)),
            pl.BlockSpec((None, D_MODEL, PEER_HEADS * PEER_DKEY), lambda i, s: (layer, 0, 0)),
            pl.BlockSpec((None, 2 * PEER_HEADS, PEER_NKEYS, kd), lambda i, s: (layer, 0, 0, 0)),
        ],
        out_specs=pl.BlockSpec((nk, ROUTE_MPITCH, nk),
                               lambda i, s: (0, jnp.maximum((i - 1) * ROUTE_STEPS + s, 0), 0)),
        out_shape=jax.ShapeDtypeStruct((nk, n // ROUTE_MTOK * ROUTE_MPITCH, nk), F32),
        scratch_shapes=[
            pltpu.VMEM((2 * PEER_HEADS, tb, kd), BF16),
            pltpu.VMEM((tile_rows(PEER_NKEYS), LANES), F32),
            pltpu.VMEM((2 * PEER_HEADS, PEER_TOPK, SUBLANES, LANES), F32),
            pltpu.VMEM((2 * PEER_HEADS, PEER_TOPK, SUBLANES, LANES), I32),
            pltpu.VMEM((tile_rows(len(_CAND_PAIRS)), LANES), F32),
            pltpu.VMEM((tile_rows(len(_CAND_PAIRS)), LANES), I32),
            pltpu.VMEM((tile_rows(nsel), LANES), I32),
            pltpu.VMEM((tile_rows(nsel), LANES), F32),
            pltpu.VMEM((tb, nsel), I32),
            pltpu.VMEM((tb, nsel), F32),
        ],
        compiler_params=_params("arbitrary", "arbitrary"),
        name="route",
    )(h2, wq, subkeys)
    return gates


def _peer_kernel(x_ref, h_ref, m_ref, mod_ref, fg_ref, u_ref, v_ref, o_ref, p0_ref, p1_ref, acc_ref,
                 *, final, n_chunks):
    c = pl.program_id(1)

    def step(prev_ref, cur_ref):
        n_sub = PEER_CH // PEER_SUB
        nw = D_MODEL // n_sub
        for sub in range(n_sub):
            if cur_ref is not None:
                rows = slice(sub * PEER_SUB, (sub + 1) * PEER_SUB)
                s = _gelu(_nt_dot(h_ref[...], u_ref[rows, :]))
                per = PEER_SUB // PEER_NKEYS
                m = jnp.concatenate(
                    [jnp.concatenate([m_ref[sub * per + a, g * ROUTE_MPITCH:g * ROUTE_MPITCH + ROUTE_MTOK, :]
                                      for g in range(h_ref.shape[0] // ROUTE_MTOK)], axis=0)
                     for a in range(per)], axis=1)
                cur_ref[:, rows] = (m * s).astype(BF16)
            if prev_ref is not None:
                cols = slice(sub * nw, (sub + 1) * nw)
                acc_ref[:, cols] += jnp.dot(prev_ref[...], v_ref[:, cols], preferred_element_type=F32)

    stage = (p0_ref, p1_ref)
    steady = jnp.logical_and(c > 0, c < n_chunks)

    @pl.when(c == 0)
    def _():
        acc_ref[...] = jnp.zeros_like(acc_ref)
        step(None, stage[0])

    for parity in range(2):
        @pl.when(jnp.logical_and(steady, c % 2 == parity))
        def _(parity=parity):
            step(stage[1 - parity], stage[parity])

    @pl.when(c == n_chunks)
    def _():
        step(stage[(n_chunks - 1) % 2], None)
        gt2 = mod_ref[:, 5 * D_MODEL:6 * D_MODEL]
        xn = x_ref[...] + gt2 * acc_ref[...]
        o_ref[...] = _rms(xn, fg_ref[...]) if final else xn


def _peer(x, h2, gates, mod4, row_fn, layer, final_g, u_tab, v_tab, final):
    n = x.shape[0]
    tb = PEER_TB
    n_chunks = u_tab.shape[1] // PEER_CH
    return pl.pallas_call(
        functools.partial(_peer_kernel, final=final, n_chunks=n_chunks),
        grid=(n // tb, n_chunks + 1),
        in_specs=[
            pl.BlockSpec((tb, D_MODEL), lambda i, c: (i, 0)),
            pl.BlockSpec((tb, D_MODEL), lambda i, c: (i, 0)),
            pl.BlockSpec((PEER_CH // PEER_NKEYS, tb // ROUTE_MTOK * ROUTE_MPITCH, PEER_NKEYS),
                         lambda i, c: (jnp.minimum(c, n_chunks - 1), i, 0)),
            pl.BlockSpec((None, None, 1, 6 * D_MODEL), lambda i, c: (layer, row_fn(i * tb), 0, 0)),
            pl.BlockSpec((1, D_MODEL), lambda i, c: (0, 0)),
            pl.BlockSpec((None, PEER_CH, D_MODEL), lambda i, c: (layer, jnp.minimum(c, n_chunks - 1), 0)),
            pl.BlockSpec((None, PEER_CH, D_MODEL), lambda i, c: (layer, jnp.maximum(c - 1, 0), 0)),
        ],
        out_specs=pl.BlockSpec((tb, D_MODEL), lambda i, c: (i, 0)),
        out_shape=jax.ShapeDtypeStruct((n, D_MODEL), F32),
        scratch_shapes=[pltpu.VMEM((tb, PEER_CH), BF16),
                        pltpu.VMEM((tb, PEER_CH), BF16),
                        pltpu.VMEM((tb, D_MODEL), F32)],
        compiler_params=_params("parallel", "arbitrary"),
        name="peer",
    )(x, h2, gates, mod4, final_g, u_tab, v_tab)


def kernel(x_prompt, x_sample, state_ret, c, c_ctx, w_mod, b_mod, norm_g, w_in, w_out, gmlp_ws, gmlp_b,
           gmlp_ln_g, pool_w, pool_scale, ret_decay, peer_wq, peer_subkeys, peer_u, peer_v, final_norm_g):
    batch, seq, d = x_prompt.shape
    dec_batch, dec_seq, _ = x_sample.shape
    depth = w_mod.shape[0]
    assert d == D_MODEL and dec_batch + 1 <= MOD_ROWS and dec_seq % GRID_W == 0

    cond = jnp.zeros((MOD_ROWS, d), F32).at[0].set(c_ctx).at[1:1 + dec_batch].set(c)
    mod = _modulation(cond, w_mod, b_mod)
    mod4 = mod.reshape(depth, MOD_ROWS, 1, 6 * d)
    ctx_row = lambda tok0: 0
    lat_row = lambda tok0: 1 + tok0 // dec_seq

    xc = x_prompt.reshape(batch * seq, d)
    xs = x_sample.reshape(dec_batch * dec_seq, d)
    fg = final_norm_g.reshape(1, d)
    norm_g4 = norm_g.reshape(depth, 2, 1, d)
    bs_t = jnp.swapaxes(gmlp_b, 1, 2)
    ln_g = gmlp_ln_g.reshape(depth, 1, d)
    ps = pool_scale.reshape(depth, 1, d)
    sk = peer_subkeys.reshape(depth, 2 * PEER_HEADS, PEER_NKEYS, PEER_DKEY // 2)
    u_tab = peer_u.astype(BF16)
    v_tab = peer_v.astype(BF16)
    new_states = []
    for l in range(depth):
        last = l == depth - 1

        def path(x, nseq, n, row_fn, grid, s0):
            proj = _proj(x, mod4, row_fn, l, norm_g4, w_in)
            ya = _gmlp(proj, l, gmlp_ws, bs_t, ln_g)
            yb = _pool(proj, l, pool_w, ps, nseq, n, grid)
            ret = _retention(proj, ret_decay, nseq, n, s0, l)
            yc, s_fin = (ret, None) if s0 is not None else ret
            xn, h2 = _merge(x, proj, ya, yb, yc, mod4, row_fn, l, norm_g4, w_out)
            gates = _route(h2, l, peer_wq, sk)
            return _peer(xn, h2, gates, mod4, row_fn, l, fg, u_tab, v_tab, last), s_fin

        xc, s_fin = path(xc, batch, seq, ctx_row, False, None)
        new_states.append(s_fin)
        xs, _ = path(xs, dec_batch, dec_seq, lat_row, True, state_ret)

    new_state_ret = jnp.stack(new_states, axis=1).astype(x_prompt.dtype)
    return (xc.reshape(batch, seq, d), xs.reshape(dec_batch, dec_seq, d), new_state_ret)
```

```python
import functools

import numpy as np
import jax
import jax.numpy as jnp
from jax import lax
from jax.experimental import pallas as pl
from jax.experimental.pallas import tpu as pltpu

F32 = jnp.float32
BF16 = jnp.bfloat16
I32 = jnp.int32

D_MODEL = 1024
A_GROUPS = 4
A_CHUNK = 128
POOL_WINDOWS = (2, 4, 8, 16)
B_GW = D_MODEL // len(POOL_WINDOWS)
GRID_W = 64
RET_HEADS = 4
RET_DV = D_MODEL // RET_HEADS
RET_DK = RET_DV // 2
RET_CHUNK = 128
PEER_HEADS = 8
PEER_NKEYS = 128
PEER_DKEY = 256
PEER_TOPK = 16
EPS = 1e-6
IN_COLS = 10 * D_MODEL

SUBLANES = 8
LANES = 128
VMEM_LIMIT = 56 * 1024 * 1024

MOD_ROWS = 16
PROJ_TB = 1024
PROJ_TN = 2048
PROJ_DTYPE = BF16
Y_DTYPE = BF16
GMLP_TB = 512
MERGE_TB = 1024
ROUTE_TB = SUBLANES * LANES
ROUTE_STEPS = 8
ROUTE_MTOK = ROUTE_TB // ROUTE_STEPS
ROUTE_MPITCH = ROUTE_MTOK + SUBLANES
PEER_TB = 1024
PEER_CH = 1024
PEER_SUB = 256


def _gelu(x):
    return 0.5 * x * (1.0 + jnp.tanh(0.7978845608028654 * (x + 0.044715 * (x * x * x))))


def _sigmoid(x):
    return 1.0 / (1.0 + jnp.exp(-x))


def _rms(x, g):
    return x * lax.rsqrt(jnp.mean(x * x, axis=-1, keepdims=True) + EPS) * g


def _params(*sem):
    return pltpu.CompilerParams(dimension_semantics=sem, vmem_limit_bytes=VMEM_LIMIT)


def _tree(fn, xs):
    xs = list(xs)
    while len(xs) > 1:
        xs = [fn(xs[i], xs[i + 1]) for i in range(0, len(xs) - 1, 2)] + ([xs[-1]] if len(xs) % 2 else [])
    return xs[0]


def _nt_dot(a, b):
    return lax.dot_general(a, b, (((1,), (1,)), ((), ())), preferred_element_type=F32)


def _mod_kernel(cond_ref, w_ref, b_ref, o_ref):
    c = cond_ref[...]
    s = (c * _sigmoid(c)).astype(BF16)
    o_ref[...] = jnp.dot(s, w_ref[...].astype(BF16), preferred_element_type=F32) + b_ref[...]


def _modulation(cond, w_mod, b_mod):
    depth = w_mod.shape[0]
    nj = w_mod.shape[2] // D_MODEL
    return pl.pallas_call(
        _mod_kernel,
        grid=(depth, nj),
        in_specs=[
            pl.BlockSpec((MOD_ROWS, D_MODEL), lambda l, j: (0, 0)),
            pl.BlockSpec((None, D_MODEL, D_MODEL), lambda l, j: (l, 0, j)),
            pl.BlockSpec((None, 1, D_MODEL), lambda l, j: (l, 0, j)),
        ],
        out_specs=pl.BlockSpec((None, MOD_ROWS, D_MODEL), lambda l, j: (l, 0, j)),
        out_shape=jax.ShapeDtypeStruct((depth, MOD_ROWS, nj * D_MODEL), F32),
        compiler_params=_params("parallel", "parallel"),
        name="mod",
    )(cond, w_mod, b_mod.reshape(depth, 1, nj * D_MODEL))


def _proj_kernel(x_ref, mod_ref, g_ref, w_ref, o_ref, wb_ref):
    @pl.when(pl.program_id(1) == 0)
    def _():
        wb_ref[...] = w_ref[...].astype(BF16)

    y = _rms(x_ref[...], g_ref[...])
    sh = mod_ref[:, 0:D_MODEL]
    sc = mod_ref[:, D_MODEL:2 * D_MODEL]
    h = (y * (1.0 + sc) + sh).astype(BF16)
    o_ref[...] = jnp.dot(h, wb_ref[...], preferred_element_type=F32).astype(o_ref.dtype)


def _proj(x, mod4, row_fn, layer, norm_g, w_in):
    n = x.shape[0]
    tb, tn = PROJ_TB, PROJ_TN
    return pl.pallas_call(
        _proj_kernel,
        grid=(IN_COLS // tn, n // tb),
        in_specs=[
            pl.BlockSpec((tb, D_MODEL), lambda j, i: (i, 0)),
            pl.BlockSpec((None, None, 1, 6 * D_MODEL), lambda j, i: (layer, row_fn(i * tb), 0, 0)),
            pl.BlockSpec((None, None, 1, D_MODEL), lambda j, i: (layer, 0, 0, 0)),
            pl.BlockSpec((None, D_MODEL, tn), lambda j, i: (layer, 0, j)),
        ],
        out_specs=pl.BlockSpec((tb, tn), lambda j, i: (i, j)),
        out_shape=jax.ShapeDtypeStruct((n, IN_COLS), PROJ_DTYPE),
        scratch_shapes=[pltpu.VMEM((D_MODEL, tn), BF16)],
        compiler_params=_params("parallel", "arbitrary"),
        name="proj",
    )(x, mod4, norm_g, w_in)


def _gmlp_kernel(u_ref, v_ref, ws_ref, bs_ref, lng_ref, o_ref):
    gw = D_MODEL // A_GROUPS
    for c in range(GMLP_TB // A_CHUNK):
        rows = slice(c * A_CHUNK, (c + 1) * A_CHUNK)
        v = _gelu(v_ref[rows, :].astype(F32))
        vc = v - jnp.mean(v, axis=-1, keepdims=True)
        vn = vc * lax.rsqrt(jnp.mean(vc * vc, axis=-1, keepdims=True) + EPS) * lng_ref[...]
        vnb = vn.astype(BF16)
        for g in range(A_GROUPS):
            cols = slice(g * gw, (g + 1) * gw)
            sv = jnp.dot(ws_ref[g].astype(BF16), vnb[:, cols], preferred_element_type=F32) + bs_ref[:, g:g + 1]
            o_ref[rows, cols] = (_gelu(u_ref[rows, cols].astype(F32)) * sv).astype(o_ref.dtype)


def _gmlp(proj, layer, ws, bs_t, ln_g):
    n = proj.shape[0]
    tb = GMLP_TB
    return pl.pallas_call(
        _gmlp_kernel,
        grid=(n // tb,),
        in_specs=[
            pl.BlockSpec((tb, D_MODEL), lambda i: (i, 0)),
            pl.BlockSpec((tb, D_MODEL), lambda i: (i, 1)),
            pl.BlockSpec((None, A_GROUPS, A_CHUNK, A_CHUNK), lambda i: (layer, 0, 0, 0)),
            pl.BlockSpec((None, A_CHUNK, A_GROUPS), lambda i: (layer, 0, 0)),
            pl.BlockSpec((None, 1, D_MODEL), lambda i: (layer, 0, 0)),
        ],
        out_specs=pl.BlockSpec((tb, D_MODEL), lambda i: (i, 0)),
        out_shape=jax.ShapeDtypeStruct((n, D_MODEL), Y_DTYPE),
        compiler_params=_params("parallel"),
        name="gmlp",
    )(proj, proj, ws, bs_t, ln_g)


POOL_PAD = 16


def _window_count(pos, size, a, b):
    return jnp.minimum(pos + b, size - 1) - jnp.maximum(pos - a, 0) + 1


def _pool_kernel(z_ref, w_ref, scale_ref, o_ref, zp_ref, cp_ref, *, n, grid):
    rows = n // GRID_W
    rpad = cp_ref.shape[0] - n
    tok = lax.broadcasted_iota(I32, (n, 1), 0)
    for gi, w in enumerate(POOL_WINDOWS):
        a = w // 2
        b = w - 1 - a
        cols = slice(gi * B_GW, (gi + 1) * B_GW)
        z = z_ref[:, cols].astype(F32)
        zp_ref[0:POOL_PAD, :] = jnp.zeros((POOL_PAD, B_GW), F32)
        zp_ref[POOL_PAD + n:, :] = jnp.zeros((POOL_PAD, B_GW), F32)
        zp_ref[POOL_PAD:POOL_PAD + n, :] = z
        if grid:
            col = tok & (GRID_W - 1)
            row = tok >> (GRID_W.bit_length() - 1)
            s = jnp.zeros((n, B_GW), F32)
            for d in range(-a, b + 1):
                sh = zp_ref[POOL_PAD + d:POOL_PAD + d + n, :]
                ok = jnp.logical_and(col + d >= 0, col + d < GRID_W)
                s = s + jnp.where(ok, sh, 0.0)
            half = rpad // 2
            cp_ref[0:half, :] = jnp.zeros((half, B_GW), F32)
            cp_ref[half + n:, :] = jnp.zeros((half, B_GW), F32)
            cp_ref[half:half + n, :] = s
            s = jnp.zeros((n, B_GW), F32)
            for d in range(-a, b + 1):
                s = s + cp_ref[half + d * GRID_W:half + d * GRID_W + n, :]
            cnt = (_window_count(row, rows, a, b) * _window_count(col, GRID_W, a, b)).astype(F32)
        else:
            s = jnp.zeros((n, B_GW), F32)
            for d in range(-a, b + 1):
                s = s + zp_ref[POOL_PAD + d:POOL_PAD + d + n, :]
            cnt = _window_count(tok, n, a, b).astype(F32)
        diff = (s / cnt - z).astype(BF16)
        o_ref[:, cols] = (jnp.dot(diff, w_ref[gi].astype(BF16), preferred_element_type=F32)
                          * scale_ref[:, cols]).astype(o_ref.dtype)


def _pool(proj, layer, pool_w, pool_scale, nseq, n, grid):
    rpad = 2 * (max(POOL_WINDOWS) // 2) * GRID_W if grid else 2 * SUBLANES
    return pl.pallas_call(
        functools.partial(_pool_kernel, n=n, grid=grid),
        grid=(nseq,),
        in_specs=[
            pl.BlockSpec((n, D_MODEL), lambda i: (i, 2)),
            pl.BlockSpec((None, len(POOL_WINDOWS), B_GW, B_GW), lambda i: (layer, 0, 0, 0)),
            pl.BlockSpec((None, 1, D_MODEL), lambda i: (layer, 0, 0)),
        ],
        out_specs=pl.BlockSpec((n, D_MODEL), lambda i: (i, 0)),
        out_shape=jax.ShapeDtypeStruct((nseq * n, D_MODEL), Y_DTYPE),
        scratch_shapes=[pltpu.VMEM((n + 2 * POOL_PAD, B_GW), F32), pltpu.VMEM((n + rpad, B_GW), F32)],
        compiler_params=_params("parallel"),
        name="pool",
    )(proj, pool_w, pool_scale)


def _log_sigmoid(x):
    return jnp.minimum(x, 0.0) - jnp.log(1.0 + jnp.exp(-jnp.abs(x)))


def _ret_kernel(*refs, n, has_s0):
    if has_s0:
        q_ref, k_ref, v_ref, gf_ref, gb_ref, rd_ref, s0_ref, y_ref, s_ref, dm_ref, yf_ref, dec_ref = refs
        sfin_ref = None
    else:
        q_ref, k_ref, v_ref, gf_ref, gb_ref, rd_ref, y_ref, sfin_ref, s_ref, dm_ref, yf_ref, dec_ref = refs
        s0_ref = None
    cl = RET_CHUNK
    nc = n // cl
    pi = lax.broadcasted_iota(I32, (cl, cl), 0).astype(F32)
    pj = lax.broadcasted_iota(I32, (cl, cl), 1).astype(F32)
    prow = lax.broadcasted_iota(I32, (cl, RET_DK), 0).astype(F32)
    kscale = RET_DK ** -0.5

    for d in range(2):
        g_ref = gf_ref if d == 0 else gb_ref
        lgs = []
        for h in range(RET_HEADS):
            lg = _log_sigmoid(rd_ref[d:d + 1, h:h + 1])
            lgs.append(lg)
            rel = (pi - pj) if d == 0 else (pj - pi)
            dm_ref[h] = jnp.where(rel >= 0.0, jnp.exp(lg * jnp.maximum(rel, 0.0)), 0.0)
            dec_ref[h] = jnp.exp(lg * ((prow + 1.0) if d == 0 else (cl - prow)))
            dec_ref[RET_HEADS + h] = jnp.exp(lg * ((cl - 1.0 - prow) if d == 0 else prow))
            if has_s0:
                s_ref[h] = s0_ref[d, h]
            else:
                s_ref[h] = jnp.zeros((RET_DK, RET_DV), F32)

        def chunk(ci, carry, d=d, g_ref=g_ref, lgs=lgs):
            c = ci if d == 0 else nc - 1 - ci
            rows = pl.ds(pl.multiple_of(c * cl, cl), cl)
            for h in range(RET_HEADS):
                qdec = dec_ref[h]
                kdec = dec_ref[RET_HEADS + h]
                cdec = jnp.exp(lgs[h] * float(cl))
                q = q_ref[rows, h * RET_DK:(h + 1) * RET_DK].astype(F32)
                k = k_ref[rows, h * RET_DK:(h + 1) * RET_DK].astype(F32) * kscale
                vb = v_ref[rows, h * RET_DV:(h + 1) * RET_DV].astype(BF16)
                sc = _nt_dot(q.astype(BF16), k.astype(BF16)) * dm_ref[h]
                o = jnp.dot(sc.astype(BF16), vb, preferred_element_type=F32)
                s_prev = s_ref[h]
                o = o + jnp.dot((q * qdec).astype(BF16), s_prev.astype(BF16), preferred_element_type=F32)
                kd_t = jnp.transpose(k * kdec).astype(BF16)
                s_ref[h] = cdec * s_prev + jnp.dot(kd_t, vb, preferred_element_type=F32)
                on = o * lax.rsqrt(jnp.mean(o * o, axis=-1, keepdims=True) + EPS)
                g = g_ref[rows, h * RET_DV:(h + 1) * RET_DV].astype(F32)
                yv = g * _sigmoid(g) * on
                cols = slice(h * RET_DV, (h + 1) * RET_DV)
                if d == 0:
                    yf_ref[rows, cols] = yv
                else:
                    y_ref[rows, cols] = (yf_ref[rows, cols] + yv).astype(y_ref.dtype)
            return carry

        lax.fori_loop(0, nc, chunk, 0)
        if not has_s0:
            for h in range(RET_HEADS):
                sfin_ref[d, h] = s_ref[h]


def _retention(proj, ret_decay, nseq, n, s0, layer):
    has_s0 = s0 is not None
    dkb = RET_HEADS * RET_DK
    in_specs = [
        pl.BlockSpec((n, dkb), lambda i: (i, 3 * D_MODEL // dkb)),
        pl.BlockSpec((n, dkb), lambda i: (i, 3 * D_MODEL // dkb + 1)),
        pl.BlockSpec((n, D_MODEL), lambda i: (i, 4)),
        pl.BlockSpec((n, D_MODEL), lambda i: (i, 5)),
        pl.BlockSpec((n, D_MODEL), lambda i: (i, 6)),
        pl.BlockSpec((None, 2, RET_HEADS), lambda i: (layer, 0, 0)),
    ]
    args = [proj, proj, proj, proj, proj, ret_decay]
    y_shape = jax.ShapeDtypeStruct((nseq * n, D_MODEL), Y_DTYPE)
    y_spec = pl.BlockSpec((n, D_MODEL), lambda i: (i, 0))
    if has_s0:
        in_specs.append(pl.BlockSpec((None, None, 2, RET_HEADS, RET_DK, RET_DV), lambda i: (i, layer, 0, 0, 0, 0)))
        args.append(s0)
        out_shape, out_specs = y_shape, y_spec
    else:
        out_shape = (y_shape, jax.ShapeDtypeStruct((nseq, 2, RET_HEADS, RET_DK, RET_DV), F32))
        out_specs = (y_spec, pl.BlockSpec((None, 2, RET_HEADS, RET_DK, RET_DV), lambda i: (i, 0, 0, 0, 0)))
    return pl.pallas_call(
        functools.partial(_ret_kernel, n=n, has_s0=has_s0),
        grid=(nseq,),
        in_specs=in_specs,
        out_specs=out_specs,
        out_shape=out_shape,
        scratch_shapes=[pltpu.VMEM((RET_HEADS, RET_DK, RET_DV), F32),
                        pltpu.VMEM((RET_HEADS, RET_CHUNK, RET_CHUNK), F32),
                        pltpu.VMEM((n, D_MODEL), F32),
                        pltpu.VMEM((2 * RET_HEADS, RET_CHUNK, RET_DK), F32)],
        compiler_params=_params("parallel"),
        name="ret",
    )(*args)


def _merge_kernel(x_ref, ga_ref, gb_ref, gc_ref, ya_ref, yb_ref, yc_ref, mod_ref, g2_ref, w_ref, xo_ref, h2_ref):
    merged = (_sigmoid(ga_ref[...].astype(F32)) * ya_ref[...] + _sigmoid(gb_ref[...].astype(F32)) * yb_ref[...]
              + _sigmoid(gc_ref[...].astype(F32)) * yc_ref[...])
    o = jnp.dot(merged.astype(BF16), w_ref[...].astype(BF16), preferred_element_type=F32)
    gt1 = mod_ref[:, 2 * D_MODEL:3 * D_MODEL]
    sh2 = mod_ref[:, 3 * D_MODEL:4 * D_MODEL]
    sc2 = mod_ref[:, 4 * D_MODEL:5 * D_MODEL]
    xn = x_ref[...] + gt1 * o
    xo_ref[...] = xn
    h2_ref[...] = (_rms(xn, g2_ref[...]) * (1.0 + sc2) + sh2).astype(BF16)


def _merge(x, proj, ya, yb, yc, mod4, row_fn, layer, norm_g2, w_out):
    n = x.shape[0]
    tb = MERGE_TB
    tok = lambda c: pl.BlockSpec((tb, D_MODEL), lambda i: (i, c))
    return pl.pallas_call(
        _merge_kernel,
        grid=(n // tb,),
        in_specs=[
            tok(0), tok(7), tok(8), tok(9), tok(0), tok(0), tok(0),
            pl.BlockSpec((None, None, 1, 6 * D_MODEL), lambda i: (layer, row_fn(i * tb), 0, 0)),
            pl.BlockSpec((None, None, 1, D_MODEL), lambda i: (layer, 1, 0, 0)),
            pl.BlockSpec((None, D_MODEL, D_MODEL), lambda i: (layer, 0, 0)),
        ],
        out_specs=(tok(0), tok(0)),
        out_shape=(jax.ShapeDtypeStruct((n, D_MODEL), F32), jax.ShapeDtypeStruct((n, D_MODEL), BF16)),
        compiler_params=_params("parallel"),
        name="merge",
    )(x, proj, proj, proj, ya, yb, yc, mod4, norm_g2, w_out)


_CAND_PAIRS = tuple((j1, j2) for j1 in range(PEER_TOPK) for j2 in range(PEER_TOPK)
                    if (j1 + 1) * (j2 + 1) <= PEER_TOPK)
ROUTE_ACCS = 4


def _scan_max(n_items, load, knock, store):
    per = -(-n_items // ROUTE_ACCS)
    accs = []
    for a0 in range(0, n_items, per):
        bv = bt = None
        for i in range(a0, min(a0 + per, n_items)):
            v, t = load(i)
            v = knock(i, v, t)
            store(i, v)
            if bv is None:
                bv, bt = v, t
            else:
                gt = v > bv
                bv = jnp.where(gt, v, bv)
                bt = jnp.where(gt, t, bt)
        accs.append((bv, bt))
    bv, bt = accs[0]
    for v, t in accs[1:]:
        gt = v > bv
        bv = jnp.where(gt, v, bv)
        bt = jnp.where(gt, t, bt)
    return bv, bt


def _route_kernel(h_ref, wq_ref, sk_ref, m_ref, q_scr, x_scr, ts_scr, ti_scr, c_scr, k_scr, eo_scr, go_scr,
                  pe_scr, pg_scr, *, n_blocks):
    tb = h_ref.shape[0]
    nk = PEER_NKEYS
    kd = PEER_DKEY // 2
    nlt = tb // LANES
    assert nlt == SUBLANES
    neg = -jnp.inf
    tile = lambda i: pl.ds(i * SUBLANES, SUBLANES)
    blk = pl.program_id(0)
    step = pl.program_id(1)
    live = blk < n_blocks

    @pl.when(jnp.logical_and(blk == 0, step == 0))
    def _():
        pe_scr[...] = jnp.zeros_like(pe_scr)
        pg_scr[...] = jnp.zeros_like(pg_scr)

    @pl.when(jnp.logical_and(live, step == 0))
    def _():
        for hd in range(PEER_HEADS):
            cols = slice(hd * PEER_DKEY, (hd + 1) * PEER_DKEY)
            q = jnp.dot(h_ref[...], wq_ref[:, cols].astype(BF16), preferred_element_type=F32).astype(BF16)
            q_scr[2 * hd] = q[:, :kd]
            q_scr[2 * hd + 1] = q[:, kd:]

    def gate_tasks():
        t0 = step * ROUTE_MTOK
        key_iota = lax.broadcasted_iota(I32, (nk, nk), 0).astype(BF16)
        zero = jnp.zeros((nk, nk), BF16)
        one = jnp.ones((nk, nk), BF16)
        m_flat = m_ref.reshape(nk * ROUTE_MPITCH, nk)
        m_ref[:, ROUTE_MTOK:, :] = jnp.zeros((nk, ROUTE_MPITCH - ROUTE_MTOK, nk), F32)
        group_rows = {}

        def pair_task(grp, pair):
            if pair == 0:
                rows = pl.ds(pl.multiple_of(t0 + grp * SUBLANES, SUBLANES), SUBLANES)
                er = pe_scr[rows, :]
                group_rows[grp] = ((er >> 7).astype(F32).astype(BF16), (er & (nk - 1)).astype(F32).astype(BF16),
                                   pg_scr[rows, :].astype(BF16))
            i1r, i2r, gr = group_rows[grp]
            at, bt = [], []
            for tt in (2 * pair, 2 * pair + 1):
                at.append(jnp.where(key_iota == i1r[tt:tt + 1, :], gr[tt:tt + 1, :], zero))
                bt.append(jnp.where(key_iota == i2r[tt:tt + 1, :], one, zero))
            lhs = jnp.concatenate([jnp.concatenate([at[0], zero], axis=1),
                                   jnp.concatenate([zero, at[1]], axis=1)], axis=0)
            out = _nt_dot(lhs, jnp.concatenate(bt, axis=1))
            tok = grp * SUBLANES + 2 * pair
            m_flat[pl.ds(tok, nk, stride=ROUTE_MPITCH), :] = out[:nk, :]
            m_flat[pl.ds(tok + 1, nk, stride=ROUTE_MPITCH), :] = out[nk:, :]

        return [functools.partial(pair_task, grp, pair)
                for grp in range(ROUTE_MTOK // SUBLANES) for pair in range(SUBLANES // 2)]

    def side(hs, between_rounds):
        sk = sk_ref[hs].astype(BF16)
        for j in range(nlt):
            sc = _nt_dot(sk, q_scr[hs, j * LANES:(j + 1) * LANES, :])
            x_scr[pl.ds(j, nk, stride=nlt), :] = sc
        prev = None
        for r in range(PEER_TOPK):
            load = lambda n: (x_scr[tile(n), :], jnp.full((SUBLANES, LANES), n, I32))
            if prev is None:
                knock = lambda n, v, t: v
                store = lambda n, v: None
            else:
                knock = lambda n, v, t, prev=prev: jnp.where(prev == n, neg, v)
                store = lambda n, v: x_scr.__setitem__((tile(n), slice(None)), v)
            bv, bi = _scan_max(nk, load, knock, store)
            ts_scr[hs, r] = bv
            ti_scr[hs, r] = bi
            prev = bi
            between_rounds()

    sides_per_step = 2 * PEER_HEADS // ROUTE_STEPS

    @pl.when(live)
    def _():
        tasks = gate_tasks()
        per_round = -(-len(tasks) // (sides_per_step * PEER_TOPK))

        def between_rounds():
            for _ in range(min(per_round, len(tasks))):
                tasks.pop(0)()

        for k in range(sides_per_step):
            side(step * sides_per_step + k, between_rounds)
        while tasks:
            tasks.pop(0)()

    @pl.when(blk == n_blocks)
    def _():
        for task in gate_tasks():
            task()

    def head(h, carry):
        for p, (j1, j2) in enumerate(_CAND_PAIRS):
            c_scr[tile(p), :] = ts_scr[2 * h, j1] + ts_scr[2 * h + 1, j2]
            k_scr[tile(p), :] = ((j1 * PEER_TOPK + j2) * (nk * nk) + ti_scr[2 * h, j1] * nk
                                 + ti_scr[2 * h + 1, j2])
        prev = None
        vals, exps = [], []
        for r in range(PEER_TOPK):
            load = lambda p: (c_scr[tile(p), :], k_scr[tile(p), :])
            if prev is None:
                knock = lambda p, v, t: v
                store = lambda p, v: None
            else:
                knock = lambda p, v, t, prev=prev: jnp.where(t == prev, neg, v)
                store = lambda p, v: c_scr.__setitem__((tile(p), slice(None)), v)
            bv, bk = _scan_max(len(_CAND_PAIRS), load, knock, store)
            vals.append(bv)
            exps.append(bk & (nk * nk - 1))
            prev = bk
        ps = [jnp.exp(v - vals[0]) for v in vals]
        z = _tree(jnp.add, ps)
        for r in range(PEER_TOPK):
            row = pl.ds(pl.multiple_of((h * PEER_TOPK + r) * SUBLANES, SUBLANES), SUBLANES)
            go_scr[row, :] = ps[r] / z
            eo_scr[row, :] = exps[r]
        return carry

    @pl.when(jnp.logical_and(live, step == ROUTE_STEPS - 1))
    def _():
        lax.fori_loop(0, PEER_HEADS, head, 0)
        nsel = PEER_HEADS * PEER_TOPK
        for j in range(nlt):
            rows = slice(j * LANES, (j + 1) * LANES)
            pe_scr[rows, :] = jnp.transpose(eo_scr[pl.ds(j, nsel, stride=nlt), :])
            pg_scr[rows, :] = jnp.transpose(go_scr[pl.ds(j, nsel, stride=nlt), :])


def _route(h2, layer, wq, subkeys):
    n = h2.shape[0]
    tb = ROUTE_TB
    n_blocks = n // tb
    nsel = PEER_HEADS * PEER_TOPK
    kd = PEER_DKEY // 2
    nk = PEER_NKEYS
    tile_rows = lambda count: count * SUBLANES
    gates = pl.pallas_call(
        functools.partial(_route_kernel, n_blocks=n_blocks),
        grid=(n_blocks + 1, ROUTE_STEPS),
        in_specs=[
            pl.BlockSpec((tb, D_MODEL), lambda i, s: (jnp.minimum(i, n_blocks - 1), 0)),
            pl.BlockSpec((None, D_MODEL, PEER_HEADS * PEER_DKEY), lambda i, s: (layer, 0, 0)),
            pl.BlockSpec((None, 2 * PEER_HEADS, PEER_NKEYS, kd), lambda i, s: (layer, 0, 0, 0)),
        ],
        out_specs=pl.BlockSpec((nk, ROUTE_MPITCH, nk),
                               lambda i, s: (0, jnp.maximum((i - 1) * ROUTE_STEPS + s, 0), 0)),
        out_shape=jax.ShapeDtypeStruct((nk, n // ROUTE_MTOK * ROUTE_MPITCH, nk), F32),
        scratch_shapes=[
            pltpu.VMEM((2 * PEER_HEADS, tb, kd), BF16),
            pltpu.VMEM((tile_rows(PEER_NKEYS), LANES), F32),
            pltpu.VMEM((2 * PEER_HEADS, PEER_TOPK, SUBLANES, LANES), F32),
            pltpu.VMEM((2 * PEER_HEADS, PEER_TOPK, SUBLANES, LANES), I32),
            pltpu.VMEM((tile_rows(len(_CAND_PAIRS)), LANES), F32),
            pltpu.VMEM((tile_rows(len(_CAND_PAIRS)), LANES), I32),
            pltpu.VMEM((tile_rows(nsel), LANES), I32),
            pltpu.VMEM((tile_rows(nsel), LANES), F32),
            pltpu.VMEM((tb, nsel), I32),
            pltpu.VMEM((tb, nsel), F32),
        ],
        compiler_params=_params("arbitrary", "arbitrary"),
        name="route",
    )(h2, wq, subkeys)
    return gates


def _peer_kernel(x_ref, h_ref, m_ref, mod_ref, fg_ref, u_ref, v_ref, o_ref, p0_ref, p1_ref, acc_ref,
                 *, final, n_chunks):
    c = pl.program_id(1)

    def step(prev_ref, cur_ref):
        n_sub = PEER_CH // PEER_SUB
        nw = D_MODEL // n_sub
        for sub in range(n_sub):
            if cur_ref is not None:
                rows = slice(sub * PEER_SUB, (sub + 1) * PEER_SUB)
                s = _gelu(_nt_dot(h_ref[...], u_ref[rows, :]))
                per = PEER_SUB // PEER_NKEYS
                m = jnp.concatenate(
                    [jnp.concatenate([m_ref[sub * per + a, g * ROUTE_MPITCH:g * ROUTE_MPITCH + ROUTE_MTOK, :]
                                      for g in range(h_ref.shape[0] // ROUTE_MTOK)], axis=0)
                     for a in range(per)], axis=1)
                cur_ref[:, rows] = (m * s).astype(BF16)
            if prev_ref is not None:
                cols = slice(sub * nw, (sub + 1) * nw)
                acc_ref[:, cols] += jnp.dot(prev_ref[...], v_ref[:, cols], preferred_element_type=F32)

    stage = (p0_ref, p1_ref)
    steady = jnp.logical_and(c > 0, c < n_chunks)

    @pl.when(c == 0)
    def _():
        acc_ref[...] = jnp.zeros_like(acc_ref)
        step(None, stage[0])

    for parity in range(2):
        @pl.when(jnp.logical_and(steady, c % 2 == parity))
        def _(parity=parity):
            step(stage[1 - parity], stage[parity])

    @pl.when(c == n_chunks)
    def _():
        step(stage[(n_chunks - 1) % 2], None)
        gt2 = mod_ref[:, 5 * D_MODEL:6 * D_MODEL]
        xn = x_ref[...] + gt2 * acc_ref[...]
        o_ref[...] = _rms(xn, fg_ref[...]) if final else xn


def _peer(x, h2, gates, mod4, row_fn, layer, final_g, u_tab, v_tab, final):
    n = x.shape[0]
    tb = PEER_TB
    n_chunks = u_tab.shape[1] // PEER_CH
    return pl.pallas_call(
        functools.partial(_peer_kernel, final=final, n_chunks=n_chunks),
        grid=(n // tb, n_chunks + 1),
        in_specs=[
            pl.BlockSpec((tb, D_MODEL), lambda i, c: (i, 0)),
            pl.BlockSpec((tb, D_MODEL), lambda i, c: (i, 0)),
            pl.BlockSpec((PEER_CH // PEER_NKEYS, tb // ROUTE_MTOK * ROUTE_MPITCH, PEER_NKEYS),
                         lambda i, c: (jnp.minimum(c, n_chunks - 1), i, 0)),
            pl.BlockSpec((None, None, 1, 6 * D_MODEL), lambda i, c: (layer, row_fn(i * tb), 0, 0)),
            pl.BlockSpec((1, D_MODEL), lambda i, c: (0, 0)),
            pl.BlockSpec((None, PEER_CH, D_MODEL), lambda i, c: (layer, jnp.minimum(c, n_chunks - 1), 0)),
            pl.BlockSpec((None, PEER_CH, D_MODEL), lambda i, c: (layer, jnp.maximum(c - 1, 0), 0)),
        ],
        out_specs=pl.BlockSpec((tb, D_MODEL), lambda i, c: (i, 0)),
        out_shape=jax.ShapeDtypeStruct((n, D_MODEL), F32),
        scratch_shapes=[pltpu.VMEM((tb, PEER_CH), BF16),
                        pltpu.VMEM((tb, PEER_CH), BF16),
                        pltpu.VMEM((tb, D_MODEL), F32)],
        compiler_params=_params("parallel", "arbitrary"),
        name="peer",
    )(x, h2, gates, mod4, final_g, u_tab, v_tab)


def kernel(x_prompt, x_sample, state_ret, c, c_ctx, w_mod, b_mod, norm_g, w_in, w_out, gmlp_ws, gmlp_b,
           gmlp_ln_g, pool_w, pool_scale, ret_decay, peer_wq, peer_subkeys, peer_u, peer_v, final_norm_g):
    batch, seq, d = x_prompt.shape
    dec_batch, dec_seq, _ = x_sample.shape
    depth = w_mod.shape[0]
    assert d == D_MODEL and dec_batch + 1 <= MOD_ROWS and dec_seq % GRID_W == 0

    cond = jnp.zeros((MOD_ROWS, d), F32).at[0].set(c_ctx).at[1:1 + dec_batch].set(c)
    mod = _modulation(cond, w_mod, b_mod)
    mod4 = mod.reshape(depth, MOD_ROWS, 1, 6 * d)
    ctx_row = lambda tok0: 0
    lat_row = lambda tok0: 1 + tok0 // dec_seq

    xc = x_prompt.reshape(batch * seq, d)
    xs = x_sample.reshape(dec_batch * dec_seq, d)
    fg = final_norm_g.reshape(1, d)
    norm_g4 = norm_g.reshape(depth, 2, 1, d)
    bs_t = jnp.swapaxes(gmlp_b, 1, 2)
    ln_g = gmlp_ln_g.reshape(depth, 1, d)
    ps = pool_scale.reshape(depth, 1, d)
    sk = peer_subkeys.reshape(depth, 2 * PEER_HEADS, PEER_NKEYS, PEER_DKEY // 2)
    u_tab = peer_u.astype(BF16)
    v_tab = peer_v.astype(BF16)
    new_states = []
    for l in range(depth):
        last = l == depth - 1

        def path(x, nseq, n, row_fn, grid, s0):
            proj = _proj(x, mod4, row_fn, l, norm_g4, w_in)
            ya = _gmlp(proj, l, gmlp_ws, bs_t, ln_g)
            yb = _pool(proj, l, pool_w, ps, nseq, n, grid)
            ret = _retention(proj, ret_decay, nseq, n, s0, l)
            yc, s_fin = (ret, None) if s0 is not None else ret
            xn, h2 = _merge(x, proj, ya, yb, yc, mod4, row_fn, l, norm_g4, w_out)
            gates = _route(h2, l, peer_wq, sk)
            return _peer(xn, h2, gates, mod4, row_fn, l, fg, u_tab, v_tab, last), s_fin

        xc, s_fin = path(xc, batch, seq, ctx_row, False, None)
        new_states.append(s_fin)
        xs, _ = path(xs, dec_batch, dec_seq, lat_row, True, state_ret)

    new_state_ret = jnp.stack(new_states, axis=1).astype(x_prompt.dtype)
    return (xc.reshape(batch, seq, d), xs.reshape(dec_batch, dec_seq, d), new_state_ret)
```

```python
import functools

import numpy as np
import jax
import jax.numpy as jnp
from jax import lax
from jax.experimental import pallas as pl
from jax.experimental.pallas import tpu as pltpu

F32 = jnp.float32
BF16 = jnp.bfloat16
I32 = jnp.int32

D_MODEL = 1024
A_GROUPS = 4
A_CHUNK = 128
POOL_WINDOWS = (2, 4, 8, 16)
B_GW = D_MODEL // len(POOL_WINDOWS)
GRID_W = 64
RET_HEADS = 4
RET_DV = D_MODEL // RET_HEADS
RET_DK = RET_DV // 2
RET_CHUNK = 128
PEER_HEADS = 8
PEER_NKEYS = 128
PEER_DKEY = 256
PEER_TOPK = 16
EPS = 1e-6
IN_COLS = 10 * D_MODEL

SUBLANES = 8
LANES = 128
VMEM_LIMIT = 56 * 1024 * 1024

MOD_ROWS = 16
PROJ_TB = 1024
PROJ_TN = 2048
PROJ_DTYPE = BF16
Y_DTYPE = BF16
GMLP_TB = 512
MERGE_TB = 1024
ROUTE_TB = SUBLANES * LANES
ROUTE_STEPS = 8
ROUTE_MTOK = ROUTE_TB // ROUTE_STEPS
ROUTE_MPITCH = ROUTE_MTOK + SUBLANES
PEER_TB = 1024
PEER_CH = 1024
PEER_SUB = 256


def _gelu(x):
    return 0.5 * x * (1.0 + jnp.tanh(0.7978845608028654 * (x + 0.044715 * (x * x * x))))


def _sigmoid(x):
    return 0.5 + 0.5 * jnp.tanh(0.5 * x)


def _rms(x, g):
    return x * lax.rsqrt(jnp.mean(x * x, axis=-1, keepdims=True) + EPS) * g


def _params(*sem):
    return pltpu.CompilerParams(dimension_semantics=sem, vmem_limit_bytes=VMEM_LIMIT)


def _tree(fn, xs):
    xs = list(xs)
    while len(xs) > 1:
        xs = [fn(xs[i], xs[i + 1]) for i in range(0, len(xs) - 1, 2)] + ([xs[-1]] if len(xs) % 2 else [])
    return xs[0]


def _nt_dot(a, b):
    return lax.dot_general(a, b, (((1,), (1,)), ((), ())), preferred_element_type=F32)


def _mod_kernel(cond_ref, w_ref, b_ref, o_ref):
    c = cond_ref[...]
    s = (c * _sigmoid(c)).astype(BF16)
    o_ref[...] = jnp.dot(s, w_ref[...].astype(BF16), preferred_element_type=F32) + b_ref[...]


def _modulation(cond, w_mod, b_mod):
    depth = w_mod.shape[0]
    nj = w_mod.shape[2] // D_MODEL
    return pl.pallas_call(
        _mod_kernel,
        grid=(depth, nj),
        in_specs=[
            pl.BlockSpec((MOD_ROWS, D_MODEL), lambda l, j: (0, 0)),
            pl.BlockSpec((None, D_MODEL, D_MODEL), lambda l, j: (l, 0, j)),
            pl.BlockSpec((None, 1, D_MODEL), lambda l, j: (l, 0, j)),
        ],
        out_specs=pl.BlockSpec((None, MOD_ROWS, D_MODEL), lambda l, j: (l, 0, j)),
        out_shape=jax.ShapeDtypeStruct((depth, MOD_ROWS, nj * D_MODEL), F32),
        compiler_params=_params("parallel", "parallel"),
        name="mod",
    )(cond, w_mod, b_mod.reshape(depth, 1, nj * D_MODEL))


def _proj_kernel(x_ref, mod_ref, g_ref, w_ref, o_ref, wb_ref):
    @pl.when(pl.program_id(1) == 0)
    def _():
        wb_ref[...] = w_ref[...].astype(BF16)

    y = _rms(x_ref[...], g_ref[...])
    sh = mod_ref[:, 0:D_MODEL]
    sc = mod_ref[:, D_MODEL:2 * D_MODEL]
    h = (y * (1.0 + sc) + sh).astype(BF16)
    o_ref[...] = jnp.dot(h, wb_ref[...], preferred_element_type=F32).astype(o_ref.dtype)


def _proj(x, mod4, row_fn, layer, norm_g, w_in):
    n = x.shape[0]
    tb, tn = PROJ_TB, PROJ_TN
    return pl.pallas_call(
        _proj_kernel,
        grid=(IN_COLS // tn, n // tb),
        in_specs=[
            pl.BlockSpec((tb, D_MODEL), lambda j, i: (i, 0)),
            pl.BlockSpec((None, None, 1, 6 * D_MODEL), lambda j, i: (layer, row_fn(i * tb), 0, 0)),
            pl.BlockSpec((None, None, 1, D_MODEL), lambda j, i: (layer, 0, 0, 0)),
            pl.BlockSpec((None, D_MODEL, tn), lambda j, i: (layer, 0, j)),
        ],
        out_specs=pl.BlockSpec((tb, tn), lambda j, i: (i, j)),
        out_shape=jax.ShapeDtypeStruct((n, IN_COLS), PROJ_DTYPE),
        scratch_shapes=[pltpu.VMEM((D_MODEL, tn), BF16)],
        compiler_params=_params("parallel", "arbitrary"),
        name="proj",
    )(x, mod4, norm_g, w_in)


def _gmlp_kernel(u_ref, v_ref, ws_ref, bs_ref, lng_ref, o_ref):
    gw = D_MODEL // A_GROUPS
    for c in range(GMLP_TB // A_CHUNK):
        rows = slice(c * A_CHUNK, (c + 1) * A_CHUNK)
        v = _gelu(v_ref[rows, :].astype(F32))
        vc = v - jnp.mean(v, axis=-1, keepdims=True)
        vn = vc * lax.rsqrt(jnp.mean(vc * vc, axis=-1, keepdims=True) + EPS) * lng_ref[...]
        vnb = vn.astype(BF16)
        for g in range(A_GROUPS):
            cols = slice(g * gw, (g + 1) * gw)
            sv = jnp.dot(ws_ref[g].astype(BF16), vnb[:, cols], preferred_element_type=F32) + bs_ref[:, g:g + 1]
            o_ref[rows, cols] = (_gelu(u_ref[rows, cols].astype(F32)) * sv).astype(o_ref.dtype)


def _gmlp(proj, layer, ws, bs_t, ln_g):
    n = proj.shape[0]
    tb = GMLP_TB
    return pl.pallas_call(
        _gmlp_kernel,
        grid=(n // tb,),
        in_specs=[
            pl.BlockSpec((tb, D_MODEL), lambda i: (i, 0)),
            pl.BlockSpec((tb, D_MODEL), lambda i: (i, 1)),
            pl.BlockSpec((None, A_GROUPS, A_CHUNK, A_CHUNK), lambda i: (layer, 0, 0, 0)),
            pl.BlockSpec((None, A_CHUNK, A_GROUPS), lambda i: (layer, 0, 0)),
            pl.BlockSpec((None, 1, D_MODEL), lambda i: (layer, 0, 0)),
        ],
        out_specs=pl.BlockSpec((tb, D_MODEL), lambda i: (i, 0)),
        out_shape=jax.ShapeDtypeStruct((n, D_MODEL), Y_DTYPE),
        compiler_params=_params("parallel"),
        name="gmlp",
    )(proj, proj, ws, bs_t, ln_g)


POOL_PAD = 16


def _window_count(pos, size, a, b):
    return jnp.minimum(pos + b, size - 1) - jnp.maximum(pos - a, 0) + 1


def _pool_kernel(z_ref, w_ref, scale_ref, o_ref, zp_ref, cp_ref, *, n, grid):
    rows = n // GRID_W
    rpad = cp_ref.shape[0] - n
    tok = lax.broadcasted_iota(I32, (n, 1), 0)
    for gi, w in enumerate(POOL_WINDOWS):
        a = w // 2
        b = w - 1 - a
        cols = slice(gi * B_GW, (gi + 1) * B_GW)
        z = z_ref[:, cols].astype(F32)
        zp_ref[0:POOL_PAD, :] = jnp.zeros((POOL_PAD, B_GW), F32)
        zp_ref[POOL_PAD + n:, :] = jnp.zeros((POOL_PAD, B_GW), F32)
        zp_ref[POOL_PAD:POOL_PAD + n, :] = z
        if grid:
            col = tok & (GRID_W - 1)
            row = tok >> (GRID_W.bit_length() - 1)
            s = jnp.zeros((n, B_GW), F32)
            for d in range(-a, b + 1):
                sh = zp_ref[POOL_PAD + d:POOL_PAD + d + n, :]
                ok = jnp.logical_and(col + d >= 0, col + d < GRID_W)
                s = s + jnp.where(ok, sh, 0.0)
            half = rpad // 2
            cp_ref[0:half, :] = jnp.zeros((half, B_GW), F32)
            cp_ref[half + n:, :] = jnp.zeros((half, B_GW), F32)
            cp_ref[half:half + n, :] = s
            s = jnp.zeros((n, B_GW), F32)
            for d in range(-a, b + 1):
                s = s + cp_ref[half + d * GRID_W:half + d * GRID_W + n, :]
            cnt = (_window_count(row, rows, a, b) * _window_count(col, GRID_W, a, b)).astype(F32)
        else:
            s = jnp.zeros((n, B_GW), F32)
            for d in range(-a, b + 1):
                s = s + zp_ref[POOL_PAD + d:POOL_PAD + d + n, :]
            cnt = _window_count(tok, n, a, b).astype(F32)
        diff = (s / cnt - z).astype(BF16)
        o_ref[:, cols] = (jnp.dot(diff, w_ref[gi].astype(BF16), preferred_element_type=F32)
                          * scale_ref[:, cols]).astype(o_ref.dtype)


def _pool(proj, layer, pool_w, pool_scale, nseq, n, grid):
    rpad = 2 * (max(POOL_WINDOWS) // 2) * GRID_W if grid else 2 * SUBLANES
    return pl.pallas_call(
        functools.partial(_pool_kernel, n=n, grid=grid),
        grid=(nseq,),
        in_specs=[
            pl.BlockSpec((n, D_MODEL), lambda i: (i, 2)),
            pl.BlockSpec((None, len(POOL_WINDOWS), B_GW, B_GW), lambda i: (layer, 0, 0, 0)),
            pl.BlockSpec((None, 1, D_MODEL), lambda i: (layer, 0, 0)),
        ],
        out_specs=pl.BlockSpec((n, D_MODEL), lambda i: (i, 0)),
        out_shape=jax.ShapeDtypeStruct((nseq * n, D_MODEL), Y_DTYPE),
        scratch_shapes=[pltpu.VMEM((n + 2 * POOL_PAD, B_GW), F32), pltpu.VMEM((n + rpad, B_GW), F32)],
        compiler_params=_params("parallel"),
        name="pool",
    )(proj, pool_w, pool_scale)


def _log_sigmoid(x):
    return jnp.minimum(x, 0.0) - jnp.log(1.0 + jnp.exp(-jnp.abs(x)))


def _ret_kernel(*refs, n, has_s0, has_prev_states):
    if has_s0:
        q_ref, k_ref, v_ref, gf_ref, gb_ref, rd_ref, s0_ref, y_ref, s_ref, dm_ref, yf_ref, dec_ref = refs
        sfin_ref = None
    else:
        if has_prev_states:
            refs = refs[:6] + refs[7:]
        q_ref, k_ref, v_ref, gf_ref, gb_ref, rd_ref, y_ref, sfin_ref, s_ref, dm_ref, yf_ref, dec_ref = refs
        s0_ref = None
    cl = RET_CHUNK
    nc = n // cl
    pi = lax.broadcasted_iota(I32, (cl, cl), 0).astype(F32)
    pj = lax.broadcasted_iota(I32, (cl, cl), 1).astype(F32)
    prow = lax.broadcasted_iota(I32, (cl, RET_DK), 0).astype(F32)
    kscale = RET_DK ** -0.5

    for d in range(2):
        g_ref = gf_ref if d == 0 else gb_ref
        lgs = []
        for h in range(RET_HEADS):
            lg = _log_sigmoid(rd_ref[d:d + 1, h:h + 1])
            lgs.append(lg)
            rel = (pi - pj) if d == 0 else (pj - pi)
            dm_ref[h] = jnp.where(rel >= 0.0, jnp.exp(lg * jnp.maximum(rel, 0.0)), 0.0)
            dec_ref[h] = jnp.exp(lg * ((prow + 1.0) if d == 0 else (cl - prow)))
            dec_ref[RET_HEADS + h] = jnp.exp(lg * ((cl - 1.0 - prow) if d == 0 else prow))
            if has_s0:
                s_ref[h] = s0_ref[d, h]
            else:
                s_ref[h] = jnp.zeros((RET_DK, RET_DV), F32)

        def chunk(ci, carry, d=d, g_ref=g_ref, lgs=lgs):
            c = ci if d == 0 else nc - 1 - ci
            rows = pl.ds(pl.multiple_of(c * cl, cl), cl)
            for h in range(RET_HEADS):
                qdec = dec_ref[h]
                kdec = dec_ref[RET_HEADS + h]
                cdec = jnp.exp(lgs[h] * float(cl))
                qb = q_ref[rows, h * RET_DK:(h + 1) * RET_DK].astype(BF16)
                q = qb.astype(F32)
                k = k_ref[rows, h * RET_DK:(h + 1) * RET_DK].astype(F32) * kscale
                vb = v_ref[rows, h * RET_DV:(h + 1) * RET_DV].astype(BF16)
                sc = _nt_dot(qb, k.astype(BF16)) * dm_ref[h]
                o = jnp.dot(sc.astype(BF16), vb, preferred_element_type=F32)
                s_prev = s_ref[h]
                o = o + jnp.dot((q * qdec).astype(BF16), s_prev.astype(BF16), preferred_element_type=F32)
                kd_t = jnp.transpose(k * kdec).astype(BF16)
                s_ref[h] = cdec * s_prev + jnp.dot(kd_t, vb, preferred_element_type=F32)
                on = o * lax.rsqrt(jnp.mean(o * o, axis=-1, keepdims=True) + EPS)
                g = g_ref[rows, h * RET_DV:(h + 1) * RET_DV].astype(F32)
                yv = g * _sigmoid(g) * on
                cols = slice(h * RET_DV, (h + 1) * RET_DV)
                if d == 0:
                    yf_ref[rows, cols] = yv
                else:
                    y_ref[rows, cols] = (yf_ref[rows, cols] + yv).astype(y_ref.dtype)
            return carry

        lax.fori_loop(0, nc, chunk, 0)
        if not has_s0:
            for h in range(RET_HEADS):
                sfin_ref[d, h] = s_ref[h]


def _retention(proj, ret_decay, nseq, n, s0, layer, depth=None, states=None):
    has_s0 = s0 is not None
    aliases = {}
    dkb = RET_HEADS * RET_DK
    in_specs = [
        pl.BlockSpec((n, dkb), lambda i: (i, 3 * D_MODEL // dkb)),
        pl.BlockSpec((n, dkb), lambda i: (i, 3 * D_MODEL // dkb + 1)),
        pl.BlockSpec((n, D_MODEL), lambda i: (i, 4)),
        pl.BlockSpec((n, D_MODEL), lambda i: (i, 5)),
        pl.BlockSpec((n, D_MODEL), lambda i: (i, 6)),
        pl.BlockSpec((None, 2, RET_HEADS), lambda i: (layer, 0, 0)),
    ]
    args = [proj, proj, proj, proj, proj, ret_decay]
    y_shape = jax.ShapeDtypeStruct((nseq * n, D_MODEL), Y_DTYPE)
    y_spec = pl.BlockSpec((n, D_MODEL), lambda i: (i, 0))
    if has_s0:
        in_specs.append(pl.BlockSpec((None, None, 2, RET_HEADS, RET_DK, RET_DV), lambda i: (i, layer, 0, 0, 0, 0)))
        args.append(s0)
        out_shape, out_specs = y_shape, y_spec
    else:
        state_shape = (nseq, depth, 2, RET_HEADS, RET_DK, RET_DV)
        out_shape = (y_shape, jax.ShapeDtypeStruct(state_shape, F32))
        out_specs = (y_spec, pl.BlockSpec((None, None, 2, RET_HEADS, RET_DK, RET_DV),
                                          lambda i: (i, layer, 0, 0, 0, 0)))
        if states is not None:
            in_specs.append(pl.BlockSpec(memory_space=pl.ANY))
            args.append(states)
            aliases = {len(args) - 1: 1}
    return pl.pallas_call(
        functools.partial(_ret_kernel, n=n, has_s0=has_s0, has_prev_states=states is not None),
        grid=(nseq,),
        in_specs=in_specs,
        out_specs=out_specs,
        out_shape=out_shape,
        input_output_aliases=aliases,
        scratch_shapes=[pltpu.VMEM((RET_HEADS, RET_DK, RET_DV), F32),
                        pltpu.VMEM((RET_HEADS, RET_CHUNK, RET_CHUNK), F32),
                        pltpu.VMEM((n, D_MODEL), F32),
                        pltpu.VMEM((2 * RET_HEADS, RET_CHUNK, RET_DK), F32)],
        compiler_params=_params("parallel"),
        name="ret",
    )(*args)


def _merge_kernel(x_ref, ga_ref, gb_ref, gc_ref, ya_ref, yb_ref, yc_ref, mod_ref, g2_ref, w_ref, xo_ref, h2_ref):
    merged = (_sigmoid(ga_ref[...].astype(F32)) * ya_ref[...] + _sigmoid(gb_ref[...].astype(F32)) * yb_ref[...]
              + _sigmoid(gc_ref[...].astype(F32)) * yc_ref[...])
    o = jnp.dot(merged.astype(BF16), w_ref[...].astype(BF16), preferred_element_type=F32)
    gt1 = mod_ref[:, 2 * D_MODEL:3 * D_MODEL]
    sh2 = mod_ref[:, 3 * D_MODEL:4 * D_MODEL]
    sc2 = mod_ref[:, 4 * D_MODEL:5 * D_MODEL]
    xn = x_ref[...] + gt1 * o
    xo_ref[...] = xn
    h2_ref[...] = (_rms(xn, g2_ref[...]) * (1.0 + sc2) + sh2).astype(BF16)


def _merge(x, proj, ya, yb, yc, mod4, row_fn, layer, norm_g2, w_out):
    n = x.shape[0]
    tb = MERGE_TB
    tok = lambda c: pl.BlockSpec((tb, D_MODEL), lambda i: (i, c))
    return pl.pallas_call(
        _merge_kernel,
        grid=(n // tb,),
        in_specs=[
            tok(0), tok(7), tok(8), tok(9), tok(0), tok(0), tok(0),
            pl.BlockSpec((None, None, 1, 6 * D_MODEL), lambda i: (layer, row_fn(i * tb), 0, 0)),
            pl.BlockSpec((None, None, 1, D_MODEL), lambda i: (layer, 1, 0, 0)),
            pl.BlockSpec((None, D_MODEL, D_MODEL), lambda i: (layer, 0, 0)),
        ],
        out_specs=(tok(0), tok(0)),
        out_shape=(jax.ShapeDtypeStruct((n, D_MODEL), F32), jax.ShapeDtypeStruct((n, D_MODEL), BF16)),
        compiler_params=_params("parallel"),
        name="merge",
    )(x, proj, proj, proj, ya, yb, yc, mod4, norm_g2, w_out)


_CAND_PAIRS = tuple((j1, j2) for j1 in range(PEER_TOPK) for j2 in range(PEER_TOPK)
                    if (j1 + 1) * (j2 + 1) <= PEER_TOPK)
ROUTE_ACCS = 4


def _scan_max(n_items, load, knock, store):
    per = -(-n_items // ROUTE_ACCS)
    accs = []
    for a0 in range(0, n_items, per):
        bv = bt = None
        for i in range(a0, min(a0 + per, n_items)):
            v, t = load(i)
            v = knock(i, v, t)
            store(i, v)
            if bv is None:
                bv, bt = v, t
            else:
                gt = v > bv
                bv = jnp.where(gt, v, bv)
                bt = jnp.where(gt, t, bt)
        accs.append((bv, bt))
    bv, bt = accs[0]
    for v, t in accs[1:]:
        gt = v > bv
        bv = jnp.where(gt, v, bv)
        bt = jnp.where(gt, t, bt)
    return bv, bt


def _route_kernel(h_ref, wq_ref, sk_ref, m_ref, q_scr, x_scr, ts_scr, ti_scr, c_scr, k_scr, eo_scr, go_scr,
                  pe_scr, pg_scr, *, n_blocks):
    tb = h_ref.shape[0]
    nk = PEER_NKEYS
    kd = PEER_DKEY // 2
    nlt = tb // LANES
    assert nlt == SUBLANES
    neg = -jnp.inf
    tile = lambda i: pl.ds(i * SUBLANES, SUBLANES)
    blk = pl.program_id(0)
    step = pl.program_id(1)
    live = blk < n_blocks

    @pl.when(jnp.logical_and(blk == 0, step == 0))
    def _():
        pe_scr[...] = jnp.zeros_like(pe_scr)
        pg_scr[...] = jnp.zeros_like(pg_scr)

    @pl.when(jnp.logical_and(live, step == 0))
    def _():
        for hd in range(PEER_HEADS):
            cols = slice(hd * PEER_DKEY, (hd + 1) * PEER_DKEY)
            q = jnp.dot(h_ref[...], wq_ref[:, cols].astype(BF16), preferred_element_type=F32).astype(BF16)
            q_scr[2 * hd] = q[:, :kd]
            q_scr[2 * hd + 1] = q[:, kd:]

    def gate_tasks():
        t0 = step * ROUTE_MTOK
        key_iota = lax.broadcasted_iota(I32, (nk, nk), 0).astype(BF16)
        zero = jnp.zeros((nk, nk), BF16)
        one = jnp.ones((nk, nk), BF16)
        m_flat = m_ref
        for spare in range(ROUTE_MTOK, ROUTE_MPITCH):
            m_flat[pl.ds(spare, nk, stride=ROUTE_MPITCH), :] = jnp.zeros((nk, nk), F32)
        group_rows = {}

        def pair_task(grp, pair):
            if pair == 0:
                rows = pl.ds(pl.multiple_of(t0 + grp * SUBLANES, SUBLANES), SUBLANES)
                er = pe_scr[rows, :]
                group_rows[grp] = ((er >> 7).astype(F32).astype(BF16), (er & (nk - 1)).astype(F32).astype(BF16),
                                   pg_scr[rows, :].astype(BF16))
            i1r, i2r, gr = group_rows[grp]
            at, bt = [], []
            for tt in (2 * pair, 2 * pair + 1):
                at.append(jnp.where(key_iota == i1r[tt:tt + 1, :], gr[tt:tt + 1, :], zero))
                bt.append(jnp.where(key_iota == i2r[tt:tt + 1, :], one, zero))
            lhs = jnp.concatenate([jnp.concatenate([at[0], zero], axis=1),
                                   jnp.concatenate([zero, at[1]], axis=1)], axis=0)
            out = _nt_dot(lhs, jnp.concatenate(bt, axis=1))
            tok = grp * SUBLANES + 2 * pair
            m_flat[pl.ds(tok, nk, stride=ROUTE_MPITCH), :] = out[:nk, :]
            m_flat[pl.ds(tok + 1, nk, stride=ROUTE_MPITCH), :] = out[nk:, :]

        return [functools.partial(pair_task, grp, pair)
                for grp in range(ROUTE_MTOK // SUBLANES) for pair in range(SUBLANES // 2)]

    def side(hs, between_rounds):
        sk = sk_ref[hs].astype(BF16)
        for j in range(nlt):
            sc = _nt_dot(sk, q_scr[hs, j * LANES:(j + 1) * LANES, :])
            x_scr[pl.ds(j, nk, stride=nlt), :] = sc
        prev = None
        for r in range(PEER_TOPK):
            load = lambda n: (x_scr[tile(n), :], jnp.full((SUBLANES, LANES), n, I32))
            if prev is None:
                knock = lambda n, v, t: v
                store = lambda n, v: None
            else:
                knock = lambda n, v, t, prev=prev: jnp.where(prev == n, neg, v)
                store = lambda n, v: x_scr.__setitem__((tile(n), slice(None)), v)
            bv, bi = _scan_max(nk, load, knock, store)
            ts_scr[hs, r] = bv
            ti_scr[hs, r] = bi
            prev = bi
            between_rounds()

    sides_per_step = 2 * PEER_HEADS // ROUTE_STEPS

    @pl.when(live)
    def _():
        tasks = gate_tasks()
        per_round = -(-len(tasks) // (sides_per_step * PEER_TOPK))

        def between_rounds():
            for _ in range(min(per_round, len(tasks))):
                tasks.pop(0)()

        for k in range(sides_per_step):
            side(step * sides_per_step + k, between_rounds)
        while tasks:
            tasks.pop(0)()

    @pl.when(blk == n_blocks)
    def _():
        for task in gate_tasks():
            task()

    def head(h, carry):
        for p, (j1, j2) in enumerate(_CAND_PAIRS):
            c_scr[tile(p), :] = ts_scr[2 * h, j1] + ts_scr[2 * h + 1, j2]
            k_scr[tile(p), :] = ((j1 * PEER_TOPK + j2) * (nk * nk) + ti_scr[2 * h, j1] * nk
                                 + ti_scr[2 * h + 1, j2])
        prev = None
        vals, exps = [], []
        for r in range(PEER_TOPK):
            load = lambda p: (c_scr[tile(p), :], k_scr[tile(p), :])
            if prev is None:
                knock = lambda p, v, t: v
                store = lambda p, v: None
            else:
                knock = lambda p, v, t, prev=prev: jnp.where(t == prev, neg, v)
                store = lambda p, v: c_scr.__setitem__((tile(p), slice(None)), v)
            bv, bk = _scan_max(len(_CAND_PAIRS), load, knock, store)
            vals.append(bv)
            exps.append(bk & (nk * nk - 1))
            prev = bk
        ps = [jnp.exp(v - vals[0]) for v in vals]
        z = _tree(jnp.add, ps)
        for r in range(PEER_TOPK):
            row = pl.ds(pl.multiple_of((h * PEER_TOPK + r) * SUBLANES, SUBLANES), SUBLANES)
            go_scr[row, :] = ps[r] / z
            eo_scr[row, :] = exps[r]
        return carry

    @pl.when(jnp.logical_and(live, step == ROUTE_STEPS - 1))
    def _():
        lax.fori_loop(0, PEER_HEADS, head, 0)
        nsel = PEER_HEADS * PEER_TOPK
        for j in range(nlt):
            rows = slice(j * LANES, (j + 1) * LANES)
            pe_scr[rows, :] = jnp.transpose(eo_scr[pl.ds(j, nsel, stride=nlt), :])
            pg_scr[rows, :] = jnp.transpose(go_scr[pl.ds(j, nsel, stride=nlt), :])


def _route(h2, layer, wq, subkeys):
    n = h2.shape[0]
    tb = ROUTE_TB
    n_blocks = n // tb
    nsel = PEER_HEADS * PEER_TOPK
    kd = PEER_DKEY // 2
    nk = PEER_NKEYS
    tile_rows = lambda count: count * SUBLANES
    gates = pl.pallas_call(
        functools.partial(_route_kernel, n_blocks=n_blocks),
        grid=(n_blocks + 1, ROUTE_STEPS),
        in_specs=[
            pl.BlockSpec((tb, D_MODEL), lambda i, s: (jnp.minimum(i, n_blocks - 1), 0)),
            pl.BlockSpec((None, D_MODEL, PEER_HEADS * PEER_DKEY), lambda i, s: (layer, 0, 0)),
            pl.BlockSpec((None, 2 * PEER_HEADS, PEER_NKEYS, kd), lambda i, s: (layer, 0, 0, 0)),
        ],
        out_specs=pl.BlockSpec((None, nk * ROUTE_MPITCH, nk),
                               lambda i, s: (jnp.maximum((i - 1) * ROUTE_STEPS + s, 0), 0, 0)),
        out_shape=jax.ShapeDtypeStruct((n // ROUTE_MTOK, nk * ROUTE_MPITCH, nk), F32),
        scratch_shapes=[
            pltpu.VMEM((2 * PEER_HEADS, tb, kd), BF16),
            pltpu.VMEM((tile_rows(PEER_NKEYS), LANES), F32),
            pltpu.VMEM((2 * PEER_HEADS, PEER_TOPK, SUBLANES, LANES), F32),
            pltpu.VMEM((2 * PEER_HEADS, PEER_TOPK, SUBLANES, LANES), I32),
            pltpu.VMEM((tile_rows(len(_CAND_PAIRS)), LANES), F32),
            pltpu.VMEM((tile_rows(len(_CAND_PAIRS)), LANES), I32),
            pltpu.VMEM((tile_rows(nsel), LANES), I32),
            pltpu.VMEM((tile_rows(nsel), LANES), F32),
            pltpu.VMEM((tb, nsel), I32),
            pltpu.VMEM((tb, nsel), F32),
        ],
        compiler_params=_params("arbitrary", "arbitrary"),
        name="route",
    )(h2, wq, subkeys)
    return gates.reshape(n // ROUTE_MTOK, nk, ROUTE_MPITCH, nk)


def _peer_kernel(x_ref, h_ref, m_ref, mod_ref, fg_ref, u_ref, v_ref, o_ref, p0_ref, p1_ref, acc_ref,
                 *, final, n_chunks):
    c = pl.program_id(1)

    def step(prev_ref, cur_ref):
        n_sub = PEER_CH // PEER_SUB
        nw = D_MODEL // n_sub
        for sub in range(n_sub):
            if cur_ref is not None:
                rows = slice(sub * PEER_SUB, (sub + 1) * PEER_SUB)
                s = _gelu(_nt_dot(h_ref[...], u_ref[rows, :]))
                per = PEER_SUB // PEER_NKEYS
                m = jnp.concatenate(
                    [jnp.concatenate([m_ref[g, sub * per + a, :ROUTE_MTOK, :]
                                      for g in range(h_ref.shape[0] // ROUTE_MTOK)], axis=0)
                     for a in range(per)], axis=1)
                cur_ref[:, rows] = (m * s).astype(BF16)
            if prev_ref is not None:
                cols = slice(sub * nw, (sub + 1) * nw)
                acc_ref[:, cols] += jnp.dot(prev_ref[...], v_ref[:, cols], preferred_element_type=F32)

    stage = (p0_ref, p1_ref)
    steady = jnp.logical_and(c > 0, c < n_chunks)

    @pl.when(c == 0)
    def _():
        acc_ref[...] = jnp.zeros_like(acc_ref)
        step(None, stage[0])

    for parity in range(2):
        @pl.when(jnp.logical_and(steady, c % 2 == parity))
        def _(parity=parity):
            step(stage[1 - parity], stage[parity])

    @pl.when(c == n_chunks)
    def _():
        step(stage[(n_chunks - 1) % 2], None)
        gt2 = mod_ref[:, 5 * D_MODEL:6 * D_MODEL]
        xn = x_ref[...] + gt2 * acc_ref[...]
        o_ref[...] = _rms(xn, fg_ref[...]) if final else xn


def _peer(x, h2, gates, mod4, row_fn, layer, final_g, u_tab, v_tab, final):
    n = x.shape[0]
    tb = PEER_TB
    n_chunks = u_tab.shape[1] // PEER_CH
    return pl.pallas_call(
        functools.partial(_peer_kernel, final=final, n_chunks=n_chunks),
        grid=(n // tb, n_chunks + 1),
        in_specs=[
            pl.BlockSpec((tb, D_MODEL), lambda i, c: (i, 0)),
            pl.BlockSpec((tb, D_MODEL), lambda i, c: (i, 0)),
            pl.BlockSpec((tb // ROUTE_MTOK, PEER_CH // PEER_NKEYS, ROUTE_MPITCH, PEER_NKEYS),
                         lambda i, c: (i, jnp.minimum(c, n_chunks - 1), 0, 0)),
            pl.BlockSpec((None, None, 1, 6 * D_MODEL), lambda i, c: (layer, row_fn(i * tb), 0, 0)),
            pl.BlockSpec((1, D_MODEL), lambda i, c: (0, 0)),
            pl.BlockSpec((None, PEER_CH, D_MODEL), lambda i, c: (layer, jnp.minimum(c, n_chunks - 1), 0)),
            pl.BlockSpec((None, PEER_CH, D_MODEL), lambda i, c: (layer, jnp.maximum(c - 1, 0), 0)),
        ],
        out_specs=pl.BlockSpec((tb, D_MODEL), lambda i, c: (i, 0)),
        out_shape=jax.ShapeDtypeStruct((n, D_MODEL), F32),
        scratch_shapes=[pltpu.VMEM((tb, PEER_CH), BF16),
                        pltpu.VMEM((tb, PEER_CH), BF16),
                        pltpu.VMEM((tb, D_MODEL), F32)],
        compiler_params=_params("parallel", "arbitrary"),
        name="peer",
    )(x, h2, gates, mod4, final_g, u_tab, v_tab)


def kernel(x_prompt, x_sample, state_ret, c, c_ctx, w_mod, b_mod, norm_g, w_in, w_out, gmlp_ws, gmlp_b,
           gmlp_ln_g, pool_w, pool_scale, ret_decay, peer_wq, peer_subkeys, peer_u, peer_v, final_norm_g):
    batch, seq, d = x_prompt.shape
    dec_batch, dec_seq, _ = x_sample.shape
    depth = w_mod.shape[0]
    assert d == D_MODEL and dec_batch + 1 <= MOD_ROWS and dec_seq % GRID_W == 0

    cond = jnp.zeros((MOD_ROWS, d), F32).at[0].set(c_ctx).at[1:1 + dec_batch].set(c)
    mod = _modulation(cond, w_mod, b_mod)
    mod4 = mod.reshape(depth, MOD_ROWS, 1, 6 * d)
    ctx_row = lambda tok0: 0
    lat_row = lambda tok0: 1 + tok0 // dec_seq

    xc = x_prompt.reshape(batch * seq, d)
    xs = x_sample.reshape(dec_batch * dec_seq, d)
    fg = final_norm_g.reshape(1, d)
    norm_g4 = norm_g.reshape(depth, 2, 1, d)
    bs_t = jnp.swapaxes(gmlp_b, 1, 2)
    ln_g = gmlp_ln_g.reshape(depth, 1, d)
    ps = pool_scale.reshape(depth, 1, d)
    sk = peer_subkeys.reshape(depth, 2 * PEER_HEADS, PEER_NKEYS, PEER_DKEY // 2)
    u_tab = peer_u.astype(BF16)
    v_tab = peer_v.astype(BF16)
    new_states = None
    for l in range(depth):
        last = l == depth - 1

        def path(x, nseq, n, row_fn, grid, s0):
            proj = _proj(x, mod4, row_fn, l, norm_g4, w_in)
            ya = _gmlp(proj, l, gmlp_ws, bs_t, ln_g)
            yb = _pool(proj, l, pool_w, ps, nseq, n, grid)
            ret = _retention(proj, ret_decay, nseq, n, s0, l, depth, new_states)
            yc, s_fin = (ret, None) if s0 is not None else ret
            xn, h2 = _merge(x, proj, ya, yb, yc, mod4, row_fn, l, norm_g4, w_out)
            gates = _route(h2, l, peer_wq, sk)
            return _peer(xn, h2, gates, mod4, row_fn, l, fg, u_tab, v_tab, last), s_fin

        xc, new_states = path(xc, batch, seq, ctx_row, False, None)
        xs, _ = path(xs, dec_batch, dec_seq, lat_row, True, state_ret)

    return (xc.reshape(batch, seq, d), xs.reshape(dec_batch, dec_seq, d), new_states.astype(x_prompt.dtype))
```

```python
import functools

import numpy as np
import jax
import jax.numpy as jnp
from jax import lax
from jax.experimental import pallas as pl
from jax.experimental.pallas import tpu as pltpu

F32 = jnp.float32
BF16 = jnp.bfloat16
I32 = jnp.int32

D_MODEL = 1024
A_GROUPS = 4
A_CHUNK = 128
POOL_WINDOWS = (2, 4, 8, 16)
B_GW = D_MODEL // len(POOL_WINDOWS)
GRID_W = 64
RET_HEADS = 4
RET_DV = D_MODEL // RET_HEADS
RET_DK = RET_DV // 2
RET_CHUNK = 128
PEER_HEADS = 8
PEER_NKEYS = 128
PEER_DKEY = 256
PEER_TOPK = 16
EPS = 1e-6
IN_COLS = 10 * D_MODEL

SUBLANES = 8
LANES = 128
VMEM_LIMIT = 56 * 1024 * 1024

MOD_ROWS = 16
PROJ_TB = 1024
PROJ_TN = 2048
PROJ_DTYPE = BF16
Y_DTYPE = BF16
GMLP_TB = 512
MERGE_TB = 1024
ROUTE_TB = SUBLANES * LANES
ROUTE_STEPS = 8
ROUTE_MTOK = ROUTE_TB // ROUTE_STEPS
ROUTE_MPITCH = ROUTE_MTOK + SUBLANES
PEER_TB = 1024
PEER_CH = 1024
PEER_SUB = 256


def _gelu(x):
    return 0.5 * x * (1.0 + jnp.tanh(0.7978845608028654 * (x + 0.044715 * (x * x * x))))


def _sigmoid(x):
    return 0.5 + 0.5 * jnp.tanh(0.5 * x)


def _rms(x, g):
    return x * lax.rsqrt(jnp.mean(x * x, axis=-1, keepdims=True) + EPS) * g


def _params(*sem):
    return pltpu.CompilerParams(dimension_semantics=sem, vmem_limit_bytes=VMEM_LIMIT)


def _tree(fn, xs):
    xs = list(xs)
    while len(xs) > 1:
        xs = [fn(xs[i], xs[i + 1]) for i in range(0, len(xs) - 1, 2)] + ([xs[-1]] if len(xs) % 2 else [])
    return xs[0]


def _nt_dot(a, b):
    return lax.dot_general(a, b, (((1,), (1,)), ((), ())), preferred_element_type=F32)


def _mod_kernel(cond_ref, w_ref, b_ref, o_ref):
    c = cond_ref[...]
    s = (c * _sigmoid(c)).astype(BF16)
    o_ref[...] = jnp.dot(s, w_ref[...].astype(BF16), preferred_element_type=F32) + b_ref[...]


def _modulation(cond, w_mod, b_mod):
    depth = w_mod.shape[0]
    nj = w_mod.shape[2] // D_MODEL
    return pl.pallas_call(
        _mod_kernel,
        grid=(depth, nj),
        in_specs=[
            pl.BlockSpec((MOD_ROWS, D_MODEL), lambda l, j: (0, 0)),
            pl.BlockSpec((None, D_MODEL, D_MODEL), lambda l, j: (l, 0, j)),
            pl.BlockSpec((None, 1, D_MODEL), lambda l, j: (l, 0, j)),
        ],
        out_specs=pl.BlockSpec((None, MOD_ROWS, D_MODEL), lambda l, j: (l, 0, j)),
        out_shape=jax.ShapeDtypeStruct((depth, MOD_ROWS, nj * D_MODEL), F32),
        compiler_params=_params("parallel", "parallel"),
        name="mod",
    )(cond, w_mod, b_mod.reshape(depth, 1, nj * D_MODEL))


def _proj_kernel(x_ref, mod_ref, g_ref, w_ref, o_ref, wb_ref):
    @pl.when(pl.program_id(1) == 0)
    def _():
        wb_ref[...] = w_ref[...].astype(BF16)

    y = _rms(x_ref[...], g_ref[...])
    sh = mod_ref[:, 0:D_MODEL]
    sc = mod_ref[:, D_MODEL:2 * D_MODEL]
    h = (y * (1.0 + sc) + sh).astype(BF16)
    o_ref[...] = jnp.dot(h, wb_ref[...], preferred_element_type=F32).astype(o_ref.dtype)


def _proj(x, mod4, row_fn, layer, norm_g, w_in):
    n = x.shape[0]
    tb, tn = PROJ_TB, PROJ_TN
    return pl.pallas_call(
        _proj_kernel,
        grid=(IN_COLS // tn, n // tb),
        in_specs=[
            pl.BlockSpec((tb, D_MODEL), lambda j, i: (i, 0)),
            pl.BlockSpec((None, None, 1, 6 * D_MODEL), lambda j, i: (layer, row_fn(i * tb), 0, 0)),
            pl.BlockSpec((None, None, 1, D_MODEL), lambda j, i: (layer, 0, 0, 0)),
            pl.BlockSpec((None, D_MODEL, tn), lambda j, i: (layer, 0, j)),
        ],
        out_specs=pl.BlockSpec((tb, tn), lambda j, i: (i, j)),
        out_shape=jax.ShapeDtypeStruct((n, IN_COLS), PROJ_DTYPE),
        scratch_shapes=[pltpu.VMEM((D_MODEL, tn), BF16)],
        compiler_params=_params("parallel", "arbitrary"),
        name="proj",
    )(x, mod4, norm_g, w_in)


def _gmlp_kernel(u_ref, v_ref, ws_ref, bs_ref, lng_ref, o_ref):
    gw = D_MODEL // A_GROUPS
    for c in range(GMLP_TB // A_CHUNK):
        rows = slice(c * A_CHUNK, (c + 1) * A_CHUNK)
        v = _gelu(v_ref[rows, :].astype(F32))
        vc = v - jnp.mean(v, axis=-1, keepdims=True)
        vn = vc * lax.rsqrt(jnp.mean(vc * vc, axis=-1, keepdims=True) + EPS) * lng_ref[...]
        vnb = vn.astype(BF16)
        for g in range(A_GROUPS):
            cols = slice(g * gw, (g + 1) * gw)
            sv = jnp.dot(ws_ref[g].astype(BF16), vnb[:, cols], preferred_element_type=F32) + bs_ref[:, g:g + 1]
            o_ref[rows, cols] = (_gelu(u_ref[rows, cols].astype(F32)) * sv).astype(o_ref.dtype)


def _gmlp(proj, layer, ws, bs_t, ln_g):
    n = proj.shape[0]
    tb = GMLP_TB
    return pl.pallas_call(
        _gmlp_kernel,
        grid=(n // tb,),
        in_specs=[
            pl.BlockSpec((tb, D_MODEL), lambda i: (i, 0)),
            pl.BlockSpec((tb, D_MODEL), lambda i: (i, 1)),
            pl.BlockSpec((None, A_GROUPS, A_CHUNK, A_CHUNK), lambda i: (layer, 0, 0, 0)),
            pl.BlockSpec((None, A_CHUNK, A_GROUPS), lambda i: (layer, 0, 0)),
            pl.BlockSpec((None, 1, D_MODEL), lambda i: (layer, 0, 0)),
        ],
        out_specs=pl.BlockSpec((tb, D_MODEL), lambda i: (i, 0)),
        out_shape=jax.ShapeDtypeStruct((n, D_MODEL), Y_DTYPE),
        compiler_params=_params("parallel"),
        name="gmlp",
    )(proj, proj, ws, bs_t, ln_g)


POOL_PAD = 16


def _window_count(pos, size, a, b):
    return jnp.minimum(pos + b, size - 1) - jnp.maximum(pos - a, 0) + 1


def _pool_kernel(z_ref, w_ref, scale_ref, o_ref, zp_ref, cp_ref, *, n, grid):
    rows = n // GRID_W
    rpad = cp_ref.shape[0] - n
    tok = lax.broadcasted_iota(I32, (n, 1), 0)
    for gi, w in enumerate(POOL_WINDOWS):
        a = w // 2
        b = w - 1 - a
        cols = slice(gi * B_GW, (gi + 1) * B_GW)
        z = z_ref[:, cols].astype(F32)
        zp_ref[0:POOL_PAD, :] = jnp.zeros((POOL_PAD, B_GW), F32)
        zp_ref[POOL_PAD + n:, :] = jnp.zeros((POOL_PAD, B_GW), F32)
        zp_ref[POOL_PAD:POOL_PAD + n, :] = z
        if grid:
            col = tok & (GRID_W - 1)
            row = tok >> (GRID_W.bit_length() - 1)
            s = jnp.zeros((n, B_GW), F32)
            for d in range(-a, b + 1):
                sh = zp_ref[POOL_PAD + d:POOL_PAD + d + n, :]
                ok = jnp.logical_and(col + d >= 0, col + d < GRID_W)
                s = s + jnp.where(ok, sh, 0.0)
            half = rpad // 2
            cp_ref[0:half, :] = jnp.zeros((half, B_GW), F32)
            cp_ref[half + n:, :] = jnp.zeros((half, B_GW), F32)
            cp_ref[half:half + n, :] = s
            s = jnp.zeros((n, B_GW), F32)
            for d in range(-a, b + 1):
                s = s + cp_ref[half + d * GRID_W:half + d * GRID_W + n, :]
            cnt = (_window_count(row, rows, a, b) * _window_count(col, GRID_W, a, b)).astype(F32)
        else:
            s = jnp.zeros((n, B_GW), F32)
            for d in range(-a, b + 1):
                s = s + zp_ref[POOL_PAD + d:POOL_PAD + d + n, :]
            cnt = _window_count(tok, n, a, b).astype(F32)
        diff = (s / cnt - z).astype(BF16)
        o_ref[:, cols] = (jnp.dot(diff, w_ref[gi].astype(BF16), preferred_element_type=F32)
                          * scale_ref[:, cols]).astype(o_ref.dtype)


def _pool(proj, layer, pool_w, pool_scale, nseq, n, grid):
    rpad = 2 * (max(POOL_WINDOWS) // 2) * GRID_W if grid else 2 * SUBLANES
    return pl.pallas_call(
        functools.partial(_pool_kernel, n=n, grid=grid),
        grid=(nseq,),
        in_specs=[
            pl.BlockSpec((n, D_MODEL), lambda i: (i, 2)),
            pl.BlockSpec((None, len(POOL_WINDOWS), B_GW, B_GW), lambda i: (layer, 0, 0, 0)),
            pl.BlockSpec((None, 1, D_MODEL), lambda i: (layer, 0, 0)),
        ],
        out_specs=pl.BlockSpec((n, D_MODEL), lambda i: (i, 0)),
        out_shape=jax.ShapeDtypeStruct((nseq * n, D_MODEL), Y_DTYPE),
        scratch_shapes=[pltpu.VMEM((n + 2 * POOL_PAD, B_GW), F32), pltpu.VMEM((n + rpad, B_GW), F32)],
        compiler_params=_params("parallel"),
        name="pool",
    )(proj, pool_w, pool_scale)


def _log_sigmoid(x):
    return jnp.minimum(x, 0.0) - jnp.log(1.0 + jnp.exp(-jnp.abs(x)))


def _ret_kernel(*refs, n, has_s0, has_prev_states):
    if has_s0:
        q_ref, k_ref, v_ref, gf_ref, gb_ref, rd_ref, s0_ref, y_ref, s_ref, dm_ref, yf_ref, dec_ref = refs
        sfin_ref = None
    else:
        if has_prev_states:
            refs = refs[:6] + refs[7:]
        q_ref, k_ref, v_ref, gf_ref, gb_ref, rd_ref, y_ref, sfin_ref, s_ref, dm_ref, yf_ref, dec_ref = refs
        s0_ref = None
    cl = RET_CHUNK
    nc = n // cl
    pi = lax.broadcasted_iota(I32, (cl, cl), 0).astype(F32)
    pj = lax.broadcasted_iota(I32, (cl, cl), 1).astype(F32)
    prow = lax.broadcasted_iota(I32, (cl, RET_DK), 0).astype(F32)
    kscale = RET_DK ** -0.5

    for d in range(2):
        g_ref = gf_ref if d == 0 else gb_ref
        lgs = []
        for h in range(RET_HEADS):
            lg = _log_sigmoid(rd_ref[d:d + 1, h:h + 1])
            lgs.append(lg)
            rel = (pi - pj) if d == 0 else (pj - pi)
            dm_ref[h] = jnp.where(rel >= 0.0, jnp.exp(lg * jnp.maximum(rel, 0.0)), 0.0)
            dec_ref[h] = jnp.exp(lg * ((prow + 1.0) if d == 0 else (cl - prow)))
            dec_ref[RET_HEADS + h] = jnp.exp(lg * ((cl - 1.0 - prow) if d == 0 else prow))
            if has_s0:
                s_ref[h] = s0_ref[d, h]
            else:
                s_ref[h] = jnp.zeros((RET_DK, RET_DV), F32)

        def chunk(ci, carry, d=d, g_ref=g_ref, lgs=lgs):
            c = ci if d == 0 else nc - 1 - ci
            rows = pl.ds(pl.multiple_of(c * cl, cl), cl)
            for h in range(RET_HEADS):
                qdec = dec_ref[h]
                kdec = dec_ref[RET_HEADS + h]
                cdec = jnp.exp(lgs[h] * float(cl))
                qb = q_ref[rows, h * RET_DK:(h + 1) * RET_DK].astype(BF16)
                q = qb.astype(F32)
                k = k_ref[rows, h * RET_DK:(h + 1) * RET_DK].astype(F32) * kscale
                vb = v_ref[rows, h * RET_DV:(h + 1) * RET_DV].astype(BF16)
                sc = _nt_dot(qb, k.astype(BF16)) * dm_ref[h]
                o = jnp.dot(sc.astype(BF16), vb, preferred_element_type=F32)
                s_prev = s_ref[h]
                o = o + jnp.dot((q * qdec).astype(BF16), s_prev.astype(BF16), preferred_element_type=F32)
                kd_t = jnp.transpose(k * kdec).astype(BF16)
                s_ref[h] = cdec * s_prev + jnp.dot(kd_t, vb, preferred_element_type=F32)
                on = o * lax.rsqrt(jnp.mean(o * o, axis=-1, keepdims=True) + EPS)
                g = g_ref[rows, h * RET_DV:(h + 1) * RET_DV].astype(F32)
                yv = g * _sigmoid(g) * on
                cols = slice(h * RET_DV, (h + 1) * RET_DV)
                if d == 0:
                    yf_ref[rows, cols] = yv
                else:
                    y_ref[rows, cols] = (yf_ref[rows, cols] + yv).astype(y_ref.dtype)
            return carry

        lax.fori_loop(0, nc, chunk, 0)
        if not has_s0:
            for h in range(RET_HEADS):
                sfin_ref[d, h] = s_ref[h]


def _retention(proj, ret_decay, nseq, n, s0, layer, depth=None, states=None):
    has_s0 = s0 is not None
    aliases = {}
    dkb = RET_HEADS * RET_DK
    in_specs = [
        pl.BlockSpec((n, dkb), lambda i: (i, 3 * D_MODEL // dkb)),
        pl.BlockSpec((n, dkb), lambda i: (i, 3 * D_MODEL // dkb + 1)),
        pl.BlockSpec((n, D_MODEL), lambda i: (i, 4)),
        pl.BlockSpec((n, D_MODEL), lambda i: (i, 5)),
        pl.BlockSpec((n, D_MODEL), lambda i: (i, 6)),
        pl.BlockSpec((None, 2, RET_HEADS), lambda i: (layer, 0, 0)),
    ]
    args = [proj, proj, proj, proj, proj, ret_decay]
    y_shape = jax.ShapeDtypeStruct((nseq * n, D_MODEL), Y_DTYPE)
    y_spec = pl.BlockSpec((n, D_MODEL), lambda i: (i, 0))
    if has_s0:
        in_specs.append(pl.BlockSpec((None, None, 2, RET_HEADS, RET_DK, RET_DV), lambda i: (i, layer, 0, 0, 0, 0)))
        args.append(s0)
        out_shape, out_specs = y_shape, y_spec
    else:
        state_shape = (nseq, depth, 2, RET_HEADS, RET_DK, RET_DV)
        out_shape = (y_shape, jax.ShapeDtypeStruct(state_shape, F32))
        out_specs = (y_spec, pl.BlockSpec((None, None, 2, RET_HEADS, RET_DK, RET_DV),
                                          lambda i: (i, layer, 0, 0, 0, 0)))
        if states is not None:
            in_specs.append(pl.BlockSpec(memory_space=pl.ANY))
            args.append(states)
            aliases = {len(args) - 1: 1}
    return pl.pallas_call(
        functools.partial(_ret_kernel, n=n, has_s0=has_s0, has_prev_states=states is not None),
        grid=(nseq,),
        in_specs=in_specs,
        out_specs=out_specs,
        out_shape=out_shape,
        input_output_aliases=aliases,
        scratch_shapes=[pltpu.VMEM((RET_HEADS, RET_DK, RET_DV), F32),
                        pltpu.VMEM((RET_HEADS, RET_CHUNK, RET_CHUNK), F32),
                        pltpu.VMEM((n, D_MODEL), F32),
                        pltpu.VMEM((2 * RET_HEADS, RET_CHUNK, RET_DK), F32)],
        compiler_params=_params("parallel"),
        name="ret",
    )(*args)


def _merge_kernel(x_ref, ga_ref, gb_ref, gc_ref, ya_ref, yb_ref, yc_ref, mod_ref, g2_ref, w_ref, xo_ref, h2_ref):
    merged = (_sigmoid(ga_ref[...].astype(F32)) * ya_ref[...] + _sigmoid(gb_ref[...].astype(F32)) * yb_ref[...]
              + _sigmoid(gc_ref[...].astype(F32)) * yc_ref[...])
    o = jnp.dot(merged.astype(BF16), w_ref[...].astype(BF16), preferred_element_type=F32)
    gt1 = mod_ref[:, 2 * D_MODEL:3 * D_MODEL]
    sh2 = mod_ref[:, 3 * D_MODEL:4 * D_MODEL]
    sc2 = mod_ref[:, 4 * D_MODEL:5 * D_MODEL]
    xn = x_ref[...] + gt1 * o
    xo_ref[...] = xn
    h2_ref[...] = (_rms(xn, g2_ref[...]) * (1.0 + sc2) + sh2).astype(BF16)


def _merge(x, proj, ya, yb, yc, mod4, row_fn, layer, norm_g2, w_out):
    n = x.shape[0]
    tb = MERGE_TB
    tok = lambda c: pl.BlockSpec((tb, D_MODEL), lambda i: (i, c))
    return pl.pallas_call(
        _merge_kernel,
        grid=(n // tb,),
        in_specs=[
            tok(0), tok(7), tok(8), tok(9), tok(0), tok(0), tok(0),
            pl.BlockSpec((None, None, 1, 6 * D_MODEL), lambda i: (layer, row_fn(i * tb), 0, 0)),
            pl.BlockSpec((None, None, 1, D_MODEL), lambda i: (layer, 1, 0, 0)),
            pl.BlockSpec((None, D_MODEL, D_MODEL), lambda i: (layer, 0, 0)),
        ],
        out_specs=(tok(0), tok(0)),
        out_shape=(jax.ShapeDtypeStruct((n, D_MODEL), F32), jax.ShapeDtypeStruct((n, D_MODEL), BF16)),
        compiler_params=_params("parallel"),
        name="merge",
    )(x, proj, proj, proj, ya, yb, yc, mod4, norm_g2, w_out)


_CAND_PAIRS = tuple((j1, j2) for j1 in range(PEER_TOPK) for j2 in range(PEER_TOPK)
                    if (j1 + 1) * (j2 + 1) <= PEER_TOPK)
ROUTE_ACCS = 4
ROUTE_LAG = 1


def _scan_max(n_items, load, knock, store):
    per = -(-n_items // ROUTE_ACCS)
    accs = []
    for a0 in range(0, n_items, per):
        bv = bt = None
        for i in range(a0, min(a0 + per, n_items)):
            v, t = load(i)
            v = knock(i, v, t)
            store(i, v)
            if bv is None:
                bv, bt = v, t
            else:
                gt = v > bv
                bv = jnp.where(gt, v, bv)
                bt = jnp.where(gt, t, bt)
        accs.append((bv, bt))
    bv, bt = accs[0]
    for v, t in accs[1:]:
        gt = v > bv
        bv = jnp.where(gt, v, bv)
        bt = jnp.where(gt, t, bt)
    return bv, bt


def _route_kernel(h_ref, wq_ref, sk_ref, m_ref, q_scr, x_scr, ts_scr, ti_scr, c_scr, k_scr, eo_scr, go_scr,
                  pe_scr, pg_scr, *, n_blocks):
    tb = h_ref.shape[0]
    nk = PEER_NKEYS
    kd = PEER_DKEY // 2
    nlt = tb // LANES
    assert nlt == SUBLANES
    neg = -jnp.inf
    tile = lambda i: pl.ds(i * SUBLANES, SUBLANES)
    blk = pl.program_id(0)
    step = pl.program_id(1)
    live = blk < n_blocks

    @pl.when(jnp.logical_and(blk == 0, step == 0))
    def _():
        pe_scr[...] = jnp.zeros_like(pe_scr)
        pg_scr[...] = jnp.zeros_like(pg_scr)

    @pl.when(jnp.logical_and(live, step == 0))
    def _():
        for hd in range(PEER_HEADS):
            cols = slice(hd * PEER_DKEY, (hd + 1) * PEER_DKEY)
            q = jnp.dot(h_ref[...], wq_ref[:, cols].astype(BF16), preferred_element_type=F32).astype(BF16)
            q_scr[2 * hd] = q[:, :kd]
            q_scr[2 * hd + 1] = q[:, kd:]

    def gate_tasks():
        t0 = step * ROUTE_MTOK
        key_iota = lax.broadcasted_iota(I32, (nk, nk), 0).astype(BF16)
        zero = jnp.zeros((nk, nk), BF16)
        one = jnp.ones((nk, nk), BF16)
        m_flat = m_ref
        for spare in range(ROUTE_MTOK, ROUTE_MPITCH):
            m_flat[pl.ds(spare, nk, stride=ROUTE_MPITCH), :] = jnp.zeros((nk, nk), F32)
        group_rows = {}

        def pair_task(grp, pair):
            if pair == 0:
                rows = pl.ds(pl.multiple_of(t0 + grp * SUBLANES, SUBLANES), SUBLANES)
                er = pe_scr[rows, :]
                group_rows[grp] = ((er >> 7).astype(F32).astype(BF16), (er & (nk - 1)).astype(F32).astype(BF16),
                                   pg_scr[rows, :].astype(BF16))
            i1r, i2r, gr = group_rows[grp]
            at, bt = [], []
            for tt in (2 * pair, 2 * pair + 1):
                at.append(jnp.where(key_iota == i1r[tt:tt + 1, :], gr[tt:tt + 1, :], zero))
                bt.append(jnp.where(key_iota == i2r[tt:tt + 1, :], one, zero))
            lhs = jnp.concatenate([jnp.concatenate([at[0], zero], axis=1),
                                   jnp.concatenate([zero, at[1]], axis=1)], axis=0)
            out = _nt_dot(lhs, jnp.concatenate(bt, axis=1))
            tok = grp * SUBLANES + 2 * pair
            m_flat[pl.ds(tok, nk, stride=ROUTE_MPITCH), :] = out[:nk, :]
            m_flat[pl.ds(tok + 1, nk, stride=ROUTE_MPITCH), :] = out[nk:, :]
            return out[:SUBLANES, :]

        return [functools.partial(pair_task, grp, pair)
                for grp in range(ROUTE_MTOK // SUBLANES) for pair in range(SUBLANES // 2)]

    def side(hs, between_rounds):
        sk = sk_ref[hs].astype(BF16)
        for j in range(nlt):
            sc = _nt_dot(sk, q_scr[hs, j * LANES:(j + 1) * LANES, :])
            x_scr[pl.ds(j, nk, stride=nlt), :] = sc
        prev = None
        for r in range(PEER_TOPK):
            load = lambda n: (x_scr[tile(n), :], jnp.full((SUBLANES, LANES), n, I32))
            if prev is None:
                knock = lambda n, v, t: v
                store = lambda n, v: None
            else:
                knock = lambda n, v, t, prev=prev: jnp.where(prev == n, neg, v)
                store = lambda n, v: x_scr.__setitem__((tile(n), slice(None)), v)
            bv, bi = _scan_max(nk, load, knock, store)
            ts_scr[hs, r] = bv
            ti_scr[hs, r] = bi
            anchor = between_rounds()
            prev = bi if anchor is None else bi + anchor

    sides_per_step = 2 * PEER_HEADS // ROUTE_STEPS

    @pl.when(live)
    def _():
        tasks = gate_tasks()
        per_round = -(-len(tasks) // (sides_per_step * PEER_TOPK))
        issued = []

        def between_rounds():
            issued.append([tasks.pop(0)() for _ in range(min(per_round, len(tasks)))])
            if len(issued) <= ROUTE_LAG or not issued[-1 - ROUTE_LAG]:
                return None
            return _tree(jnp.add, [jnp.where(t > 3e38, 1, 0) for t in issued[-1 - ROUTE_LAG]])

        for k in range(sides_per_step):
            side(step * sides_per_step + k, between_rounds)
        while tasks:
            tasks.pop(0)()

    @pl.when(blk == n_blocks)
    def _():
        for task in gate_tasks():
            task()

    def head(h, carry):
        for p, (j1, j2) in enumerate(_CAND_PAIRS):
            c_scr[tile(p), :] = ts_scr[2 * h, j1] + ts_scr[2 * h + 1, j2]
            k_scr[tile(p), :] = ((j1 * PEER_TOPK + j2) * (nk * nk) + ti_scr[2 * h, j1] * nk
                                 + ti_scr[2 * h + 1, j2])
        prev = None
        vals, exps = [], []
        for r in range(PEER_TOPK):
            load = lambda p: (c_scr[tile(p), :], k_scr[tile(p), :])
            if prev is None:
                knock = lambda p, v, t: v
                store = lambda p, v: None
            else:
                knock = lambda p, v, t, prev=prev: jnp.where(t == prev, neg, v)
                store = lambda p, v: c_scr.__setitem__((tile(p), slice(None)), v)
            bv, bk = _scan_max(len(_CAND_PAIRS), load, knock, store)
            vals.append(bv)
            exps.append(bk & (nk * nk - 1))
            prev = bk
        ps = [jnp.exp(v - vals[0]) for v in vals]
        z = _tree(jnp.add, ps)
        for r in range(PEER_TOPK):
            row = pl.ds(pl.multiple_of((h * PEER_TOPK + r) * SUBLANES, SUBLANES), SUBLANES)
            go_scr[row, :] = ps[r] / z
            eo_scr[row, :] = exps[r]
        return carry

    @pl.when(jnp.logical_and(live, step == ROUTE_STEPS - 1))
    def _():
        lax.fori_loop(0, PEER_HEADS, head, 0)
        nsel = PEER_HEADS * PEER_TOPK
        for j in range(nlt):
            rows = slice(j * LANES, (j + 1) * LANES)
            pe_scr[rows, :] = jnp.transpose(eo_scr[pl.ds(j, nsel, stride=nlt), :])
            pg_scr[rows, :] = jnp.transpose(go_scr[pl.ds(j, nsel, stride=nlt), :])


def _route(h2, layer, wq, subkeys):
    n = h2.shape[0]
    tb = ROUTE_TB
    n_blocks = n // tb
    nsel = PEER_HEADS * PEER_TOPK
    kd = PEER_DKEY // 2
    nk = PEER_NKEYS
    tile_rows = lambda count: count * SUBLANES
    gates = pl.pallas_call(
        functools.partial(_route_kernel, n_blocks=n_blocks),
        grid=(n_blocks + 1, ROUTE_STEPS),
        in_specs=[
            pl.BlockSpec((tb, D_MODEL), lambda i, s: (jnp.minimum(i, n_blocks - 1), 0)),
            pl.BlockSpec((None, D_MODEL, PEER_HEADS * PEER_DKEY), lambda i, s: (layer, 0, 0)),
            pl.BlockSpec((None, 2 * PEER_HEADS, PEER_NKEYS, kd), lambda i, s: (layer, 0, 0, 0)),
        ],
        out_specs=pl.BlockSpec((None, nk * ROUTE_MPITCH, nk),
                               lambda i, s: (jnp.maximum((i - 1) * ROUTE_STEPS + s, 0), 0, 0)),
        out_shape=jax.ShapeDtypeStruct((n // ROUTE_MTOK, nk * ROUTE_MPITCH, nk), F32),
        scratch_shapes=[
            pltpu.VMEM((2 * PEER_HEADS, tb, kd), BF16),
            pltpu.VMEM((tile_rows(PEER_NKEYS), LANES), F32),
            pltpu.VMEM((2 * PEER_HEADS, PEER_TOPK, SUBLANES, LANES), F32),
            pltpu.VMEM((2 * PEER_HEADS, PEER_TOPK, SUBLANES, LANES), I32),
            pltpu.VMEM((tile_rows(len(_CAND_PAIRS)), LANES), F32),
            pltpu.VMEM((tile_rows(len(_CAND_PAIRS)), LANES), I32),
            pltpu.VMEM((tile_rows(nsel), LANES), I32),
            pltpu.VMEM((tile_rows(nsel), LANES), F32),
            pltpu.VMEM((tb, nsel), I32),
            pltpu.VMEM((tb, nsel), F32),
        ],
        compiler_params=_params("arbitrary", "arbitrary"),
        name="route",
    )(h2, wq, subkeys)
    return gates.reshape(n // ROUTE_MTOK, nk, ROUTE_MPITCH, nk)


def _peer_kernel(x_ref, h_ref, m_ref, mod_ref, fg_ref, u_ref, v_ref, o_ref, p0_ref, p1_ref, acc_ref,
                 *, final, n_chunks):
    c = pl.program_id(1)

    def step(prev_ref, cur_ref):
        n_sub = PEER_CH // PEER_SUB
        nw = D_MODEL // n_sub
        for sub in range(n_sub):
            if cur_ref is not None:
                rows = slice(sub * PEER_SUB, (sub + 1) * PEER_SUB)
                s = _gelu(_nt_dot(h_ref[...], u_ref[rows, :]))
                per = PEER_SUB // PEER_NKEYS
                m = jnp.concatenate(
                    [jnp.concatenate([m_ref[g, sub * per + a, :ROUTE_MTOK, :]
                                      for g in range(h_ref.shape[0] // ROUTE_MTOK)], axis=0)
                     for a in range(per)], axis=1)
                cur_ref[:, rows] = (m * s).astype(BF16)
            if prev_ref is not None:
                cols = slice(sub * nw, (sub + 1) * nw)
                acc_ref[:, cols] += jnp.dot(prev_ref[...], v_ref[:, cols], preferred_element_type=F32)

    stage = (p0_ref, p1_ref)
    steady = jnp.logical_and(c > 0, c < n_chunks)

    @pl.when(c == 0)
    def _():
        acc_ref[...] = jnp.zeros_like(acc_ref)
        step(None, stage[0])

    for parity in range(2):
        @pl.when(jnp.logical_and(steady, c % 2 == parity))
        def _(parity=parity):
            step(stage[1 - parity], stage[parity])

    @pl.when(c == n_chunks)
    def _():
        step(stage[(n_chunks - 1) % 2], None)
        gt2 = mod_ref[:, 5 * D_MODEL:6 * D_MODEL]
        xn = x_ref[...] + gt2 * acc_ref[...]
        o_ref[...] = _rms(xn, fg_ref[...]) if final else xn


def _peer(x, h2, gates, mod4, row_fn, layer, final_g, u_tab, v_tab, final):
    n = x.shape[0]
    tb = PEER_TB
    n_chunks = u_tab.shape[1] // PEER_CH
    return pl.pallas_call(
        functools.partial(_peer_kernel, final=final, n_chunks=n_chunks),
        grid=(n // tb, n_chunks + 1),
        in_specs=[
            pl.BlockSpec((tb, D_MODEL), lambda i, c: (i, 0)),
            pl.BlockSpec((tb, D_MODEL), lambda i, c: (i, 0)),
            pl.BlockSpec((tb // ROUTE_MTOK, PEER_CH // PEER_NKEYS, ROUTE_MPITCH, PEER_NKEYS),
                         lambda i, c: (i, jnp.minimum(c, n_chunks - 1), 0, 0)),
            pl.BlockSpec((None, None, 1, 6 * D_MODEL), lambda i, c: (layer, row_fn(i * tb), 0, 0)),
            pl.BlockSpec((1, D_MODEL), lambda i, c: (0, 0)),
            pl.BlockSpec((None, PEER_CH, D_MODEL), lambda i, c: (layer, jnp.minimum(c, n_chunks - 1), 0)),
            pl.BlockSpec((None, PEER_CH, D_MODEL), lambda i, c: (layer, jnp.maximum(c - 1, 0), 0)),
        ],
        out_specs=pl.BlockSpec((tb, D_MODEL), lambda i, c: (i, 0)),
        out_shape=jax.ShapeDtypeStruct((n, D_MODEL), F32),
        scratch_shapes=[pltpu.VMEM((tb, PEER_CH), BF16),
                        pltpu.VMEM((tb, PEER_CH), BF16),
                        pltpu.VMEM((tb, D_MODEL), F32)],
        compiler_params=_params("parallel", "arbitrary"),
        name="peer",
    )(x, h2, gates, mod4, final_g, u_tab, v_tab)


def kernel(x_prompt, x_sample, state_ret, c, c_ctx, w_mod, b_mod, norm_g, w_in, w_out, gmlp_ws, gmlp_b,
           gmlp_ln_g, pool_w, pool_scale, ret_decay, peer_wq, peer_subkeys, peer_u, peer_v, final_norm_g):
    batch, seq, d = x_prompt.shape
    dec_batch, dec_seq, _ = x_sample.shape
    depth = w_mod.shape[0]
    assert d == D_MODEL and dec_batch + 1 <= MOD_ROWS and dec_seq % GRID_W == 0

    cond = jnp.zeros((MOD_ROWS, d), F32).at[0].set(c_ctx).at[1:1 + dec_batch].set(c)
    mod = _modulation(cond, w_mod, b_mod)
    mod4 = mod.reshape(depth, MOD_ROWS, 1, 6 * d)
    ctx_row = lambda tok0: 0
    lat_row = lambda tok0: 1 + tok0 // dec_seq

    xc = x_prompt.reshape(batch * seq, d)
    xs = x_sample.reshape(dec_batch * dec_seq, d)
    fg = final_norm_g.reshape(1, d)
    norm_g4 = norm_g.reshape(depth, 2, 1, d)
    bs_t = jnp.swapaxes(gmlp_b, 1, 2)
    ln_g = gmlp_ln_g.reshape(depth, 1, d)
    ps = pool_scale.reshape(depth, 1, d)
    sk = peer_subkeys.reshape(depth, 2 * PEER_HEADS, PEER_NKEYS, PEER_DKEY // 2)
    u_tab = peer_u.astype(BF16)
    v_tab = peer_v.astype(BF16)
    new_states = None
    for l in range(depth):
        last = l == depth - 1

        def path(x, nseq, n, row_fn, grid, s0):
            proj = _proj(x, mod4, row_fn, l, norm_g4, w_in)
            ya = _gmlp(proj, l, gmlp_ws, bs_t, ln_g)
            yb = _pool(proj, l, pool_w, ps, nseq, n, grid)
            ret = _retention(proj, ret_decay, nseq, n, s0, l, depth, new_states)
            yc, s_fin = (ret, None) if s0 is not None else ret
            xn, h2 = _merge(x, proj, ya, yb, yc, mod4, row_fn, l, norm_g4, w_out)
            gates = _route(h2, l, peer_wq, sk)
            return _peer(xn, h2, gates, mod4, row_fn, l, fg, u_tab, v_tab, last), s_fin

        xc, new_states = path(xc, batch, seq, ctx_row, False, None)
        xs, _ = path(xs, dec_batch, dec_seq, lat_row, True, state_ret)

    return (xc.reshape(batch, seq, d), xs.reshape(dec_batch, dec_seq, d), new_states.astype(x_prompt.dtype))
```

```python
import functools

import numpy as np
import jax
import jax.numpy as jnp
from jax import lax
from jax.experimental import pallas as pl
from jax.experimental.pallas import tpu as pltpu

F32 = jnp.float32
BF16 = jnp.bfloat16
I32 = jnp.int32

D_MODEL = 1024
A_GROUPS = 4
A_CHUNK = 128
POOL_WINDOWS = (2, 4, 8, 16)
B_GW = D_MODEL // len(POOL_WINDOWS)
GRID_W = 64
RET_HEADS = 4
RET_DV = D_MODEL // RET_HEADS
RET_DK = RET_DV // 2
RET_CHUNK = 128
PEER_HEADS = 8
PEER_NKEYS = 128
PEER_DKEY = 256
PEER_TOPK = 16
EPS = 1e-6
IN_COLS = 10 * D_MODEL

SUBLANES = 8
LANES = 128
VMEM_LIMIT = 56 * 1024 * 1024

MOD_ROWS = 16
PROJ_TB = 1024
PROJ_TN = 2048
PROJ_DTYPE = BF16
Y_DTYPE = BF16
GMLP_TB = 512
MERGE_TB = 1024
ROUTE_TB = SUBLANES * LANES
ROUTE_STEPS = 8
ROUTE_MTOK = ROUTE_TB // ROUTE_STEPS
ROUTE_MPITCH = ROUTE_MTOK // 2 + SUBLANES
PEER_TB = 1024
PEER_CH = 1024
PEER_SUB = 256


def _gelu(x):
    return 0.5 * x * (1.0 + jnp.tanh(0.7978845608028654 * (x + 0.044715 * (x * x * x))))


def _sigmoid(x):
    return 0.5 + 0.5 * jnp.tanh(0.5 * x)


def _rms(x, g):
    return x * lax.rsqrt(jnp.mean(x * x, axis=-1, keepdims=True) + EPS) * g


def _params(*sem):
    return pltpu.CompilerParams(dimension_semantics=sem, vmem_limit_bytes=VMEM_LIMIT)


def _tree(fn, xs):
    xs = list(xs)
    while len(xs) > 1:
        xs = [fn(xs[i], xs[i + 1]) for i in range(0, len(xs) - 1, 2)] + ([xs[-1]] if len(xs) % 2 else [])
    return xs[0]


def _nt_dot(a, b):
    return lax.dot_general(a, b, (((1,), (1,)), ((), ())), preferred_element_type=F32)


def _mod_kernel(cond_ref, w_ref, b_ref, o_ref):
    c = cond_ref[...]
    s = (c * _sigmoid(c)).astype(BF16)
    o_ref[...] = jnp.dot(s, w_ref[...].astype(BF16), preferred_element_type=F32) + b_ref[...]


def _modulation(cond, w_mod, b_mod):
    depth = w_mod.shape[0]
    nj = w_mod.shape[2] // D_MODEL
    return pl.pallas_call(
        _mod_kernel,
        grid=(depth, nj),
        in_specs=[
            pl.BlockSpec((MOD_ROWS, D_MODEL), lambda l, j: (0, 0)),
            pl.BlockSpec((None, D_MODEL, D_MODEL), lambda l, j: (l, 0, j)),
            pl.BlockSpec((None, 1, D_MODEL), lambda l, j: (l, 0, j)),
        ],
        out_specs=pl.BlockSpec((None, MOD_ROWS, D_MODEL), lambda l, j: (l, 0, j)),
        out_shape=jax.ShapeDtypeStruct((depth, MOD_ROWS, nj * D_MODEL), F32),
        compiler_params=_params("parallel", "parallel"),
        name="mod",
    )(cond, w_mod, b_mod.reshape(depth, 1, nj * D_MODEL))


def _proj_kernel(x_ref, mod_ref, g_ref, w_ref, o_ref, wb_ref):
    @pl.when(pl.program_id(1) == 0)
    def _():
        wb_ref[...] = w_ref[...].astype(BF16)

    y = _rms(x_ref[...], g_ref[...])
    sh = mod_ref[:, 0:D_MODEL]
    sc = mod_ref[:, D_MODEL:2 * D_MODEL]
    h = (y * (1.0 + sc) + sh).astype(BF16)
    o_ref[...] = jnp.dot(h, wb_ref[...], preferred_element_type=F32).astype(o_ref.dtype)


def _proj(x, mod4, row_fn, layer, norm_g, w_in):
    n = x.shape[0]
    tb, tn = PROJ_TB, PROJ_TN
    return pl.pallas_call(
        _proj_kernel,
        grid=(IN_COLS // tn, n // tb),
        in_specs=[
            pl.BlockSpec((tb, D_MODEL), lambda j, i: (i, 0)),
            pl.BlockSpec((None, None, 1, 6 * D_MODEL), lambda j, i: (layer, row_fn(i * tb), 0, 0)),
            pl.BlockSpec((None, None, 1, D_MODEL), lambda j, i: (layer, 0, 0, 0)),
            pl.BlockSpec((None, D_MODEL, tn), lambda j, i: (layer, 0, j)),
        ],
        out_specs=pl.BlockSpec((tb, tn), lambda j, i: (i, j)),
        out_shape=jax.ShapeDtypeStruct((n, IN_COLS), PROJ_DTYPE),
        scratch_shapes=[pltpu.VMEM((D_MODEL, tn), BF16)],
        compiler_params=_params("parallel", "arbitrary"),
        name="proj",
    )(x, mod4, norm_g, w_in)


def _gmlp_kernel(u_ref, v_ref, ws_ref, bs_ref, lng_ref, o_ref):
    gw = D_MODEL // A_GROUPS
    for c in range(GMLP_TB // A_CHUNK):
        rows = slice(c * A_CHUNK, (c + 1) * A_CHUNK)
        v = _gelu(v_ref[rows, :].astype(F32))
        vc = v - jnp.mean(v, axis=-1, keepdims=True)
        vn = vc * lax.rsqrt(jnp.mean(vc * vc, axis=-1, keepdims=True) + EPS) * lng_ref[...]
        vnb = vn.astype(BF16)
        for g in range(A_GROUPS):
            cols = slice(g * gw, (g + 1) * gw)
            sv = jnp.dot(ws_ref[g].astype(BF16), vnb[:, cols], preferred_element_type=F32) + bs_ref[:, g:g + 1]
            o_ref[rows, cols] = (_gelu(u_ref[rows, cols].astype(F32)) * sv).astype(o_ref.dtype)


def _gmlp(proj, layer, ws, bs_t, ln_g):
    n = proj.shape[0]
    tb = GMLP_TB
    return pl.pallas_call(
        _gmlp_kernel,
        grid=(n // tb,),
        in_specs=[
            pl.BlockSpec((tb, D_MODEL), lambda i: (i, 0)),
            pl.BlockSpec((tb, D_MODEL), lambda i: (i, 1)),
            pl.BlockSpec((None, A_GROUPS, A_CHUNK, A_CHUNK), lambda i: (layer, 0, 0, 0)),
            pl.BlockSpec((None, A_CHUNK, A_GROUPS), lambda i: (layer, 0, 0)),
            pl.BlockSpec((None, 1, D_MODEL), lambda i: (layer, 0, 0)),
        ],
        out_specs=pl.BlockSpec((tb, D_MODEL), lambda i: (i, 0)),
        out_shape=jax.ShapeDtypeStruct((n, D_MODEL), Y_DTYPE),
        compiler_params=_params("parallel"),
        name="gmlp",
    )(proj, proj, ws, bs_t, ln_g)


POOL_PAD = 16


def _window_count(pos, size, a, b):
    return jnp.minimum(pos + b, size - 1) - jnp.maximum(pos - a, 0) + 1


def _pool_kernel(z_ref, w_ref, scale_ref, o_ref, zp_ref, cp_ref, *, n, grid):
    rows = n // GRID_W
    rpad = cp_ref.shape[0] - n
    tok = lax.broadcasted_iota(I32, (n, 1), 0)
    for gi, w in enumerate(POOL_WINDOWS):
        a = w // 2
        b = w - 1 - a
        cols = slice(gi * B_GW, (gi + 1) * B_GW)
        z = z_ref[:, cols].astype(F32)
        zp_ref[0:POOL_PAD, :] = jnp.zeros((POOL_PAD, B_GW), F32)
        zp_ref[POOL_PAD + n:, :] = jnp.zeros((POOL_PAD, B_GW), F32)
        zp_ref[POOL_PAD:POOL_PAD + n, :] = z
        if grid:
            col = tok & (GRID_W - 1)
            row = tok >> (GRID_W.bit_length() - 1)
            s = jnp.zeros((n, B_GW), F32)
            for d in range(-a, b + 1):
                sh = zp_ref[POOL_PAD + d:POOL_PAD + d + n, :]
                ok = jnp.logical_and(col + d >= 0, col + d < GRID_W)
                s = s + jnp.where(ok, sh, 0.0)
            half = rpad // 2
            cp_ref[0:half, :] = jnp.zeros((half, B_GW), F32)
            cp_ref[half + n:, :] = jnp.zeros((half, B_GW), F32)
            cp_ref[half:half + n, :] = s
            s = jnp.zeros((n, B_GW), F32)
            for d in range(-a, b + 1):
                s = s + cp_ref[half + d * GRID_W:half + d * GRID_W + n, :]
            cnt = (_window_count(row, rows, a, b) * _window_count(col, GRID_W, a, b)).astype(F32)
        else:
            s = jnp.zeros((n, B_GW), F32)
            for d in range(-a, b + 1):
                s = s + zp_ref[POOL_PAD + d:POOL_PAD + d + n, :]
            cnt = _window_count(tok, n, a, b).astype(F32)
        diff = (s / cnt - z).astype(BF16)
        o_ref[:, cols] = (jnp.dot(diff, w_ref[gi].astype(BF16), preferred_element_type=F32)
                          * scale_ref[:, cols]).astype(o_ref.dtype)


def _pool(proj, layer, pool_w, pool_scale, nseq, n, grid):
    rpad = 2 * (max(POOL_WINDOWS) // 2) * GRID_W if grid else 2 * SUBLANES
    return pl.pallas_call(
        functools.partial(_pool_kernel, n=n, grid=grid),
        grid=(nseq,),
        in_specs=[
            pl.BlockSpec((n, D_MODEL), lambda i: (i, 2)),
            pl.BlockSpec((None, len(POOL_WINDOWS), B_GW, B_GW), lambda i: (layer, 0, 0, 0)),
            pl.BlockSpec((None, 1, D_MODEL), lambda i: (layer, 0, 0)),
        ],
        out_specs=pl.BlockSpec((n, D_MODEL), lambda i: (i, 0)),
        out_shape=jax.ShapeDtypeStruct((nseq * n, D_MODEL), Y_DTYPE),
        scratch_shapes=[pltpu.VMEM((n + 2 * POOL_PAD, B_GW), F32), pltpu.VMEM((n + rpad, B_GW), F32)],
        compiler_params=_params("parallel"),
        name="pool",
    )(proj, pool_w, pool_scale)


def _log_sigmoid(x):
    return jnp.minimum(x, 0.0) - jnp.log(1.0 + jnp.exp(-jnp.abs(x)))


def _ret_kernel(*refs, n, has_s0, has_prev_states):
    if has_s0:
        q_ref, k_ref, v_ref, gf_ref, gb_ref, rd_ref, s0_ref, y_ref, s_ref, dm_ref, yf_ref, dec_ref = refs
        sfin_ref = None
    else:
        if has_prev_states:
            refs = refs[:6] + refs[7:]
        q_ref, k_ref, v_ref, gf_ref, gb_ref, rd_ref, y_ref, sfin_ref, s_ref, dm_ref, yf_ref, dec_ref = refs
        s0_ref = None
    cl = RET_CHUNK
    nc = n // cl
    pi = lax.broadcasted_iota(I32, (cl, cl), 0).astype(F32)
    pj = lax.broadcasted_iota(I32, (cl, cl), 1).astype(F32)
    prow = lax.broadcasted_iota(I32, (cl, RET_DK), 0).astype(F32)
    kscale = RET_DK ** -0.5

    for d in range(2):
        g_ref = gf_ref if d == 0 else gb_ref
        lgs = []
        for h in range(RET_HEADS):
            lg = _log_sigmoid(rd_ref[d:d + 1, h:h + 1])
            lgs.append(lg)
            rel = (pi - pj) if d == 0 else (pj - pi)
            dm_ref[h] = jnp.where(rel >= 0.0, jnp.exp(lg * jnp.maximum(rel, 0.0)), 0.0)
            dec_ref[h] = jnp.exp(lg * ((prow + 1.0) if d == 0 else (cl - prow)))
            dec_ref[RET_HEADS + h] = jnp.exp(lg * ((cl - 1.0 - prow) if d == 0 else prow))
            if has_s0:
                s_ref[h] = s0_ref[d, h]
            else:
                s_ref[h] = jnp.zeros((RET_DK, RET_DV), F32)

        def chunk(ci, carry, d=d, g_ref=g_ref, lgs=lgs):
            c = ci if d == 0 else nc - 1 - ci
            rows = pl.ds(pl.multiple_of(c * cl, cl), cl)
            for h in range(RET_HEADS):
                qdec = dec_ref[h]
                kdec = dec_ref[RET_HEADS + h]
                cdec = jnp.exp(lgs[h] * float(cl))
                qb = q_ref[rows, h * RET_DK:(h + 1) * RET_DK].astype(BF16)
                q = qb.astype(F32)
                k = k_ref[rows, h * RET_DK:(h + 1) * RET_DK].astype(F32) * kscale
                vb = v_ref[rows, h * RET_DV:(h + 1) * RET_DV].astype(BF16)
                sc = _nt_dot(qb, k.astype(BF16)) * dm_ref[h]
                o = jnp.dot(sc.astype(BF16), vb, preferred_element_type=F32)
                s_prev = s_ref[h]
                o = o + jnp.dot((q * qdec).astype(BF16), s_prev.astype(BF16), preferred_element_type=F32)
                kd_t = jnp.transpose(k * kdec).astype(BF16)
                s_ref[h] = cdec * s_prev + jnp.dot(kd_t, vb, preferred_element_type=F32)
                on = o * lax.rsqrt(jnp.mean(o * o, axis=-1, keepdims=True) + EPS)
                g = g_ref[rows, h * RET_DV:(h + 1) * RET_DV].astype(F32)
                yv = g * _sigmoid(g) * on
                cols = slice(h * RET_DV, (h + 1) * RET_DV)
                if d == 0:
                    yf_ref[rows, cols] = yv
                else:
                    y_ref[rows, cols] = (yf_ref[rows, cols] + yv).astype(y_ref.dtype)
            return carry

        lax.fori_loop(0, nc, chunk, 0)
        if not has_s0:
            for h in range(RET_HEADS):
                sfin_ref[d, h] = s_ref[h]


def _retention(proj, ret_decay, nseq, n, s0, layer, depth=None, states=None):
    has_s0 = s0 is not None
    aliases = {}
    dkb = RET_HEADS * RET_DK
    in_specs = [
        pl.BlockSpec((n, dkb), lambda i: (i, 3 * D_MODEL // dkb)),
        pl.BlockSpec((n, dkb), lambda i: (i, 3 * D_MODEL // dkb + 1)),
        pl.BlockSpec((n, D_MODEL), lambda i: (i, 4)),
        pl.BlockSpec((n, D_MODEL), lambda i: (i, 5)),
        pl.BlockSpec((n, D_MODEL), lambda i: (i, 6)),
        pl.BlockSpec((None, 2, RET_HEADS), lambda i: (layer, 0, 0)),
    ]
    args = [proj, proj, proj, proj, proj, ret_decay]
    y_shape = jax.ShapeDtypeStruct((nseq * n, D_MODEL), Y_DTYPE)
    y_spec = pl.BlockSpec((n, D_MODEL), lambda i: (i, 0))
    if has_s0:
        in_specs.append(pl.BlockSpec((None, None, 2, RET_HEADS, RET_DK, RET_DV), lambda i: (i, layer, 0, 0, 0, 0)))
        args.append(s0)
        out_shape, out_specs = y_shape, y_spec
    else:
        state_shape = (nseq, depth, 2, RET_HEADS, RET_DK, RET_DV)
        out_shape = (y_shape, jax.ShapeDtypeStruct(state_shape, F32))
        out_specs = (y_spec, pl.BlockSpec((None, None, 2, RET_HEADS, RET_DK, RET_DV),
                                          lambda i: (i, layer, 0, 0, 0, 0)))
        if states is not None:
            in_specs.append(pl.BlockSpec(memory_space=pl.ANY))
            args.append(states)
            aliases = {len(args) - 1: 1}
    return pl.pallas_call(
        functools.partial(_ret_kernel, n=n, has_s0=has_s0, has_prev_states=states is not None),
        grid=(nseq,),
        in_specs=in_specs,
        out_specs=out_specs,
        out_shape=out_shape,
        input_output_aliases=aliases,
        scratch_shapes=[pltpu.VMEM((RET_HEADS, RET_DK, RET_DV), F32),
                        pltpu.VMEM((RET_HEADS, RET_CHUNK, RET_CHUNK), F32),
                        pltpu.VMEM((n, D_MODEL), F32),
                        pltpu.VMEM((2 * RET_HEADS, RET_CHUNK, RET_DK), F32)],
        compiler_params=_params("parallel"),
        name="ret",
    )(*args)


def _merge_kernel(x_ref, ga_ref, gb_ref, gc_ref, ya_ref, yb_ref, yc_ref, mod_ref, g2_ref, w_ref, xo_ref, h2_ref):
    merged = (_sigmoid(ga_ref[...].astype(F32)) * ya_ref[...] + _sigmoid(gb_ref[...].astype(F32)) * yb_ref[...]
              + _sigmoid(gc_ref[...].astype(F32)) * yc_ref[...])
    o = jnp.dot(merged.astype(BF16), w_ref[...].astype(BF16), preferred_element_type=F32)
    gt1 = mod_ref[:, 2 * D_MODEL:3 * D_MODEL]
    sh2 = mod_ref[:, 3 * D_MODEL:4 * D_MODEL]
    sc2 = mod_ref[:, 4 * D_MODEL:5 * D_MODEL]
    xn = x_ref[...] + gt1 * o
    xo_ref[...] = xn
    h2_ref[...] = (_rms(xn, g2_ref[...]) * (1.0 + sc2) + sh2).astype(BF16)


def _merge(x, proj, ya, yb, yc, mod4, row_fn, layer, norm_g2, w_out):
    n = x.shape[0]
    tb = MERGE_TB
    tok = lambda c: pl.BlockSpec((tb, D_MODEL), lambda i: (i, c))
    return pl.pallas_call(
        _merge_kernel,
        grid=(n // tb,),
        in_specs=[
            tok(0), tok(7), tok(8), tok(9), tok(0), tok(0), tok(0),
            pl.BlockSpec((None, None, 1, 6 * D_MODEL), lambda i: (layer, row_fn(i * tb), 0, 0)),
            pl.BlockSpec((None, None, 1, D_MODEL), lambda i: (layer, 1, 0, 0)),
            pl.BlockSpec((None, D_MODEL, D_MODEL), lambda i: (layer, 0, 0)),
        ],
        out_specs=(tok(0), tok(0)),
        out_shape=(jax.ShapeDtypeStruct((n, D_MODEL), F32), jax.ShapeDtypeStruct((n, D_MODEL), BF16)),
        compiler_params=_params("parallel"),
        name="merge",
    )(x, proj, proj, proj, ya, yb, yc, mod4, norm_g2, w_out)


_CAND_PAIRS = tuple((j1, j2) for j1 in range(PEER_TOPK) for j2 in range(PEER_TOPK)
                    if (j1 + 1) * (j2 + 1) <= PEER_TOPK)
ROUTE_ACCS = 4
ROUTE_LAG = 1


def _scan_max(n_items, load, knock, store):
    per = -(-n_items // ROUTE_ACCS)
    accs = []
    for a0 in range(0, n_items, per):
        bv = bt = None
        for i in range(a0, min(a0 + per, n_items)):
            v, t = load(i)
            v = knock(i, v, t)
            store(i, v)
            if bv is None:
                bv, bt = v, t
            else:
                gt = v > bv
                bv = jnp.where(gt, v, bv)
                bt = jnp.where(gt, t, bt)
        accs.append((bv, bt))
    bv, bt = accs[0]
    for v, t in accs[1:]:
        gt = v > bv
        bv = jnp.where(gt, v, bv)
        bt = jnp.where(gt, t, bt)
    return bv, bt


def _route_kernel(h_ref, wq_ref, sk_ref, m_ref, q_scr, x_scr, ts_scr, ti_scr, c_scr, k_scr, eo_scr, go_scr,
                  pe_scr, pg_scr, *, n_blocks):
    tb = h_ref.shape[0]
    nk = PEER_NKEYS
    kd = PEER_DKEY // 2
    nlt = tb // LANES
    assert nlt == SUBLANES
    neg = -jnp.inf
    tile = lambda i: pl.ds(i * SUBLANES, SUBLANES)
    blk = pl.program_id(0)
    step = pl.program_id(1)
    live = blk < n_blocks

    @pl.when(jnp.logical_and(blk == 0, step == 0))
    def _():
        pe_scr[...] = jnp.zeros_like(pe_scr)
        pg_scr[...] = jnp.zeros_like(pg_scr)

    @pl.when(jnp.logical_and(live, step == 0))
    def _():
        for hd in range(PEER_HEADS):
            cols = slice(hd * PEER_DKEY, (hd + 1) * PEER_DKEY)
            q = jnp.dot(h_ref[...], wq_ref[:, cols].astype(BF16), preferred_element_type=F32).astype(BF16)
            q_scr[2 * hd] = q[:, :kd]
            q_scr[2 * hd + 1] = q[:, kd:]

    def gate_tasks():
        t0 = step * ROUTE_MTOK
        key_iota = lax.broadcasted_iota(I32, (nk, nk), 0).astype(BF16)
        zero = jnp.zeros((nk, nk), BF16)
        one = jnp.ones((nk, nk), BF16)
        m_flat = m_ref
        half = ROUTE_MTOK // 2
        for spare in range(half, ROUTE_MPITCH):
            m_flat[pl.ds(spare, nk, stride=ROUTE_MPITCH), :] = jnp.zeros((nk, nk), jnp.uint32)
        group_rows = {}
        bf16_bits = lambda v: pltpu.bitcast(v.astype(BF16).astype(F32), jnp.uint32)

        def pair_task(grp, pair):
            if pair == 0:
                vals = []
                for base in (0, half):
                    rows = pl.ds(pl.multiple_of(t0 + base + grp * SUBLANES, SUBLANES), SUBLANES)
                    er = pe_scr[rows, :]
                    vals.append(((er >> 7).astype(F32).astype(BF16), (er & (nk - 1)).astype(F32).astype(BF16),
                                 pg_scr[rows, :].astype(BF16)))
                group_rows[grp] = vals
            at, bt = [], []
            for i1r, i2r, gr in group_rows[grp]:
                at.append(jnp.where(key_iota == i1r[pair:pair + 1, :], gr[pair:pair + 1, :], zero))
                bt.append(jnp.where(key_iota == i2r[pair:pair + 1, :], one, zero))
            lhs = jnp.concatenate([jnp.concatenate([at[0], zero], axis=1),
                                   jnp.concatenate([zero, at[1]], axis=1)], axis=0)
            out = _nt_dot(lhs, jnp.concatenate(bt, axis=1))
            word = (bf16_bits(out[:nk, :]) >> 16) | (bf16_bits(out[nk:, :]) & jnp.uint32(0xFFFF0000))
            m_flat[pl.ds(grp * SUBLANES + pair, nk, stride=ROUTE_MPITCH), :] = word
            return out[:SUBLANES, :]

        return [functools.partial(pair_task, grp, pair)
                for grp in range(half // SUBLANES) for pair in range(SUBLANES)]

    def side(hs, between_rounds):
        sk = sk_ref[hs].astype(BF16)
        for j in range(nlt):
            sc = _nt_dot(sk, q_scr[hs, j * LANES:(j + 1) * LANES, :])
            x_scr[pl.ds(j, nk, stride=nlt), :] = sc
        prev = None
        for r in range(PEER_TOPK):
            load = lambda n: (x_scr[tile(n), :], jnp.full((SUBLANES, LANES), n, I32))
            if prev is None:
                knock = lambda n, v, t: v
                store = lambda n, v: None
            else:
                knock = lambda n, v, t, prev=prev: jnp.where(prev == n, neg, v)
                store = lambda n, v: x_scr.__setitem__((tile(n), slice(None)), v)
            bv, bi = _scan_max(nk, load, knock, store)
            ts_scr[hs, r] = bv
            ti_scr[hs, r] = bi
            anchor = between_rounds()
            prev = bi if anchor is None else bi + anchor

    sides_per_step = 2 * PEER_HEADS // ROUTE_STEPS

    @pl.when(live)
    def _():
        tasks = gate_tasks()
        per_round = -(-len(tasks) // (sides_per_step * PEER_TOPK))
        issued = []

        def between_rounds():
            issued.append([tasks.pop(0)() for _ in range(min(per_round, len(tasks)))])
            if len(issued) <= ROUTE_LAG or not issued[-1 - ROUTE_LAG]:
                return None
            return _tree(jnp.add, [jnp.where(t > 3e38, 1, 0) for t in issued[-1 - ROUTE_LAG]])

        for k in range(sides_per_step):
            side(step * sides_per_step + k, between_rounds)
        while tasks:
            tasks.pop(0)()

    @pl.when(blk == n_blocks)
    def _():
        for task in gate_tasks():
            task()

    def head(h, carry):
        for p, (j1, j2) in enumerate(_CAND_PAIRS):
            c_scr[tile(p), :] = ts_scr[2 * h, j1] + ts_scr[2 * h + 1, j2]
            k_scr[tile(p), :] = ((j1 * PEER_TOPK + j2) * (nk * nk) + ti_scr[2 * h, j1] * nk
                                 + ti_scr[2 * h + 1, j2])
        prev = None
        vals, exps = [], []
        for r in range(PEER_TOPK):
            load = lambda p: (c_scr[tile(p), :], k_scr[tile(p), :])
            if prev is None:
                knock = lambda p, v, t: v
                store = lambda p, v: None
            else:
                knock = lambda p, v, t, prev=prev: jnp.where(t == prev, neg, v)
                store = lambda p, v: c_scr.__setitem__((tile(p), slice(None)), v)
            bv, bk = _scan_max(len(_CAND_PAIRS), load, knock, store)
            vals.append(bv)
            exps.append(bk & (nk * nk - 1))
            prev = bk
        ps = [jnp.exp(v - vals[0]) for v in vals]
        z = _tree(jnp.add, ps)
        for r in range(PEER_TOPK):
            row = pl.ds(pl.multiple_of((h * PEER_TOPK + r) * SUBLANES, SUBLANES), SUBLANES)
            go_scr[row, :] = ps[r] / z
            eo_scr[row, :] = exps[r]
        return carry

    @pl.when(jnp.logical_and(live, step == ROUTE_STEPS - 1))
    def _():
        lax.fori_loop(0, PEER_HEADS, head, 0)
        nsel = PEER_HEADS * PEER_TOPK
        for j in range(nlt):
            rows = slice(j * LANES, (j + 1) * LANES)
            pe_scr[rows, :] = jnp.transpose(eo_scr[pl.ds(j, nsel, stride=nlt), :])
            pg_scr[rows, :] = jnp.transpose(go_scr[pl.ds(j, nsel, stride=nlt), :])


def _route(h2, layer, wq, subkeys):
    n = h2.shape[0]
    tb = ROUTE_TB
    n_blocks = n // tb
    nsel = PEER_HEADS * PEER_TOPK
    kd = PEER_DKEY // 2
    nk = PEER_NKEYS
    tile_rows = lambda count: count * SUBLANES
    gates = pl.pallas_call(
        functools.partial(_route_kernel, n_blocks=n_blocks),
        grid=(n_blocks + 1, ROUTE_STEPS),
        in_specs=[
            pl.BlockSpec((tb, D_MODEL), lambda i, s: (jnp.minimum(i, n_blocks - 1), 0)),
            pl.BlockSpec((None, D_MODEL, PEER_HEADS * PEER_DKEY), lambda i, s: (layer, 0, 0)),
            pl.BlockSpec((None, 2 * PEER_HEADS, PEER_NKEYS, kd), lambda i, s: (layer, 0, 0, 0)),
        ],
        out_specs=pl.BlockSpec((None, nk * ROUTE_MPITCH, nk),
                               lambda i, s: (jnp.maximum((i - 1) * ROUTE_STEPS + s, 0), 0, 0)),
        out_shape=jax.ShapeDtypeStruct((n // ROUTE_MTOK, nk * ROUTE_MPITCH, nk), jnp.uint32),
        scratch_shapes=[
            pltpu.VMEM((2 * PEER_HEADS, tb, kd), BF16),
            pltpu.VMEM((tile_rows(PEER_NKEYS), LANES), F32),
            pltpu.VMEM((2 * PEER_HEADS, PEER_TOPK, SUBLANES, LANES), F32),
            pltpu.VMEM((2 * PEER_HEADS, PEER_TOPK, SUBLANES, LANES), I32),
            pltpu.VMEM((tile_rows(len(_CAND_PAIRS)), LANES), F32),
            pltpu.VMEM((tile_rows(len(_CAND_PAIRS)), LANES), I32),
            pltpu.VMEM((tile_rows(nsel), LANES), I32),
            pltpu.VMEM((tile_rows(nsel), LANES), F32),
            pltpu.VMEM((tb, nsel), I32),
            pltpu.VMEM((tb, nsel), F32),
        ],
        compiler_params=_params("arbitrary", "arbitrary"),
        name="route",
    )(h2, wq, subkeys)
    return gates.reshape(n // ROUTE_MTOK, nk, ROUTE_MPITCH, nk)


def _peer_kernel(x_ref, h_ref, m_ref, mod_ref, fg_ref, u_ref, v_ref, o_ref, p0_ref, p1_ref, acc_ref,
                 *, final, n_chunks):
    c = pl.program_id(1)

    def step(prev_ref, cur_ref):
        n_sub = PEER_CH // PEER_SUB
        nw = D_MODEL // n_sub
        for sub in range(n_sub):
            if cur_ref is not None:
                rows = slice(sub * PEER_SUB, (sub + 1) * PEER_SUB)
                s = _gelu(_nt_dot(h_ref[...], u_ref[rows, :]))
                per = PEER_SUB // PEER_NKEYS
                half = ROUTE_MTOK // 2
                cols_m = []
                for a in range(per):
                    parts = []
                    for g in range(h_ref.shape[0] // ROUTE_MTOK):
                        word = m_ref[g, sub * per + a, :half, :]
                        parts.append(pltpu.bitcast(word << 16, F32))
                        parts.append(pltpu.bitcast(word & jnp.uint32(0xFFFF0000), F32))
                    cols_m.append(jnp.concatenate(parts, axis=0))
                m = jnp.concatenate(cols_m, axis=1)
                cur_ref[:, rows] = (m * s).astype(BF16)
            if prev_ref is not None:
                cols = slice(sub * nw, (sub + 1) * nw)
                acc_ref[:, cols] += jnp.dot(prev_ref[...], v_ref[:, cols], preferred_element_type=F32)

    stage = (p0_ref, p1_ref)
    steady = jnp.logical_and(c > 0, c < n_chunks)

    @pl.when(c == 0)
    def _():
        acc_ref[...] = jnp.zeros_like(acc_ref)
        step(None, stage[0])

    for parity in range(2):
        @pl.when(jnp.logical_and(steady, c % 2 == parity))
        def _(parity=parity):
            step(stage[1 - parity], stage[parity])

    @pl.when(c == n_chunks)
    def _():
        step(stage[(n_chunks - 1) % 2], None)
        gt2 = mod_ref[:, 5 * D_MODEL:6 * D_MODEL]
        xn = x_ref[...] + gt2 * acc_ref[...]
        o_ref[...] = _rms(xn, fg_ref[...]) if final else xn


def _peer(x, h2, gates, mod4, row_fn, layer, final_g, u_tab, v_tab, final):
    n = x.shape[0]
    tb = PEER_TB
    n_chunks = u_tab.shape[1] // PEER_CH
    return pl.pallas_call(
        functools.partial(_peer_kernel, final=final, n_chunks=n_chunks),
        grid=(n // tb, n_chunks + 1),
        in_specs=[
            pl.BlockSpec((tb, D_MODEL), lambda i, c: (i, 0)),
            pl.BlockSpec((tb, D_MODEL), lambda i, c: (i, 0)),
            pl.BlockSpec((tb // ROUTE_MTOK, PEER_CH // PEER_NKEYS, ROUTE_MPITCH, PEER_NKEYS),
                         lambda i, c: (i, jnp.minimum(c, n_chunks - 1), 0, 0)),
            pl.BlockSpec((None, None, 1, 6 * D_MODEL), lambda i, c: (layer, row_fn(i * tb), 0, 0)),
            pl.BlockSpec((1, D_MODEL), lambda i, c: (0, 0)),
            pl.BlockSpec((None, PEER_CH, D_MODEL), lambda i, c: (layer, jnp.minimum(c, n_chunks - 1), 0)),
            pl.BlockSpec((None, PEER_CH, D_MODEL), lambda i, c: (layer, jnp.maximum(c - 1, 0), 0)),
        ],
        out_specs=pl.BlockSpec((tb, D_MODEL), lambda i, c: (i, 0)),
        out_shape=jax.ShapeDtypeStruct((n, D_MODEL), F32),
        scratch_shapes=[pltpu.VMEM((tb, PEER_CH), BF16),
                        pltpu.VMEM((tb, PEER_CH), BF16),
                        pltpu.VMEM((tb, D_MODEL), F32)],
        compiler_params=_params("parallel", "arbitrary"),
        name="peer",
    )(x, h2, gates, mod4, final_g, u_tab, v_tab)


def kernel(x_prompt, x_sample, state_ret, c, c_ctx, w_mod, b_mod, norm_g, w_in, w_out, gmlp_ws, gmlp_b,
           gmlp_ln_g, pool_w, pool_scale, ret_decay, peer_wq, peer_subkeys, peer_u, peer_v, final_norm_g):
    batch, seq, d = x_prompt.shape
    dec_batch, dec_seq, _ = x_sample.shape
    depth = w_mod.shape[0]
    assert d == D_MODEL and dec_batch + 1 <= MOD_ROWS and dec_seq % GRID_W == 0

    cond = jnp.zeros((MOD_ROWS, d), F32).at[0].set(c_ctx).at[1:1 + dec_batch].set(c)
    mod = _modulation(cond, w_mod, b_mod)
    mod4 = mod.reshape(depth, MOD_ROWS, 1, 6 * d)
    ctx_row = lambda tok0: 0
    lat_row = lambda tok0: 1 + tok0 // dec_seq

    xc = x_prompt.reshape(batch * seq, d)
    xs = x_sample.reshape(dec_batch * dec_seq, d)
    fg = final_norm_g.reshape(1, d)
    norm_g4 = norm_g.reshape(depth, 2, 1, d)
    bs_t = jnp.swapaxes(gmlp_b, 1, 2)
    ln_g = gmlp_ln_g.reshape(depth, 1, d)
    ps = pool_scale.reshape(depth, 1, d)
    sk = peer_subkeys.reshape(depth, 2 * PEER_HEADS, PEER_NKEYS, PEER_DKEY // 2)
    u_tab = peer_u.astype(BF16)
    v_tab = peer_v.astype(BF16)
    new_states = None
    for l in range(depth):
        last = l == depth - 1

        def path(x, nseq, n, row_fn, grid, s0):
            proj = _proj(x, mod4, row_fn, l, norm_g4, w_in)
            ya = _gmlp(proj, l, gmlp_ws, bs_t, ln_g)
            yb = _pool(proj, l, pool_w, ps, nseq, n, grid)
            ret = _retention(proj, ret_decay, nseq, n, s0, l, depth, new_states)
            yc, s_fin = (ret, None) if s0 is not None else ret
            xn, h2 = _merge(x, proj, ya, yb, yc, mod4, row_fn, l, norm_g4, w_out)
            gates = _route(h2, l, peer_wq, sk)
            return _peer(xn, h2, gates, mod4, row_fn, l, fg, u_tab, v_tab, last), s_fin

        xc, new_states = path(xc, batch, seq, ctx_row, False, None)
        xs, _ = path(xs, dec_batch, dec_seq, lat_row, True, state_ret)

    return (xc.reshape(batch, seq, d), xs.reshape(dec_batch, dec_seq, d), new_states.astype(x_prompt.dtype))
```
